```python
import math
import jax, jax.numpy as jnp
from jax import lax
import numpy as np

D_MODEL = 2048
BATCH = 4
SEQ = 2048
DEPTH = 4
DEC_BATCH = 128
DEC_SEQ = 4
PAST_LEN = 16384
PAGE_SIZE = 128

HEAD_DIM = 128
H_A = 4
H_B = 4
H_C = 8
D_A = H_A * HEAD_DIM
D_B = H_B * HEAD_DIM
D_C = H_C * HEAD_DIM
D_MIX = D_A + D_B + D_C
D_FF = 4 * D_MODEL
CONV_W = 4
CHUNK = 64
ROPE_BASE = 10000.0
GATE_SOFTCAP = 15.0
NORM_EPS = 1e-6
IN_SPLITS = (D_A, D_A, D_A, D_A, H_A, H_A, D_B, D_B, D_B, D_B, D_C, D_C, D_C, D_C, H_C, H_C)
N_IN = 4 * D_A + 2 * H_A + 4 * D_B + 4 * D_C + 2 * H_C

kernel_name = 'hybrid_mlstm_retention_gdn_decoder_step'


def _rmsnorm(x, g):
    xf = x.astype(jnp.float32)
    xf = xf * lax.rsqrt(jnp.mean(xf * xf, axis=-1, keepdims=True) + NORM_EPS)
    return (xf * g.astype(jnp.float32)).astype(x.dtype)


def _head_norm(x, g, center):
    if center:
        x = x - jnp.mean(x, axis=-1, keepdims=True)
    x = x * lax.rsqrt(jnp.mean(x * x, axis=-1, keepdims=True) + NORM_EPS)
    x = x * g.astype(jnp.float32).reshape(-1, x.shape[-1])
    return x.reshape(x.shape[:2] + (-1,))


def _l2norm(x):
    return x * lax.rsqrt(jnp.sum(x * x, axis=-1, keepdims=True) + NORM_EPS)


def _rotary(x, pos):
    half = x.shape[-1] // 2
    inv = ROPE_BASE ** (-jnp.arange(half, dtype=jnp.float32) / half)
    ang = pos.astype(jnp.float32)[:, None] * inv[None, :]
    cos = jnp.cos(ang)[None, :, None, :]
    sin = jnp.sin(ang)[None, :, None, :]
    x1, x2 = x[..., :half], x[..., half:]
    return jnp.concatenate([x1 * cos - x2 * sin, x1 * sin + x2 * cos], axis=-1)


def _to_chunks(x, L):
    B, T = x.shape[:2]
    x = x.reshape((B, T // L, L) + x.shape[2:])
    return jnp.moveaxis(jnp.moveaxis(x, 1, 0), 2, 3)


def _from_chunks(y):
    y = jnp.moveaxis(jnp.moveaxis(y, 0, 1), 2, 3)
    B, nC, L = y.shape[:3]
    return y.reshape((B, nC * L) + y.shape[3:])


def _mlstm(q, k, v, ig, lf, C0, n0, m0):
    L = math.gcd(q.shape[1], CHUNK)
    incl = jnp.tril(jnp.ones((L, L), dtype=bool))

    def chunk(carry, xs):
        C, n, mp = carry
        qc, kc, vc, ic, fc = xs
        F = jnp.cumsum(fc, axis=-1)
        m = F + jnp.maximum(mp[..., None], lax.cummax(ic - F, axis=2))
        logw = F[..., :, None] - F[..., None, :] + ic[..., None, :] - m[..., :, None]
        s = jnp.einsum('bhid,bhjd->bhij', qc, kc) * jnp.exp(jnp.where(incl, logw, -jnp.inf))
        inter = jnp.exp(F + mp[..., None] - m)
        num = inter[..., None] * jnp.einsum('bhid,bhde->bhie', qc, C) + jnp.einsum('bhij,bhje->bhie', s, vc)
        den = inter * jnp.einsum('bhid,bhd->bhi', qc, n) + jnp.sum(s, axis=-1)
        h = num / jnp.maximum(jnp.abs(den), jnp.exp(-m))[..., None]
        m_new = m[..., -1]
        wl = jnp.exp(ic + F[..., -1:] - F - m_new[..., None])
        dec = jnp.exp(F[..., -1] + mp - m_new)
        kw = kc * wl[..., None]
        C = dec[..., None, None] * C + jnp.einsum('bhjd,bhje->bhde', kw, vc)
        n = dec[..., None] * n + jnp.sum(kw, axis=2)
        return (C, n, m_new), h

    xs = tuple(_to_chunks(a, L) for a in (q, k, v, ig, lf))
    (C, n, m), h = lax.scan(chunk, (C0, n0, m0), xs)
    return _from_chunks(h), C, n, m


def _retention(q, k, v, S0):
    L = math.gcd(q.shape[1], CHUNK)
    lg = jnp.log1p(-jnp.exp2(-5.0 - jnp.arange(H_B, dtype=jnp.float32)))
    i = jnp.arange(L, dtype=jnp.float32)
    diff = i[:, None] - i[None, :]
    D = jnp.where(diff >= 0, jnp.exp(jnp.maximum(diff, 0.0) * lg[:, None, None]), 0.0)
    q_dec = jnp.exp((i + 1.0) * lg[:, None])
    k_dec = jnp.exp((L - 1.0 - i) * lg[:, None])
    s_dec = jnp.exp(L * lg)

    def chunk(S, xs):
        qc, kc, vc = xs
        inner = jnp.einsum('bhid,bhjd->bhij', qc, kc) * D
        o = jnp.einsum('bhij,bhje->bhie', inner, vc) + q_dec[..., None] * jnp.einsum('bhid,bhde->bhie', qc, S)
        S = s_dec[:, None, None] * S + jnp.einsum('bhjd,bhje->bhde', kc * k_dec[..., None], vc)
        return S, o

    S, o = lax.scan(chunk, S0, tuple(_to_chunks(a, L) for a in (q, k, v)))
    return _from_chunks(o), S


def _gated_delta(q, k, v, g, beta, S0):
    L = math.gcd(q.shape[1], CHUNK)
    ar = jnp.arange(L)
    strict = ar[:, None] > ar[None, :]
    incl = ar[:, None] >= ar[None, :]
    eye = jnp.eye(L, dtype=jnp.float32)

    def chunk(S, xs):
        qc, kc, vc, gc, bc = xs
        G = jnp.cumsum(gc, axis=-1)
        dec_in = jnp.exp(jnp.where(incl, G[..., :, None] - G[..., None, :], -jnp.inf))
        A = jnp.where(strict, dec_in, 0.0) * bc[..., :, None] * jnp.einsum('bhid,bhjd->bhij', kc, kc)
        gam = jnp.exp(G)
        rhs = jnp.concatenate([(bc * gam)[..., None] * kc, bc[..., None] * vc], axis=-1)
        sol = lax.linalg.triangular_solve(eye + A, rhs, left_side=True, lower=True, unit_diagonal=True)
        W, Ut = sol[..., :HEAD_DIM], sol[..., HEAD_DIM:]
        U = Ut - jnp.einsum('bhid,bhde->bhie', W, S)
        o = gam[..., None] * jnp.einsum('bhid,bhde->bhie', qc, S) + jnp.einsum(
            'bhij,bhje->bhie', jnp.einsum('bhid,bhjd->bhij', qc, kc) * dec_in, U)
        S = jnp.exp(G[..., -1])[..., None, None] * S + jnp.einsum(
            'bhjd,bhje->bhde', kc * jnp.exp(G[..., -1:] - G)[..., None], U)
        return S, o

    S, o = lax.scan(chunk, S0, tuple(_to_chunks(a, L) for a in (q, k, v, g, beta)))
    return _from_chunks(o), S


def _token_mixers(h, pos, st, lp):
    f32 = jnp.float32
    B, T, _ = h.shape
    C0, n0, m0, R0, G0, conv0 = st
    proj = jnp.einsum('btd,de->bte', h, lp['w_in']).astype(f32)
    idx = np.cumsum(IN_SPLITS)[:-1].tolist()
    (qa, ka, va, oa, ia, fa, qb, kb, vb, gb, qc, kc, vc, gcz, ac, bcz) = jnp.split(proj, idx, axis=-1)
    heads = lambda t, H: t.reshape(B, T, H, HEAD_DIM)
    gate_b = lp['mlstm_gate_bias'].astype(f32)

    ig = GATE_SOFTCAP * jnp.tanh((ia + gate_b[:H_A]) / GATE_SOFTCAP)
    lf = jax.nn.log_sigmoid(GATE_SOFTCAP * jnp.tanh((fa + gate_b[H_A:]) / GATE_SOFTCAP))
    ha, C, n, m = _mlstm(heads(qa, H_A), heads(ka, H_A) * HEAD_DIM ** -0.5, heads(va, H_A), ig, lf,
                         C0.astype(f32), n0.astype(f32), m0.astype(f32))
    ya = jax.nn.sigmoid(oa) * _head_norm(ha, lp['norm_mlstm'], False)

    qr = _rotary(heads(qb, H_B), pos)
    kr = _rotary(heads(kb, H_B), pos) * HEAD_DIM ** -0.5
    hb, R = _retention(qr, kr, heads(vb, H_B), R0.astype(f32))
    yb = jax.nn.silu(gb) * _head_norm(hb, lp['norm_ret'], True)

    xc = jnp.concatenate([conv0.astype(f32), jnp.concatenate([qc, kc, vc], axis=-1)], axis=1)
    cw = lp['gdn_conv_w'].astype(f32)
    conv = sum(xc[:, w:w + T] * cw[w] for w in range(CONV_W))
    conv_new = xc[:, xc.shape[1] - (CONV_W - 1):]
    q3, k3, v3 = jnp.split(jax.nn.silu(conv), 3, axis=-1)
    qg = _l2norm(heads(q3, H_C)) * HEAD_DIM ** -0.5
    kg = _l2norm(heads(k3, H_C))
    g = -jnp.exp(lp['gdn_A_log'].astype(f32)) * jax.nn.softplus(ac + lp['gdn_dt_bias'].astype(f32))
    beta = jax.nn.sigmoid(bcz)
    hc, Gs = _gated_delta(qg, kg, heads(v3, H_C), g, beta, G0.astype(f32))
    yc = jax.nn.silu(gcz) * _head_norm(hc, lp['norm_gdn'], False)

    y = jnp.concatenate([ya, yb, yc], axis=-1).astype(h.dtype)
    y = jnp.einsum('bte,ed->btd', y, lp['w_out'])
    return y, (C, n, m, R, Gs, conv_new)


def _layer(x, pos, st, lp):
    h = _rmsnorm(x, lp['norm_mix_pre'])
    y, new_st = _token_mixers(h, pos, st, lp)
    x = x + _rmsnorm(y, lp['norm_mix_post'])
    h = _rmsnorm(x, lp['norm_mlp_pre'])
    u = jnp.square(jax.nn.relu(jnp.einsum('btd,df->btf', h, lp['w_up'])))
    x = x + _rmsnorm(jnp.einsum('btf,fd->btd', u, lp['w_down']), lp['norm_mlp_post'])
    return x, new_st


def setup_inputs(seed: int = 0) -> dict:
    key = jax.random.key(seed)
    ks = jax.random.split(key, 24)
    f32 = jnp.float32
    nrm = lambda k, shape, s: jax.random.normal(k, shape, f32) * s
    gain = lambda k, n: 1.0 + 0.02 * jax.random.normal(k, (DEPTH, n), f32)
    dt = jnp.exp(jax.random.uniform(ks[20], (DEPTH, H_C), f32, math.log(1e-3), math.log(1e-1)))
    gate_bias = jnp.concatenate([
        -1.0 + 0.1 * jax.random.normal(ks[21], (DEPTH, H_A), f32),
        jnp.linspace(3.0, 6.0, H_A, dtype=f32)[None, :] + 0.1 * jax.random.normal(ks[22], (DEPTH, H_A), f32)], axis=-1)
    return {
        'x_prompt': nrm(ks[0], (BATCH, SEQ, D_MODEL), 1.0),
        'x_sample': nrm(ks[1], (DEC_BATCH, DEC_SEQ, D_MODEL), 1.0),
        'state_mlstm_C': nrm(ks[2], (DEPTH, DEC_BATCH, H_A, HEAD_DIM, HEAD_DIM), 0.3),
        'state_mlstm_n': nrm(ks[3], (DEPTH, DEC_BATCH, H_A, HEAD_DIM), 0.3),
        'state_mlstm_m': nrm(ks[4], (DEPTH, DEC_BATCH, H_A), 0.5),
        'state_ret_S': nrm(ks[5], (DEPTH, DEC_BATCH, H_B, HEAD_DIM, HEAD_DIM), 0.5),
        'state_gdn_S': nrm(ks[6], (DEPTH, DEC_BATCH, H_C, HEAD_DIM, HEAD_DIM), 0.1),
        'state_gdn_conv': nrm(ks[7], (DEPTH, DEC_BATCH, CONV_W - 1, 3 * D_C), 1.0),
        'norm_mix_pre': gain(ks[8], D_MODEL),
        'norm_mix_post': gain(ks[9], D_MODEL),
        'norm_mlp_pre': gain(ks[10], D_MODEL),
        'norm_mlp_post': gain(ks[11], D_MODEL),
        'w_in': nrm(ks[12], (DEPTH, D_MODEL, N_IN), D_MODEL ** -0.5),
        'mlstm_gate_bias': gate_bias,
        'gdn_conv_w': nrm(ks[13], (DEPTH, CONV_W, 3 * D_C), CONV_W ** -0.5),
        'gdn_A_log': jnp.log(jax.random.uniform(ks[14], (DEPTH, H_C), f32, 1.0, 16.0)),
        'gdn_dt_bias': dt + jnp.log(-jnp.expm1(-dt)),
        'norm_mlstm': gain(ks[15], D_A),
        'norm_ret': gain(ks[16], D_B),
        'norm_gdn': gain(ks[17], HEAD_DIM),
        'w_out': nrm(ks[18], (DEPTH, D_MIX, D_MODEL), D_MIX ** -0.5),
        'w_up': nrm(ks[19], (DEPTH, D_MODEL, D_FF), D_MODEL ** -0.5),
        'w_down': nrm(ks[23], (DEPTH, D_FF, D_MODEL), D_FF ** -0.5),
    }


def reference(x_prompt, x_sample, state_mlstm_C, state_mlstm_n, state_mlstm_m, state_ret_S, state_gdn_S,
              state_gdn_conv, norm_mix_pre, norm_mix_post, norm_mlp_pre, norm_mlp_post, w_in, mlstm_gate_bias,
              gdn_conv_w, gdn_A_log, gdn_dt_bias, norm_mlstm, norm_ret, norm_gdn, w_out, w_up, w_down):
    f32 = jnp.float32
    bp, tp = x_prompt.shape[:2]
    pos_p = jnp.arange(tp, dtype=jnp.int32)
    pos_s = PAST_LEN + jnp.arange(x_sample.shape[1], dtype=jnp.int32)
    zero_st = (jnp.zeros((bp, H_A, HEAD_DIM, HEAD_DIM), f32), jnp.zeros((bp, H_A, HEAD_DIM), f32),
               jnp.zeros((bp, H_A), f32), jnp.zeros((bp, H_B, HEAD_DIM, HEAD_DIM), f32),
               jnp.zeros((bp, H_C, HEAD_DIM, HEAD_DIM), f32), jnp.zeros((bp, CONV_W - 1, 3 * D_C), f32))
    xp, xs = x_prompt, x_sample
    p_states, s_states = [], []
    for l in range(DEPTH):
        lp = {'norm_mix_pre': norm_mix_pre[l], 'norm_mix_post': norm_mix_post[l],
              'norm_mlp_pre': norm_mlp_pre[l], 'norm_mlp_post': norm_mlp_post[l],
              'w_in': w_in[l], 'mlstm_gate_bias': mlstm_gate_bias[l], 'gdn_conv_w': gdn_conv_w[l],
              'gdn_A_log': gdn_A_log[l], 'gdn_dt_bias': gdn_dt_bias[l], 'norm_mlstm': norm_mlstm[l],
              'norm_ret': norm_ret[l], 'norm_gdn': norm_gdn[l], 'w_out': w_out[l],
              'w_up': w_up[l], 'w_down': w_down[l]}
        xp, st_p = _layer(xp, pos_p, zero_st, lp)
        st_in = (state_mlstm_C[l], state_mlstm_n[l], state_mlstm_m[l], state_ret_S[l], state_gdn_S[l], state_gdn_conv[l])
        xs, st_s = _layer(xs, pos_s, st_in, lp)
        p_states.append(st_p)
        s_states.append(st_s)
    stk = lambda sts, j: jnp.stack([s[j] for s in sts], axis=0)
    return (xp, xs,
            stk(p_states, 0), stk(p_states, 1), stk(p_states, 2), stk(p_states, 3), stk(p_states, 4), stk(p_states, 5),
            stk(s_states, 0), stk(s_states, 1), stk(s_states, 2), stk(s_states, 3), stk(s_states, 4), stk(s_states, 5))
```

```python
import functools
import math

import numpy as np
import jax
import jax.numpy as jnp
from jax import lax
from jax.experimental import pallas as pl
from jax.experimental.pallas import tpu as pltpu

f32 = jnp.float32
bf16 = jnp.bfloat16

D_MODEL = 2048
HEAD_DIM = 128
H_A, H_B, H_C = 4, 4, 8
D_A, D_B, D_C = H_A * HEAD_DIM, H_B * HEAD_DIM, H_C * HEAD_DIM
D_FF = 4 * D_MODEL
CONV_W = 4
PAST_LEN = 16384
ROPE_BASE = 10000.0
GATE_SOFTCAP = 15.0
NORM_EPS = 1e-6
QK_SCALE = HEAD_DIM ** -0.5

COL_A = 0
COL_B = 4 * D_A
COL_C = COL_B + 4 * D_B
COL_G = COL_C + 4 * D_C
N_PROJ = COL_G + 128
LANE_I, LANE_F, LANE_DECAY, LANE_BETA = 0, H_A, 2 * H_A, 2 * H_A + H_C

PROMPT_CHUNK = 64
SAMPLE_BLOCK = 8
NEG_BIG = -1e30
VMEM_LIMIT = 56 * 1024 * 1024


def _cparams(sem):
    return pltpu.CompilerParams(dimension_semantics=sem, vmem_limit_bytes=VMEM_LIMIT)


def _rms(x, g):
    return x * lax.rsqrt(jnp.mean(x * x, axis=-1, keepdims=True) + NORM_EPS) * g


def _rmsnorm_kernel(x_ref, g_ref, o_ref):
    o_ref[...] = _rms(x_ref[...], g_ref[...]).astype(o_ref.dtype)


def _rmsnorm_rows(x, g, tm):
    m, d = x.shape
    return pl.pallas_call(
        _rmsnorm_kernel,
        grid=(m // tm,),
        in_specs=[pl.BlockSpec((tm, d), lambda i: (i, 0)), pl.BlockSpec((1, d), lambda i: (0, 0))],
        out_specs=pl.BlockSpec((tm, d), lambda i: (i, 0)),
        out_shape=jax.ShapeDtypeStruct((m, d), bf16),
        compiler_params=_cparams(("parallel",)),
        name="rmsnorm_in",
    )(x, g.reshape(1, d))


def _mm_kernel(a_ref, w_ref, o_ref):
    o_ref[...] = jnp.dot(a_ref[...], w_ref[...], preferred_element_type=f32)


def _matmul_in(h, w, layer, tm, tn):
    m, k = h.shape
    n = w.shape[2]
    return pl.pallas_call(
        _mm_kernel,
        grid=(n // tn, m // tm),
        in_specs=[pl.BlockSpec((tm, k), lambda j, i: (i, 0)), pl.BlockSpec((None, k, tn), lambda j, i: (layer, 0, j))],
        out_specs=pl.BlockSpec((tm, tn), lambda j, i: (i, j)),
        out_shape=jax.ShapeDtypeStruct((m, n), f32),
        compiler_params=_cparams(("parallel", "arbitrary")),
        name="proj_in",
    )(h, w)


def _mm_up_kernel(a_ref, w_ref, o_ref, wb_ref):
    @pl.when(pl.program_id(1) == 0)
    def _():
        wb_ref[...] = w_ref[...].astype(bf16)

    z = jnp.maximum(jnp.dot(a_ref[...], wb_ref[...], preferred_element_type=f32), 0.0)
    o_ref[...] = (z * z).astype(o_ref.dtype)


def _matmul_up(h, w, layer, tm, tn):
    m, k = h.shape
    n = w.shape[2]
    return pl.pallas_call(
        _mm_up_kernel,
        grid=(n // tn, m // tm),
        in_specs=[pl.BlockSpec((tm, k), lambda j, i: (i, 0)), pl.BlockSpec((None, k, tn), lambda j, i: (layer, 0, j))],
        out_specs=pl.BlockSpec((tm, tn), lambda j, i: (i, j)),
        out_shape=jax.ShapeDtypeStruct((m, n), bf16),
        scratch_shapes=[pltpu.VMEM((k, tn), bf16)],
        compiler_params=_cparams(("parallel", "arbitrary")),
        name="mlp_up",
    )(h, w)


def _residual_epilogue(z, x_ref, gpost_ref, gnext_ref, xo_ref, ho_ref):
    x_new = x_ref[...] + _rms(z, gpost_ref[...])
    xo_ref[...] = x_new
    ho_ref[...] = _rms(x_new, gnext_ref[...]).astype(ho_ref.dtype)


def _mm_out_kernel(a_ref, w_ref, x_ref, gpost_ref, gnext_ref, xo_ref, ho_ref):
    z = jnp.dot(a_ref[...], w_ref[...], preferred_element_type=f32)
    _residual_epilogue(z, x_ref, gpost_ref, gnext_ref, xo_ref, ho_ref)


def _matmul_out(y, w, layer, x, g_post, g_next, tm):
    m, k = y.shape
    d = w.shape[2]
    row = lambda i: (i, 0)
    fixed = lambda i: (0, 0)
    return pl.pallas_call(
        _mm_out_kernel,
        grid=(m // tm,),
        in_specs=[pl.BlockSpec((tm, k), row), pl.BlockSpec((None, k, d), lambda i: (layer, 0, 0)),
                  pl.BlockSpec((tm, d), row),
                  pl.BlockSpec((1, d), fixed), pl.BlockSpec((1, d), fixed)],
        out_specs=[pl.BlockSpec((tm, d), row), pl.BlockSpec((tm, d), row)],
        out_shape=[jax.ShapeDtypeStruct((m, d), f32), jax.ShapeDtypeStruct((m, d), bf16)],
        compiler_params=_cparams(("parallel",)),
        name="proj_out",
    )(y, w, x, g_post.reshape(1, d), g_next.reshape(1, d))


def _mm_down_kernel(a_ref, w_ref, x_ref, gpost_ref, gnext_ref, xo_ref, ho_ref, acc_ref):
    kk = pl.program_id(1)

    @pl.when(kk == 0)
    def _():
        acc_ref[...] = jnp.zeros_like(acc_ref)

    acc_ref[...] += jnp.dot(a_ref[...], w_ref[...], preferred_element_type=f32)

    @pl.when(kk == pl.num_programs(1) - 1)
    def _():
        _residual_epilogue(acc_ref[...], x_ref, gpost_ref, gnext_ref, xo_ref, ho_ref)


def _matmul_down(u, w, layer, x, g_post, g_next, tm, tk):
    m, k = u.shape
    d = w.shape[2]
    row = lambda i, kk: (i, 0)
    fixed = lambda i, kk: (0, 0)
    return pl.pallas_call(
        _mm_down_kernel,
        grid=(m // tm, k // tk),
        in_specs=[pl.BlockSpec((tm, tk), lambda i, kk: (i, kk)),
                  pl.BlockSpec((None, tk, d), lambda i, kk: (layer, kk, 0)),
                  pl.BlockSpec((tm, d), row), pl.BlockSpec((1, d), fixed), pl.BlockSpec((1, d), fixed)],
        out_specs=[pl.BlockSpec((tm, d), row), pl.BlockSpec((tm, d), row)],
        out_shape=[jax.ShapeDtypeStruct((m, d), f32), jax.ShapeDtypeStruct((m, d), bf16)],
        scratch_shapes=[pltpu.VMEM((tm, d), f32)],
        compiler_params=_cparams(("parallel", "arbitrary")),
        name="mlp_down",
    )(u, w, x, g_post.reshape(1, d), g_next.reshape(1, d))


def _dot(a, b):
    return jnp.dot(a, b, preferred_element_type=f32)


def _dot_tb(a, b):
    return lax.dot_general(a, b, (((1,), (1,)), ((), ())), preferred_element_type=f32)


def _dot_ta(a, b):
    return lax.dot_general(a, b, (((0,), (0,)), ((), ())), preferred_element_type=f32)


def _lane_sum(x):
    return jnp.sum(x, axis=-1, keepdims=True)


def _softplus(z):
    return jnp.maximum(z, 0.0) + jnp.log1p(jnp.exp(-jnp.abs(z)))


def _gate_transform(gt, gp):
    z = gt + gp[0:1, :]
    capped = GATE_SOFTCAP * jnp.tanh(z / GATE_SOFTCAP)
    log_f = -_softplus(-capped)
    decay = -jnp.exp(gp[1:2, :]) * _softplus(z)
    beta = jax.nn.sigmoid(gt)
    return capped, log_f, decay, beta


def _head_norm(x, gain, center):
    if center:
        x = x - jnp.mean(x, axis=-1, keepdims=True)
    return x * lax.rsqrt(jnp.mean(x * x, axis=-1, keepdims=True) + NORM_EPS) * gain


def _l2norm(x):
    return x * lax.rsqrt(_lane_sum(x * x) + NORM_EPS)


def _silu(x):
    return x * jax.nn.sigmoid(x)


def _rotary(x, cos, sin_signed):
    return x * cos + pltpu.roll(x, HEAD_DIM // 2, 1) * sin_signed


def _retention_log_gamma():
    return np.log1p(-np.exp2(-5.0 - np.arange(H_B, dtype=np.float64)))


def _rope_tables(pos):
    half = HEAD_DIM // 2
    inv = ROPE_BASE ** (-np.arange(half, dtype=np.float64) / half)
    ang = np.asarray(pos, dtype=np.float64)[:, None] * inv[None, :]
    cos = np.concatenate([np.cos(ang), np.cos(ang)], axis=-1)
    sin = np.concatenate([-np.sin(ang), np.sin(ang)], axis=-1)
    return cos.astype(np.float32), sin.astype(np.float32)


def _scan_rows(x, row, length, op, fill):
    s = 1
    while s < length:
        x = op(x, jnp.where(row >= s, pltpu.roll(x, s, 0), fill))
        s *= 2
    return x


def _rows_to_lanes(x, length):
    if length < 128:
        x = jnp.concatenate([x, jnp.zeros((128 - length, 128), x.dtype)], axis=0)
    return x.T[:, :length]


def _prompt_mixer_kernel(L, proj_ref, cos_ref, sin_ref, rd_ref, rq_ref, rk_ref, gp_ref, cw_ref,
                         na_ref, nb_ref, nc_ref,
                         y_ref, c_ref, n_ref, m_ref, r_ref, g_ref, conv_ref, ext_ref):
    c = pl.program_id(1)

    @pl.when(c == 0)
    def _():
        c_ref[...] = jnp.zeros_like(c_ref)
        n_ref[...] = jnp.zeros_like(n_ref)
        m_ref[...] = jnp.zeros_like(m_ref)
        r_ref[...] = jnp.zeros_like(r_ref)
        g_ref[...] = jnp.zeros_like(g_ref)
        ext_ref[0:8, :] = jnp.zeros((8, 3 * D_C), f32)

    row = lax.broadcasted_iota(jnp.int32, (L, 128), 0)
    lane = lax.broadcasted_iota(jnp.int32, (L, 128), 1)
    ri = lax.broadcasted_iota(jnp.int32, (L, L), 0)
    ci = lax.broadcasted_iota(jnp.int32, (L, L), 1)
    incl = ri >= ci
    strict = ri > ci

    capped, log_f, decay, beta = _gate_transform(proj_ref[:, COL_G:COL_G + 128], gp_ref[...])

    head_lane = lane < H_A
    ig = jnp.where(head_lane, capped, 0.0)
    lf = jnp.where(head_lane, pltpu.roll(log_f, 128 - LANE_F, 1), 0.0)
    mp = m_ref[0]
    F = _scan_rows(lf, row, L, jnp.add, 0.0)
    m = F + jnp.maximum(mp, _scan_rows(ig - F, row, L, jnp.maximum, NEG_BIG))
    m_new = m[L - 1:L, :]
    f_last = F[L - 1:L, :]
    a_rows = F - m
    b_lanes = _rows_to_lanes(ig - F, L)
    inter = jnp.exp(F + mp - m)
    inv_floor = jnp.exp(-m)
    wl = jnp.exp(ig + f_last - F - m_new)
    dec = jnp.exp(f_last + mp - m_new)
    m_ref[0] = m_new

    for h in range(H_A):
        q = proj_ref[:, COL_A + h * HEAD_DIM:COL_A + (h + 1) * HEAD_DIM]
        k = proj_ref[:, COL_A + D_A + h * HEAD_DIM:COL_A + D_A + (h + 1) * HEAD_DIM] * QK_SCALE
        v = proj_ref[:, COL_A + 2 * D_A + h * HEAD_DIM:COL_A + 2 * D_A + (h + 1) * HEAD_DIM]
        og = proj_ref[:, COL_A + 3 * D_A + h * HEAD_DIM:COL_A + 3 * D_A + (h + 1) * HEAD_DIM]
        logw = a_rows[:, h:h + 1] + b_lanes[h:h + 1, :]
        s = _dot_tb(q, k) * jnp.exp(jnp.where(incl, logw, NEG_BIG))
        c_old = c_ref[0, h]
        n_old = n_ref[0, h:h + 1, :]
        inter_h = inter[:, h:h + 1]
        num = inter_h * _dot(q, c_old) + _dot(s, v)
        den = inter_h * _lane_sum(q * n_old) + _lane_sum(s)
        hh = num / jnp.maximum(jnp.abs(den), inv_floor[:, h:h + 1])
        kw = k * wl[:, h:h + 1]
        dec_h = dec[:, h:h + 1]
        c_ref[0, h] = dec_h * c_old + _dot_ta(kw, v)
        n_ref[0, h:h + 1, :] = dec_h * n_old + jnp.sum(kw, axis=0, keepdims=True)
        ya = jax.nn.sigmoid(og) * _head_norm(hh, na_ref[h:h + 1, :], False)
        y_ref[:, h * HEAD_DIM:(h + 1) * HEAD_DIM] = ya.astype(y_ref.dtype)

    cos = cos_ref[...]
    sin = sin_ref[...]
    lg = _retention_log_gamma()
    for h in range(H_B):
        q = proj_ref[:, COL_B + h * HEAD_DIM:COL_B + (h + 1) * HEAD_DIM]
        k = proj_ref[:, COL_B + D_B + h * HEAD_DIM:COL_B + D_B + (h + 1) * HEAD_DIM]
        v = proj_ref[:, COL_B + 2 * D_B + h * HEAD_DIM:COL_B + 2 * D_B + (h + 1) * HEAD_DIM]
        gb = proj_ref[:, COL_B + 3 * D_B + h * HEAD_DIM:COL_B + 3 * D_B + (h + 1) * HEAD_DIM]
        qr = _rotary(q, cos, sin)
        kr = _rotary(k, cos, sin) * QK_SCALE
        s_old = r_ref[0, h]
        inner = _dot_tb(qr, kr) * rd_ref[h]
        o = _dot(inner, v) + rq_ref[h] * _dot(qr, s_old)
        r_ref[0, h] = float(np.exp(L * lg[h])) * s_old + _dot_ta(kr * rk_ref[h], v)
        yb = _silu(gb) * _head_norm(o, nb_ref[h:h + 1, :], True)
        y_ref[:, D_A + h * HEAD_DIM:D_A + (h + 1) * HEAD_DIM] = yb.astype(y_ref.dtype)

    ext_ref[8:8 + L, :] = proj_ref[:, COL_C:COL_C + 3 * D_C]
    G = _scan_rows(decay, row, L, jnp.add, 0.0)
    g_lanes = _rows_to_lanes(G, L)
    gam = jnp.exp(G)
    g_last = G[L - 1:L, :]
    k_decay = jnp.exp(g_last - G)
    s_decay = jnp.exp(g_last)

    def conv_block(col):
        acc = cw_ref[CONV_W - 1:CONV_W, col:col + HEAD_DIM] * ext_ref[8:8 + L, col:col + HEAD_DIM]
        for w in range(CONV_W - 1):
            off = 8 - (CONV_W - 1) + w
            acc = acc + cw_ref[w:w + 1, col:col + HEAD_DIM] * ext_ref[off:off + L, col:col + HEAD_DIM]
        return _silu(acc)

    for h in range(H_C):
        dl = LANE_DECAY + h
        bl = LANE_BETA + h
        q = _l2norm(conv_block(h * HEAD_DIM)) * QK_SCALE
        k = _l2norm(conv_block(D_C + h * HEAD_DIM))
        v = conv_block(2 * D_C + h * HEAD_DIM)
        gz = proj_ref[:, COL_C + 3 * D_C + h * HEAD_DIM:COL_C + 3 * D_C + (h + 1) * HEAD_DIM]
        beta_h = beta[:, bl:bl + 1]
        dec_in = jnp.exp(jnp.where(incl, G[:, dl:dl + 1] - g_lanes[dl:dl + 1, :], NEG_BIG))
        a_mat = jnp.where(strict, dec_in, 0.0) * beta_h * _dot_tb(k, k)
        pw = -a_mat
        e_mat = pw
        span = 1
        while 2 * span < L:
            pw = _dot(pw, pw)
            e_mat = e_mat + pw + _dot(e_mat, pw)
            span *= 2
        rhs = jnp.concatenate([(beta_h * gam[:, dl:dl + 1]) * k, beta_h * v], axis=1)
        sol = rhs + _dot(e_mat, rhs)
        s_old = g_ref[0, h]
        both = _dot(jnp.concatenate([sol[:, :HEAD_DIM], q], axis=0), s_old)
        u = sol[:, HEAD_DIM:] - both[:L]
        o = gam[:, dl:dl + 1] * both[L:] + _dot(_dot_tb(q, k) * dec_in, u)
        g_ref[0, h] = s_decay[:, dl:dl + 1] * s_old + _dot_ta(k * k_decay[:, dl:dl + 1], u)
        yc = _silu(gz) * _head_norm(o, nc_ref[...], False)
        y_ref[:, D_A + D_B + h * HEAD_DIM:D_A + D_B + (h + 1) * HEAD_DIM] = yc.astype(y_ref.dtype)

    ext_ref[0:8, :] = ext_ref[L:L + 8, :]

    @pl.when(c == pl.num_programs(1) - 1)
    def _():
        conv_ref[0] = ext_ref[8 + L - (CONV_W - 1):8 + L, :]


def _prompt_tables(seq, L):
    lg = _retention_log_gamma()
    i = np.arange(L, dtype=np.float64)
    diff = i[:, None] - i[None, :]
    rd = np.where(diff >= 0, np.exp(np.maximum(diff, 0.0) * lg[:, None, None]), 0.0)
    rq = np.broadcast_to(np.exp((i + 1.0) * lg[:, None])[..., None], (H_B, L, HEAD_DIM))
    rk = np.broadcast_to(np.exp((L - 1.0 - i) * lg[:, None])[..., None], (H_B, L, HEAD_DIM))
    cos, sin = _rope_tables(np.arange(seq))
    return (jnp.asarray(cos), jnp.asarray(sin), jnp.asarray(rd, f32), jnp.asarray(rq, f32), jnp.asarray(rk, f32))


def _prompt_mixers(proj, tables, gp, cw, na, nb, nc, batch, seq, total_rows, L):
    nchunk = seq // L
    cos, sin, rd, rq, rk = tables
    full = lambda shape: pl.BlockSpec(shape, lambda b, c: (0,) * len(shape))
    state4 = lambda heads: pl.BlockSpec((1, heads, HEAD_DIM, HEAD_DIM), lambda b, c: (b, 0, 0, 0))
    return pl.pallas_call(
        functools.partial(_prompt_mixer_kernel, L),
        grid=(batch, nchunk),
        in_specs=[
            pl.BlockSpec((L, N_PROJ), lambda b, c: (b * nchunk + c, 0)),
            pl.BlockSpec((L, HEAD_DIM), lambda b, c: (c, 0)),
            pl.BlockSpec((L, HEAD_DIM), lambda b, c: (c, 0)),
            full((H_B, L, L)), full((H_B, L, HEAD_DIM)), full((H_B, L, HEAD_DIM)),
            full((8, 128)), full((CONV_W, 3 * D_C)),
            full((H_A, HEAD_DIM)), full((H_B, HEAD_DIM)), full((1, HEAD_DIM)),
        ],
        out_specs=[
            pl.BlockSpec((L, D_MODEL), lambda b, c: (b * nchunk + c, 0)),
            state4(H_A),
            pl.BlockSpec((1, H_A, HEAD_DIM), lambda b, c: (b, 0, 0)),
            pl.BlockSpec((1, 1, 128), lambda b, c: (b, 0, 0)),
            state4(H_B),
            state4(H_C),
            pl.BlockSpec((1, CONV_W - 1, 3 * D_C), lambda b, c: (b, 0, 0)),
        ],
        out_shape=[
            jax.ShapeDtypeStruct((total_rows, D_MODEL), bf16),
            jax.ShapeDtypeStruct((batch, H_A, HEAD_DIM, HEAD_DIM), f32),
            jax.ShapeDtypeStruct((batch, H_A, HEAD_DIM), f32),
            jax.ShapeDtypeStruct((batch, 1, 128), f32),
            jax.ShapeDtypeStruct((batch, H_B, HEAD_DIM, HEAD_DIM), f32),
            jax.ShapeDtypeStruct((batch, H_C, HEAD_DIM, HEAD_DIM), f32),
            jax.ShapeDtypeStruct((batch, CONV_W - 1, 3 * D_C), f32),
        ],
        scratch_shapes=[pltpu.VMEM((L + 8, 3 * D_C), f32)],
        compiler_params=_cparams(("parallel", "arbitrary")),
        name="prompt_mixers",
    )(proj, cos, sin, rd, rq, rk, gp, cw, na, nb, nc)


def _sample_mixer_kernel(T, BB, layer_first, *refs):
    (proj_ref, cos_ref, sin_ref, rt_ref, gp_ref, cw_ref, na_ref, nb_ref, nc_ref,
     c0_ref, nrep_ref, mrep_ref, r0_ref, g0_ref, conv0_ref) = refs[:15]
    rest = refs[15 + (1 if layer_first else 7):]
    (y_ref, c_ref, nout_ref, mout_ref, r_ref, g_ref, conv_ref, cv_ref, sa_ref, sb_ref, ext_ref) = rest
    R = T * BB
    row = lax.broadcasted_iota(jnp.int32, (R, 128), 0)
    lane = lax.broadcasted_iota(jnp.int32, (R, 128), 1)
    t = lax.rem(row, T)
    tcol = t[:, 0:1]

    def shift(x, s):
        return x if s == 0 else pltpu.roll(x, s, 0)

    def seg_scan(x, op, fill):
        s = 1
        while s < T:
            x = op(x, jnp.where(t >= s, shift(x, s), fill))
            s *= 2
        return x

    def last_rep(x):
        x_last = jnp.where(t == T - 1, x, 0.0)
        out = x_last
        for s in range(1, T):
            out = out + pltpu.roll(x_last, R - s, 0)
        return out

    def per_seq_rows(scr_ref, fn):
        for b in range(BB):
            scr_ref[b * T:(b + 1) * T, :] = fn(b)
        return scr_ref[...]

    capped, log_f, decay, beta = _gate_transform(proj_ref[:, COL_G:COL_G + 128], gp_ref[...])

    head_lane = lane < H_A
    ig = jnp.where(head_lane, capped, 0.0)
    lf = jnp.where(head_lane, pltpu.roll(log_f, 128 - LANE_F, 1), 0.0)
    mp = mrep_ref[...]
    F = seg_scan(lf, jnp.add, 0.0)
    m = F + jnp.maximum(mp, seg_scan(ig - F, jnp.maximum, NEG_BIG))
    m_new = last_rep(m)
    f_last = last_rep(F)
    inter = jnp.exp(F + mp - m)
    inv_floor = jnp.exp(-m)
    wl = jnp.exp(ig + f_last - F - m_new)
    dec = jnp.exp(f_last + mp - m_new)
    mout_ref[...] = m_new
    pw = [jnp.where(t >= s, jnp.exp(F - shift(F, s) + shift(ig, s) - m), 0.0) for s in range(T)]

    for h in range(H_A):
        q = proj_ref[:, COL_A + h * HEAD_DIM:COL_A + (h + 1) * HEAD_DIM]
        k = proj_ref[:, COL_A + D_A + h * HEAD_DIM:COL_A + D_A + (h + 1) * HEAD_DIM] * QK_SCALE
        v = proj_ref[:, COL_A + 2 * D_A + h * HEAD_DIM:COL_A + 2 * D_A + (h + 1) * HEAD_DIM]
        og = proj_ref[:, COL_A + 3 * D_A + h * HEAD_DIM:COL_A + 3 * D_A + (h + 1) * HEAD_DIM]
        n_old = nrep_ref[:, h * HEAD_DIM:(h + 1) * HEAD_DIM]
        inter_h = inter[:, h:h + 1]
        qc = per_seq_rows(sa_ref, lambda b: _dot(q[b * T:(b + 1) * T], c0_ref[b, h]))
        num = inter_h * qc
        den = inter_h * _lane_sum(q * n_old)
        for s in range(T):
            sc = _lane_sum(q * shift(k, s)) * pw[s][:, h:h + 1]
            num = num + sc * shift(v, s)
            den = den + sc
        hh = num / jnp.maximum(jnp.abs(den), inv_floor[:, h:h + 1])
        kw = k * wl[:, h:h + 1]
        for b in range(BB):
            dec_bh = dec[b * T:b * T + 1, h:h + 1]
            c_ref[b, h] = dec_bh * c0_ref[b, h] + _dot_ta(kw[b * T:(b + 1) * T], v[b * T:(b + 1) * T])
        nout_ref[:, h * HEAD_DIM:(h + 1) * HEAD_DIM] = dec[:, h:h + 1] * n_old + seg_scan(kw, jnp.add, 0.0)
        ya = jax.nn.sigmoid(og) * _head_norm(hh, na_ref[h:h + 1, :], False)
        y_ref[:, h * HEAD_DIM:(h + 1) * HEAD_DIM] = ya.astype(y_ref.dtype)

    cos = cos_ref[...]
    sin = sin_ref[...]
    lg = _retention_log_gamma()
    for h in range(H_B):
        q = proj_ref[:, COL_B + h * HEAD_DIM:COL_B + (h + 1) * HEAD_DIM]
        k = proj_ref[:, COL_B + D_B + h * HEAD_DIM:COL_B + D_B + (h + 1) * HEAD_DIM]
        v = proj_ref[:, COL_B + 2 * D_B + h * HEAD_DIM:COL_B + 2 * D_B + (h + 1) * HEAD_DIM]
        gb = proj_ref[:, COL_B + 3 * D_B + h * HEAD_DIM:COL_B + 3 * D_B + (h + 1) * HEAD_DIM]
        qr = _rotary(q, cos, sin)
        kr = _rotary(k, cos, sin) * QK_SCALE
        o = rt_ref[:, h:h + 1] * per_seq_rows(sa_ref, lambda b: _dot(qr[b * T:(b + 1) * T], r0_ref[b, h]))
        for s in range(T):
            sc = jnp.where(tcol >= s, _lane_sum(qr * shift(kr, s)) * float(np.exp(s * lg[h])), 0.0)
            o = o + sc * shift(v, s)
        kd = kr * rt_ref[:, H_B + h:H_B + h + 1]
        for b in range(BB):
            r_ref[b, h] = float(np.exp(T * lg[h])) * r0_ref[b, h] + _dot_ta(kd[b * T:(b + 1) * T], v[b * T:(b + 1) * T])
        yb = _silu(gb) * _head_norm(o, nb_ref[h:h + 1, :], True)
        y_ref[:, D_A + h * HEAD_DIM:D_A + (h + 1) * HEAD_DIM] = yb.astype(y_ref.dtype)

    for b in range(BB):
        ext_ref[b, 0:CONV_W - 1, :] = conv0_ref[b]
        ext_ref[b, CONV_W - 1:CONV_W - 1 + T, :] = proj_ref[b * T:(b + 1) * T, COL_C:COL_C + 3 * D_C]
        acc = cw_ref[0:1, :] * ext_ref[b, 0:T, :]
        for w in range(1, CONV_W):
            acc = acc + cw_ref[w:w + 1, :] * ext_ref[b, w:w + T, :]
        cv_ref[b * T:(b + 1) * T, :] = _silu(acc)
        conv_ref[b] = ext_ref[b, T:T + CONV_W - 1, :]

    G = seg_scan(decay, jnp.add, 0.0)
    gam = jnp.exp(G)
    g_last = last_rep(G)
    k_decay = jnp.exp(g_last - G)
    s_decay = jnp.exp(g_last)
    dshift = [None] + [jnp.where(t >= s, jnp.exp(G - shift(G, s)), 0.0) for s in range(1, T)]

    for h in range(H_C):
        dl = LANE_DECAY + h
        bl = LANE_BETA + h
        q = _l2norm(cv_ref[:, h * HEAD_DIM:(h + 1) * HEAD_DIM]) * QK_SCALE
        k = _l2norm(cv_ref[:, D_C + h * HEAD_DIM:D_C + (h + 1) * HEAD_DIM])
        v = cv_ref[:, 2 * D_C + h * HEAD_DIM:2 * D_C + (h + 1) * HEAD_DIM]
        gz = proj_ref[:, COL_C + 3 * D_C + h * HEAD_DIM:COL_C + 3 * D_C + (h + 1) * HEAD_DIM]
        beta_h = beta[:, bl:bl + 1]
        ksh = [shift(k, s) for s in range(T)]
        a_sub = [None] + [beta_h * dshift[s][:, dl:dl + 1] * _lane_sum(k * ksh[s]) for s in range(1, T)]
        rhs_w = (beta_h * gam[:, dl:dl + 1]) * k
        rhs_u = beta_h * v
        w_sol, u_sol = rhs_w, rhs_u
        for i in range(1, T):
            upd_w = a_sub[1] * shift(w_sol, 1)
            upd_u = a_sub[1] * shift(u_sol, 1)
            for s in range(2, i + 1):
                upd_w = upd_w + a_sub[s] * shift(w_sol, s)
                upd_u = upd_u + a_sub[s] * shift(u_sol, s)
            w_sol = jnp.where(t == i, rhs_w - upd_w, w_sol)
            u_sol = jnp.where(t == i, rhs_u - upd_u, u_sol)
        for b in range(BB):
            both = _dot(jnp.concatenate([w_sol[b * T:(b + 1) * T], q[b * T:(b + 1) * T]], axis=0), g0_ref[b, h])
            sa_ref[b * T:(b + 1) * T, :] = both[:T]
            sb_ref[b * T:(b + 1) * T, :] = both[T:]
        u = u_sol - sa_ref[...]
        o = gam[:, dl:dl + 1] * sb_ref[...]
        for s in range(T):
            sc = _lane_sum(q * ksh[s])
            if s > 0:
                sc = sc * dshift[s][:, dl:dl + 1]
            o = o + sc * shift(u, s)
        kd = k * k_decay[:, dl:dl + 1]
        for b in range(BB):
            g_ref[b, h] = (s_decay[b * T:b * T + 1, dl:dl + 1] * g0_ref[b, h]
                           + _dot_ta(kd[b * T:(b + 1) * T], u[b * T:(b + 1) * T]))
        yc = _silu(gz) * _head_norm(o, nc_ref[...], False)
        y_ref[:, D_A + D_B + h * HEAD_DIM:D_A + D_B + (h + 1) * HEAD_DIM] = yc.astype(y_ref.dtype)


def _sample_tables(T, BB):
    lg = _retention_log_gamma()
    tt = np.arange(T, dtype=np.float64)
    rt = np.zeros((T, 128), np.float64)
    rt[:, 0:H_B] = np.exp((tt[:, None] + 1.0) * lg[None, :])
    rt[:, H_B:2 * H_B] = np.exp((T - 1.0 - tt[:, None]) * lg[None, :])
    cos, sin = _rope_tables(PAST_LEN + np.arange(T))
    rep = lambda a: jnp.asarray(np.tile(a, (BB, 1)), f32)
    return rep(cos), rep(sin), rep(rt)


def _sample_mixers(layer, proj, y, tables, gp, cw, na, nb, nc, states, prev_out, batch, T, row0, BB):
    c0, nrep, mrep, r0, g0, conv0 = states
    depth = c0.shape[0]
    R = T * BB
    nblk = batch // BB
    blk0 = row0 // R
    cos, sin, rt = tables
    full = lambda shape: pl.BlockSpec(shape, lambda i: (0,) * len(shape))
    st4 = lambda heads: pl.BlockSpec((None, BB, heads, HEAD_DIM, HEAD_DIM), lambda i: (layer, i, 0, 0, 0))
    rows = lambda width: pl.BlockSpec((None, R, width), lambda i: (layer, i, 0))
    convspec = pl.BlockSpec((None, BB, CONV_W - 1, 3 * D_C), lambda i: (layer, i, 0, 0))
    anyspec = pl.BlockSpec(memory_space=pl.ANY)
    in_specs = [
        pl.BlockSpec((R, N_PROJ), lambda i: (blk0 + i, 0)),
        full((R, HEAD_DIM)), full((R, HEAD_DIM)), full((R, 128)),
        full((8, 128)), full((CONV_W, 3 * D_C)),
        full((H_A, HEAD_DIM)), full((H_B, HEAD_DIM)), full((1, HEAD_DIM)),
        st4(H_A), rows(D_A), rows(128), st4(H_B), st4(H_C), convspec,
    ]
    args = [proj, cos, sin, rt, gp, cw, na, nb, nc, c0, nrep, mrep, r0, g0, conv0]
    out_shape = [
        jax.ShapeDtypeStruct(y.shape, y.dtype),
        jax.ShapeDtypeStruct(c0.shape, f32),
        jax.ShapeDtypeStruct(nrep.shape, f32),
        jax.ShapeDtypeStruct(mrep.shape, f32),
        jax.ShapeDtypeStruct(r0.shape, f32),
        jax.ShapeDtypeStruct(g0.shape, f32),
        jax.ShapeDtypeStruct(conv0.shape, f32),
    ]
    out_specs = [pl.BlockSpec((R, D_MODEL), lambda i: (blk0 + i, 0)),
                 st4(H_A), rows(D_A), rows(128), st4(H_B), st4(H_C), convspec]
    first = prev_out is None
    if first:
        in_specs.append(anyspec)
        args.append(y)
        aliases = {15: 0}
    else:
        in_specs += [anyspec] * 7
        args += [y] + list(prev_out)
        aliases = {15 + j: j for j in range(7)}
    del depth
    return pl.pallas_call(
        functools.partial(_sample_mixer_kernel, T, BB, first),
        grid=(nblk,),
        in_specs=in_specs,
        out_specs=out_specs,
        out_shape=out_shape,
        input_output_aliases=aliases,
        scratch_shapes=[pltpu.VMEM((R, 3 * D_C), f32), pltpu.VMEM((R, HEAD_DIM), f32), pltpu.VMEM((R, HEAD_DIM), f32),
                        pltpu.VMEM((BB, 8, 3 * D_C), f32)],
        compiler_params=_cparams(("parallel",)),
        name="sample_mixers",
    )(*args)


def _repack_w_in(w_in):
    n_a = 4 * D_A
    n_ag = n_a + 2 * H_A
    n_bc = 4 * D_B + 4 * D_C
    pad = jnp.zeros(w_in.shape[:2] + (128 - 2 * H_A - 2 * H_C,), w_in.dtype)
    parts = [w_in[:, :, :n_a], w_in[:, :, n_ag:n_ag + n_bc], w_in[:, :, n_a:n_ag], w_in[:, :, n_ag + n_bc:], pad]
    return jnp.concatenate(parts, axis=-1).astype(bf16)


def _gate_params(gate_bias, dt_bias, a_log):
    depth = gate_bias.shape[0]
    gp = jnp.zeros((depth, 8, 128), f32)
    gp = gp.at[:, 0, LANE_I:LANE_I + 2 * H_A].set(gate_bias)
    gp = gp.at[:, 0, LANE_DECAY:LANE_DECAY + H_C].set(dt_bias)
    gp = gp.at[:, 1, LANE_DECAY:LANE_DECAY + H_C].set(a_log)
    return gp


def kernel(x_prompt, x_sample, state_mlstm_C, state_mlstm_n, state_mlstm_m, state_ret_S, state_gdn_S,
           state_gdn_conv, norm_mix_pre, norm_mix_post, norm_mlp_pre, norm_mlp_post, w_in, mlstm_gate_bias,
           gdn_conv_w, gdn_A_log, gdn_dt_bias, norm_mlstm, norm_ret, norm_gdn, w_out, w_up, w_down):
    bp, tp, d = x_prompt.shape
    bs, ts, _ = x_sample.shape
    depth = w_in.shape[0]
    rows_p = bp * tp
    rows_s = bs * ts
    rows = rows_p + rows_s

    x = jnp.concatenate([x_prompt.reshape(rows_p, d), x_sample.reshape(rows_s, d)], axis=0)
    w_in_p = _repack_w_in(w_in)
    w_out_b = w_out.astype(bf16)
    w_down_b = w_down.astype(bf16)
    gp = _gate_params(mlstm_gate_bias, gdn_dt_bias, gdn_A_log)
    na = norm_mlstm.reshape(depth, H_A, HEAD_DIM)
    nb = norm_ret.reshape(depth, H_B, HEAD_DIM)
    nc = norm_gdn.reshape(depth, 1, HEAD_DIM)

    nrep = jnp.repeat(state_mlstm_n.reshape(depth, bs, D_A), ts, axis=1)
    mrep = jnp.pad(jnp.repeat(state_mlstm_m, ts, axis=1), ((0, 0), (0, 0), (0, 128 - H_A)))
    s_states = (state_mlstm_C, nrep, mrep, state_ret_S, state_gdn_S, state_gdn_conv)

    p_tables = _prompt_tables(tp, PROMPT_CHUNK)
    s_tables = _sample_tables(ts, SAMPLE_BLOCK)

    tm = rows // 8
    h = _rmsnorm_rows(x, norm_mix_pre[0], tm // 2)
    p_states = []
    s_out = None
    for l in range(depth):
        proj = _matmul_in(h, w_in_p, l, tm, 640)
        outs = _prompt_mixers(proj, p_tables, gp[l], gdn_conv_w[l], na[l], nb[l], nc[l], bp, tp, rows, PROMPT_CHUNK)
        p_states.append(outs[1:])
        res = _sample_mixers(l, proj, outs[0], s_tables, gp[l], gdn_conv_w[l], na[l], nb[l], nc[l],
                             s_states, s_out, bs, ts, rows_p, SAMPLE_BLOCK)
        y, s_out = res[0], res[1:]
        x, h = _matmul_out(y, w_out_b, l, x, norm_mix_post[l], norm_mlp_pre[l], tm // 2)
        u = _matmul_up(h, w_up, l, tm, 1024)
        x, h = _matmul_down(u, w_down_b, l, x, norm_mlp_post[l], norm_mix_pre[(l + 1) % depth], tm // 2, 1024)

    stk = lambda j: jnp.stack([s[j] for s in p_states], axis=0)
    sc, sn, sm, sr, sg, sconv = s_out
    return (
        x[:rows_p].reshape(bp, tp, d), x[rows_p:].reshape(bs, ts, d),
        stk(0), stk(1), stk(2)[:, :, 0, :H_A], stk(3), stk(4), stk(5),
        sc, sn[:, ts - 1::ts, :].reshape(depth, bs, H_A, HEAD_DIM), sm[:, ts - 1::ts, :H_A], sr, sg,
        sconv,
    )
```

```python
import functools
import math

import numpy as np
import jax
import jax.numpy as jnp
from jax import lax
from jax.experimental import pallas as pl
from jax.experimental.pallas import tpu as pltpu

f32 = jnp.float32
bf16 = jnp.bfloat16

D_MODEL = 2048
HEAD_DIM = 128
H_A, H_B, H_C = 4, 4, 8
D_A, D_B, D_C = H_A * HEAD_DIM, H_B * HEAD_DIM, H_C * HEAD_DIM
D_FF = 4 * D_MODEL
CONV_W = 4
PAST_LEN = 16384
ROPE_BASE = 10000.0
GATE_SOFTCAP = 15.0
NORM_EPS = 1e-6
QK_SCALE = HEAD_DIM ** -0.5

COL_A = 0
COL_B = 4 * D_A
COL_C = COL_B + 4 * D_B
N_PROJ = COL_C + 4 * D_C
LANE_I, LANE_F, LANE_DECAY, LANE_BETA = 0, H_A, 2 * H_A, 2 * H_A + H_C

PROMPT_CHUNK = 64
SAMPLE_BLOCK = 8
NEG_BIG = -1e30
VMEM_LIMIT = 56 * 1024 * 1024


def _cparams(sem):
    return pltpu.CompilerParams(dimension_semantics=sem, vmem_limit_bytes=VMEM_LIMIT)


def _rms(x, g):
    return x * lax.rsqrt(jnp.mean(x * x, axis=-1, keepdims=True) + NORM_EPS) * g


def _rmsnorm_kernel(x_ref, g_ref, o_ref):
    o_ref[...] = _rms(x_ref[...], g_ref[...]).astype(o_ref.dtype)


def _rmsnorm_rows(x, g, tm):
    m, d = x.shape
    return pl.pallas_call(
        _rmsnorm_kernel,
        grid=(m // tm,),
        in_specs=[pl.BlockSpec((tm, d), lambda i: (i, 0)), pl.BlockSpec((1, d), lambda i: (0, 0))],
        out_specs=pl.BlockSpec((tm, d), lambda i: (i, 0)),
        out_shape=jax.ShapeDtypeStruct((m, d), bf16),
        compiler_params=_cparams(("parallel",)),
        name="rmsnorm_in",
    )(x, g.reshape(1, d))


def _mm_kernel(a_ref, w_ref, o_ref):
    o_ref[...] = jnp.dot(a_ref[...], w_ref[...], preferred_element_type=f32)


def _matmul_in(h, w, layer, tm, tn, name):
    m, k = h.shape
    n = w.shape[2]
    tn = min(tn, n)
    return pl.pallas_call(
        _mm_kernel,
        grid=(n // tn, m // tm),
        in_specs=[pl.BlockSpec((tm, k), lambda j, i: (i, 0)), pl.BlockSpec((None, k, tn), lambda j, i: (layer, 0, j))],
        out_specs=pl.BlockSpec((tm, tn), lambda j, i: (i, j)),
        out_shape=jax.ShapeDtypeStruct((m, n), f32),
        compiler_params=_cparams(("parallel", "arbitrary")),
        name=name,
    )(h, w)


def _mm_up_kernel(a_ref, w_ref, o_ref, wb_ref):
    @pl.when(pl.program_id(1) == 0)
    def _():
        wb_ref[...] = w_ref[...].astype(bf16)

    z = jnp.maximum(jnp.dot(a_ref[...], wb_ref[...], preferred_element_type=f32), 0.0)
    o_ref[...] = (z * z).astype(o_ref.dtype)


def _matmul_up(h, w, layer, tm, tn):
    m, k = h.shape
    n = w.shape[2]
    return pl.pallas_call(
        _mm_up_kernel,
        grid=(n // tn, m // tm),
        in_specs=[pl.BlockSpec((tm, k), lambda j, i: (i, 0)), pl.BlockSpec((None, k, tn), lambda j, i: (layer, 0, j))],
        out_specs=pl.BlockSpec((tm, tn), lambda j, i: (i, j)),
        out_shape=jax.ShapeDtypeStruct((m, n), bf16),
        scratch_shapes=[pltpu.VMEM((k, tn), bf16)],
        compiler_params=_cparams(("parallel", "arbitrary")),
        name="mlp_up",
    )(h, w)


def _residual_epilogue(z, x_ref, gpost_ref, gnext_ref, xo_ref, ho_ref):
    x_new = x_ref[...] + _rms(z, gpost_ref[...])
    xo_ref[...] = x_new
    ho_ref[...] = _rms(x_new, gnext_ref[...]).astype(ho_ref.dtype)


def _mm_out_kernel(a_ref, w_ref, x_ref, gpost_ref, gnext_ref, xo_ref, ho_ref):
    z = jnp.dot(a_ref[...], w_ref[...], preferred_element_type=f32)
    _residual_epilogue(z, x_ref, gpost_ref, gnext_ref, xo_ref, ho_ref)


def _matmul_out(y, w, layer, x, g_post, g_next, tm):
    m, k = y.shape
    d = w.shape[2]
    row = lambda i: (i, 0)
    fixed = lambda i: (0, 0)
    return pl.pallas_call(
        _mm_out_kernel,
        grid=(m // tm,),
        in_specs=[pl.BlockSpec((tm, k), row), pl.BlockSpec((None, k, d), lambda i: (layer, 0, 0)),
                  pl.BlockSpec((tm, d), row),
                  pl.BlockSpec((1, d), fixed), pl.BlockSpec((1, d), fixed)],
        out_specs=[pl.BlockSpec((tm, d), row), pl.BlockSpec((tm, d), row)],
        out_shape=[jax.ShapeDtypeStruct((m, d), f32), jax.ShapeDtypeStruct((m, d), bf16)],
        compiler_params=_cparams(("parallel",)),
        name="proj_out",
    )(y, w, x, g_post.reshape(1, d), g_next.reshape(1, d))


def _mm_down_kernel(a_ref, w_ref, x_ref, gpost_ref, gnext_ref, xo_ref, ho_ref):
    kk = pl.program_id(1)

    @pl.when(kk == 0)
    def _():
        xo_ref[...] = jnp.zeros_like(xo_ref)

    xo_ref[...] += jnp.dot(a_ref[...], w_ref[...], preferred_element_type=f32)

    @pl.when(kk == pl.num_programs(1) - 1)
    def _():
        _residual_epilogue(xo_ref[...], x_ref, gpost_ref, gnext_ref, xo_ref, ho_ref)


def _matmul_down(u, w, layer, x, g_post, g_next, tm, tk):
    m, k = u.shape
    d = w.shape[2]
    row = lambda i, kk: (i, 0)
    fixed = lambda i, kk: (0, 0)
    return pl.pallas_call(
        _mm_down_kernel,
        grid=(m // tm, k // tk),
        in_specs=[pl.BlockSpec((tm, tk), lambda i, kk: (i, kk)),
                  pl.BlockSpec((None, tk, d), lambda i, kk: (layer, kk, 0)),
                  pl.BlockSpec((tm, d), row), pl.BlockSpec((1, d), fixed), pl.BlockSpec((1, d), fixed)],
        out_specs=[pl.BlockSpec((tm, d), row), pl.BlockSpec((tm, d), row)],
        out_shape=[jax.ShapeDtypeStruct((m, d), f32), jax.ShapeDtypeStruct((m, d), bf16)],
        compiler_params=_cparams(("parallel", "arbitrary")),
        name="mlp_down",
    )(u, w, x, g_post.reshape(1, d), g_next.reshape(1, d))


def _dot(a, b):
    return jnp.dot(a, b, preferred_element_type=f32)


def _dot_tb(a, b):
    return lax.dot_general(a, b, (((1,), (1,)), ((), ())), preferred_element_type=f32)


def _dot_ta(a, b):
    return lax.dot_general(a, b, (((0,), (0,)), ((), ())), preferred_element_type=f32)


def _lane_sum(x):
    return jnp.sum(x, axis=-1, keepdims=True)


def _softplus(z):
    return jnp.maximum(z, 0.0) + jnp.log1p(jnp.exp(-jnp.abs(z)))


def _gate_transform(gt, gp):
    z = gt + gp[0:1, :]
    capped = GATE_SOFTCAP * jnp.tanh(z / GATE_SOFTCAP)
    log_f = -_softplus(-capped)
    decay = -jnp.exp(gp[1:2, :]) * _softplus(z)
    beta = jax.nn.sigmoid(gt)
    return capped, log_f, decay, beta


def _head_norm(x, gain, center):
    if center:
        x = x - jnp.mean(x, axis=-1, keepdims=True)
    return x * lax.rsqrt(jnp.mean(x * x, axis=-1, keepdims=True) + NORM_EPS) * gain


def _l2norm(x):
    return x * lax.rsqrt(_lane_sum(x * x) + NORM_EPS)


def _silu(x):
    return x * jax.nn.sigmoid(x)


def _rotary(x, cos, sin_signed):
    return x * cos + pltpu.roll(x, HEAD_DIM // 2, 1) * sin_signed


def _retention_log_gamma():
    return np.log1p(-np.exp2(-5.0 - np.arange(H_B, dtype=np.float64)))


def _rope_tables(pos):
    half = HEAD_DIM // 2
    inv = ROPE_BASE ** (-np.arange(half, dtype=np.float64) / half)
    ang = np.asarray(pos, dtype=np.float64)[:, None] * inv[None, :]
    cos = np.concatenate([np.cos(ang), np.cos(ang)], axis=-1)
    sin = np.concatenate([-np.sin(ang), np.sin(ang)], axis=-1)
    return cos.astype(np.float32), sin.astype(np.float32)


def _scan_rows(x, row, length, op, fill):
    s = 1
    while s < length:
        x = op(x, jnp.where(row >= s, pltpu.roll(x, s, 0), fill))
        s *= 2
    return x


def _rows_to_lanes(x, length):
    if length < 128:
        x = jnp.concatenate([x, jnp.zeros((128 - length, 128), x.dtype)], axis=0)
    return x.T[:, :length]


def _prompt_mixer_kernel(L, proj_ref, gt_ref, cos_ref, sin_ref, rd_ref, rq_ref, rk_ref, gp_ref, cw_ref,
                         na_ref, nb_ref, nc_ref,
                         y_ref, c_ref, n_ref, m_ref, r_ref, g_ref, conv_ref, ext_ref):
    c = pl.program_id(1)

    @pl.when(c == 0)
    def _():
        c_ref[...] = jnp.zeros_like(c_ref)
        n_ref[...] = jnp.zeros_like(n_ref)
        m_ref[...] = jnp.zeros_like(m_ref)
        r_ref[...] = jnp.zeros_like(r_ref)
        g_ref[...] = jnp.zeros_like(g_ref)
        ext_ref[0:8, :] = jnp.zeros((8, 3 * D_C), f32)

    row = lax.broadcasted_iota(jnp.int32, (L, 128), 0)
    lane = lax.broadcasted_iota(jnp.int32, (L, 128), 1)
    ri = lax.broadcasted_iota(jnp.int32, (L, L), 0)
    ci = lax.broadcasted_iota(jnp.int32, (L, L), 1)
    incl = ri >= ci
    strict = ri > ci

    capped, log_f, decay, beta = _gate_transform(gt_ref[...], gp_ref[...])

    head_lane = lane < H_A
    ig = jnp.where(head_lane, capped, 0.0)
    lf = jnp.where(head_lane, pltpu.roll(log_f, 128 - LANE_F, 1), 0.0)
    mp = m_ref[0]
    F = _scan_rows(lf, row, L, jnp.add, 0.0)
    m = F + jnp.maximum(mp, _scan_rows(ig - F, row, L, jnp.maximum, NEG_BIG))
    m_new = m[L - 1:L, :]
    f_last = F[L - 1:L, :]
    a_rows = F - m
    b_lanes = _rows_to_lanes(ig - F, L)
    inter = jnp.exp(F + mp - m)
    inv_floor = jnp.exp(-m)
    wl = jnp.exp(ig + f_last - F - m_new)
    dec = jnp.exp(f_last + mp - m_new)
    m_ref[0] = m_new

    def mlstm_head(h):
        q = proj_ref[:, COL_A + h * HEAD_DIM:COL_A + (h + 1) * HEAD_DIM]
        k = proj_ref[:, COL_A + D_A + h * HEAD_DIM:COL_A + D_A + (h + 1) * HEAD_DIM] * QK_SCALE
        v = proj_ref[:, COL_A + 2 * D_A + h * HEAD_DIM:COL_A + 2 * D_A + (h + 1) * HEAD_DIM]
        og = proj_ref[:, COL_A + 3 * D_A + h * HEAD_DIM:COL_A + 3 * D_A + (h + 1) * HEAD_DIM]
        logw = a_rows[:, h:h + 1] + b_lanes[h:h + 1, :]
        s = _dot_tb(q, k) * jnp.exp(jnp.where(incl, logw, NEG_BIG))
        c_old = c_ref[0, h]
        n_old = n_ref[0, h:h + 1, :]
        inter_h = inter[:, h:h + 1]
        num = inter_h * _dot(q, c_old) + _dot(s, v)
        den = inter_h * _lane_sum(q * n_old) + _lane_sum(s)
        hh = num / jnp.maximum(jnp.abs(den), inv_floor[:, h:h + 1])
        kw = k * wl[:, h:h + 1]
        dec_h = dec[:, h:h + 1]
        c_ref[0, h] = dec_h * c_old + _dot_ta(kw, v)
        n_ref[0, h:h + 1, :] = dec_h * n_old + jnp.sum(kw, axis=0, keepdims=True)
        ya = jax.nn.sigmoid(og) * _head_norm(hh, na_ref[h:h + 1, :], False)
        y_ref[:, h * HEAD_DIM:(h + 1) * HEAD_DIM] = ya.astype(y_ref.dtype)

    lg = _retention_log_gamma()

    def retention_head(h):
        cos = cos_ref[...]
        sin = sin_ref[...]
        q = proj_ref[:, COL_B + h * HEAD_DIM:COL_B + (h + 1) * HEAD_DIM]
        k = proj_ref[:, COL_B + D_B + h * HEAD_DIM:COL_B + D_B + (h + 1) * HEAD_DIM]
        v = proj_ref[:, COL_B + 2 * D_B + h * HEAD_DIM:COL_B + 2 * D_B + (h + 1) * HEAD_DIM]
        gb = proj_ref[:, COL_B + 3 * D_B + h * HEAD_DIM:COL_B + 3 * D_B + (h + 1) * HEAD_DIM]
        qr = _rotary(q, cos, sin)
        kr = _rotary(k, cos, sin) * QK_SCALE
        s_old = r_ref[0, h]
        inner = _dot_tb(qr, kr) * rd_ref[h]
        o = _dot(inner, v) + rq_ref[h] * _dot(qr, s_old)
        r_ref[0, h] = float(np.exp(L * lg[h])) * s_old + _dot_ta(kr * rk_ref[h], v)
        yb = _silu(gb) * _head_norm(o, nb_ref[h:h + 1, :], True)
        y_ref[:, D_A + h * HEAD_DIM:D_A + (h + 1) * HEAD_DIM] = yb.astype(y_ref.dtype)

    fillers = [functools.partial(mlstm_head, h) for h in range(H_A)]
    fillers += [functools.partial(retention_head, h) for h in range(H_B)]

    def emit_fillers(count):
        for _ in range(min(count, len(fillers))):
            fillers.pop(0)()

    ext_ref[8:8 + L, :] = proj_ref[:, COL_C:COL_C + 3 * D_C]
    G = _scan_rows(decay, row, L, jnp.add, 0.0)
    g_lanes = _rows_to_lanes(G, L)
    gam = jnp.exp(G)
    g_last = G[L - 1:L, :]
    k_decay = jnp.exp(g_last - G)
    s_decay = jnp.exp(g_last)

    def conv_block(col):
        acc = cw_ref[CONV_W - 1:CONV_W, col:col + HEAD_DIM] * ext_ref[8:8 + L, col:col + HEAD_DIM]
        for w in range(CONV_W - 1):
            off = 8 - (CONV_W - 1) + w
            acc = acc + cw_ref[w:w + 1, col:col + HEAD_DIM] * ext_ref[off:off + L, col:col + HEAD_DIM]
        return _silu(acc)

    heads = range(H_C)
    dcol = [LANE_DECAY + h for h in heads]
    beta_c = [beta[:, LANE_BETA + h:LANE_BETA + h + 1] for h in heads]
    q = [_l2norm(conv_block(h * HEAD_DIM)) * QK_SCALE for h in heads]
    k = [_l2norm(conv_block(D_C + h * HEAD_DIM)) for h in heads]
    v = [conv_block(2 * D_C + h * HEAD_DIM) for h in heads]
    dec_in = [jnp.exp(jnp.where(incl, G[:, dcol[h]:dcol[h] + 1] - g_lanes[dcol[h]:dcol[h] + 1, :], NEG_BIG))
              for h in heads]
    pw = [-(jnp.where(strict, dec_in[h], 0.0) * beta_c[h] * _dot_tb(k[h], k[h])) for h in heads]
    qk = [_dot_tb(q[h], k[h]) * dec_in[h] for h in heads]
    e_mat = list(pw)
    span = 1
    while 2 * span < L:
        pw = [_dot(p, p) for p in pw]
        e_mat = [e_mat[h] + pw[h] + _dot(e_mat[h], pw[h]) for h in heads]
        emit_fillers(2)
        span *= 2
    rhs = [jnp.concatenate([(beta_c[h] * gam[:, dcol[h]:dcol[h] + 1]) * k[h], beta_c[h] * v[h]], axis=1)
           for h in heads]
    sol = [rhs[h] + _dot(e_mat[h], rhs[h]) for h in heads]
    emit_fillers(len(fillers))
    s_old = [g_ref[0, h] for h in heads]
    both = [_dot(jnp.concatenate([sol[h][:, :HEAD_DIM], q[h]], axis=0), s_old[h]) for h in heads]
    u = [sol[h][:, HEAD_DIM:] - both[h][:L] for h in heads]
    o = [gam[:, dcol[h]:dcol[h] + 1] * both[h][L:] + _dot(qk[h], u[h]) for h in heads]
    for h in heads:
        g_ref[0, h] = s_decay[:, dcol[h]:dcol[h] + 1] * s_old[h] + _dot_ta(k[h] * k_decay[:, dcol[h]:dcol[h] + 1], u[h])
    for h in heads:
        gz = proj_ref[:, COL_C + 3 * D_C + h * HEAD_DIM:COL_C + 3 * D_C + (h + 1) * HEAD_DIM]
        yc = _silu(gz) * _head_norm(o[h], nc_ref[...], False)
        y_ref[:, D_A + D_B + h * HEAD_DIM:D_A + D_B + (h + 1) * HEAD_DIM] = yc.astype(y_ref.dtype)

    ext_ref[0:8, :] = ext_ref[L:L + 8, :]

    @pl.when(c == pl.num_programs(1) - 1)
    def _():
        conv_ref[0] = ext_ref[8 + L - (CONV_W - 1):8 + L, :]


def _prompt_tables(seq, L):
    lg = _retention_log_gamma()
    i = np.arange(L, dtype=np.float64)
    diff = i[:, None] - i[None, :]
    rd = np.where(diff >= 0, np.exp(np.maximum(diff, 0.0) * lg[:, None, None]), 0.0)
    rq = np.broadcast_to(np.exp((i + 1.0) * lg[:, None])[..., None], (H_B, L, HEAD_DIM))
    rk = np.broadcast_to(np.exp((L - 1.0 - i) * lg[:, None])[..., None], (H_B, L, HEAD_DIM))
    cos, sin = _rope_tables(np.arange(seq))
    return (jnp.asarray(cos), jnp.asarray(sin), jnp.asarray(rd, f32), jnp.asarray(rq, f32), jnp.asarray(rk, f32))


def _prompt_mixers(proj, gates, tables, gp, cw, na, nb, nc, batch, seq, total_rows, L):
    nchunk = seq // L
    cos, sin, rd, rq, rk = tables
    full = lambda shape: pl.BlockSpec(shape, lambda b, c: (0,) * len(shape))
    state4 = lambda heads: pl.BlockSpec((1, heads, HEAD_DIM, HEAD_DIM), lambda b, c: (b, 0, 0, 0))
    return pl.pallas_call(
        functools.partial(_prompt_mixer_kernel, L),
        grid=(batch, nchunk),
        in_specs=[
            pl.BlockSpec((L, N_PROJ), lambda b, c: (b * nchunk + c, 0)),
            pl.BlockSpec((L, 128), lambda b, c: (b * nchunk + c, 0)),
            pl.BlockSpec((L, HEAD_DIM), lambda b, c: (c, 0)),
            pl.BlockSpec((L, HEAD_DIM), lambda b, c: (c, 0)),
            full((H_B, L, L)), full((H_B, L, HEAD_DIM)), full((H_B, L, HEAD_DIM)),
            full((8, 128)), full((CONV_W, 3 * D_C)),
            full((H_A, HEAD_DIM)), full((H_B, HEAD_DIM)), full((1, HEAD_DIM)),
        ],
        out_specs=[
            pl.BlockSpec((L, D_MODEL), lambda b, c: (b * nchunk + c, 0)),
            state4(H_A),
            pl.BlockSpec((1, H_A, HEAD_DIM), lambda b, c: (b, 0, 0)),
            pl.BlockSpec((1, 1, 128), lambda b, c: (b, 0, 0)),
            state4(H_B),
            state4(H_C),
            pl.BlockSpec((1, CONV_W - 1, 3 * D_C), lambda b, c: (b, 0, 0)),
        ],
        out_shape=[
            jax.ShapeDtypeStruct((total_rows, D_MODEL), bf16),
            jax.ShapeDtypeStruct((batch, H_A, HEAD_DIM, HEAD_DIM), f32),
            jax.ShapeDtypeStruct((batch, H_A, HEAD_DIM), f32),
            jax.ShapeDtypeStruct((batch, 1, 128), f32),
            jax.ShapeDtypeStruct((batch, H_B, HEAD_DIM, HEAD_DIM), f32),
            jax.ShapeDtypeStruct((batch, H_C, HEAD_DIM, HEAD_DIM), f32),
            jax.ShapeDtypeStruct((batch, CONV_W - 1, 3 * D_C), f32),
        ],
        scratch_shapes=[pltpu.VMEM((L + 8, 3 * D_C), f32)],
        compiler_params=_cparams(("parallel", "arbitrary")),
        name="prompt_mixers",
    )(proj, gates, cos, sin, rd, rq, rk, gp, cw, na, nb, nc)


def _sample_mixer_kernel(T, BB, layer_first, *refs):
    (proj_ref, gt_ref, cos_ref, sin_ref, rt_ref, gp_ref, cw_ref, na_ref, nb_ref, nc_ref,
     c0_ref, nrep_ref, mrep_ref, r0_ref, g0_ref, conv0_ref) = refs[:16]
    rest = refs[16 + (1 if layer_first else 7):]
    (y_ref, c_ref, nout_ref, mout_ref, r_ref, g_ref, conv_ref, cv_ref, sa_ref, sb_ref, ext_ref) = rest
    R = T * BB
    row = lax.broadcasted_iota(jnp.int32, (R, 128), 0)
    lane = lax.broadcasted_iota(jnp.int32, (R, 128), 1)
    t = lax.rem(row, T)
    tcol = t[:, 0:1]

    def shift(x, s):
        return x if s == 0 else pltpu.roll(x, s, 0)

    def seg_scan(x, op, fill):
        s = 1
        while s < T:
            x = op(x, jnp.where(t >= s, shift(x, s), fill))
            s *= 2
        return x

    def last_rep(x):
        x_last = jnp.where(t == T - 1, x, 0.0)
        out = x_last
        for s in range(1, T):
            out = out + pltpu.roll(x_last, R - s, 0)
        return out

    def per_seq_rows(scr_ref, fn):
        for b in range(BB):
            scr_ref[b * T:(b + 1) * T, :] = fn(b)
        return scr_ref[...]

    capped, log_f, decay, beta = _gate_transform(gt_ref[...], gp_ref[...])

    head_lane = lane < H_A
    ig = jnp.where(head_lane, capped, 0.0)
    lf = jnp.where(head_lane, pltpu.roll(log_f, 128 - LANE_F, 1), 0.0)
    mp = mrep_ref[...]
    F = seg_scan(lf, jnp.add, 0.0)
    m = F + jnp.maximum(mp, seg_scan(ig - F, jnp.maximum, NEG_BIG))
    m_new = last_rep(m)
    f_last = last_rep(F)
    inter = jnp.exp(F + mp - m)
    inv_floor = jnp.exp(-m)
    wl = jnp.exp(ig + f_last - F - m_new)
    dec = jnp.exp(f_last + mp - m_new)
    mout_ref[...] = m_new
    pw = [jnp.where(t >= s, jnp.exp(F - shift(F, s) + shift(ig, s) - m), 0.0) for s in range(T)]

    for h in range(H_A):
        q = proj_ref[:, COL_A + h * HEAD_DIM:COL_A + (h + 1) * HEAD_DIM]
        k = proj_ref[:, COL_A + D_A + h * HEAD_DIM:COL_A + D_A + (h + 1) * HEAD_DIM] * QK_SCALE
        v = proj_ref[:, COL_A + 2 * D_A + h * HEAD_DIM:COL_A + 2 * D_A + (h + 1) * HEAD_DIM]
        og = proj_ref[:, COL_A + 3 * D_A + h * HEAD_DIM:COL_A + 3 * D_A + (h + 1) * HEAD_DIM]
        n_old = nrep_ref[:, h * HEAD_DIM:(h + 1) * HEAD_DIM]
        inter_h = inter[:, h:h + 1]
        qc = per_seq_rows(sa_ref, lambda b: _dot(q[b * T:(b + 1) * T], c0_ref[b, h]))
        num = inter_h * qc
        den = inter_h * _lane_sum(q * n_old)
        for s in range(T):
            sc = _lane_sum(q * shift(k, s)) * pw[s][:, h:h + 1]
            num = num + sc * shift(v, s)
            den = den + sc
        hh = num / jnp.maximum(jnp.abs(den), inv_floor[:, h:h + 1])
        kw = k * wl[:, h:h + 1]
        for b in range(BB):
            dec_bh = dec[b * T:b * T + 1, h:h + 1]
            c_ref[b, h] = dec_bh * c0_ref[b, h] + _dot_ta(kw[b * T:(b + 1) * T], v[b * T:(b + 1) * T])
        nout_ref[:, h * HEAD_DIM:(h + 1) * HEAD_DIM] = dec[:, h:h + 1] * n_old + seg_scan(kw, jnp.add, 0.0)
        ya = jax.nn.sigmoid(og) * _head_norm(hh, na_ref[h:h + 1, :], False)
        y_ref[:, h * HEAD_DIM:(h + 1) * HEAD_DIM] = ya.astype(y_ref.dtype)

    cos = cos_ref[...]
    sin = sin_ref[...]
    lg = _retention_log_gamma()
    for h in range(H_B):
        q = proj_ref[:, COL_B + h * HEAD_DIM:COL_B + (h + 1) * HEAD_DIM]
        k = proj_ref[:, COL_B + D_B + h * HEAD_DIM:COL_B + D_B + (h + 1) * HEAD_DIM]
        v = proj_ref[:, COL_B + 2 * D_B + h * HEAD_DIM:COL_B + 2 * D_B + (h + 1) * HEAD_DIM]
        gb = proj_ref[:, COL_B + 3 * D_B + h * HEAD_DIM:COL_B + 3 * D_B + (h + 1) * HEAD_DIM]
        qr = _rotary(q, cos, sin)
        kr = _rotary(k, cos, sin) * QK_SCALE
        o = rt_ref[:, h:h + 1] * per_seq_rows(sa_ref, lambda b: _dot(qr[b * T:(b + 1) * T], r0_ref[b, h]))
        for s in range(T):
            sc = jnp.where(tcol >= s, _lane_sum(qr * shift(kr, s)) * float(np.exp(s * lg[h])), 0.0)
            o = o + sc * shift(v, s)
        kd = kr * rt_ref[:, H_B + h:H_B + h + 1]
        for b in range(BB):
            r_ref[b, h] = float(np.exp(T * lg[h])) * r0_ref[b, h] + _dot_ta(kd[b * T:(b + 1) * T], v[b * T:(b + 1) * T])
        yb = _silu(gb) * _head_norm(o, nb_ref[h:h + 1, :], True)
        y_ref[:, D_A + h * HEAD_DIM:D_A + (h + 1) * HEAD_DIM] = yb.astype(y_ref.dtype)

    for b in range(BB):
        ext_ref[b, 0:CONV_W - 1, :] = conv0_ref[b]
        ext_ref[b, CONV_W - 1:CONV_W - 1 + T, :] = proj_ref[b * T:(b + 1) * T, COL_C:COL_C + 3 * D_C]
        acc = cw_ref[0:1, :] * ext_ref[b, 0:T, :]
        for w in range(1, CONV_W):
            acc = acc + cw_ref[w:w + 1, :] * ext_ref[b, w:w + T, :]
        cv_ref[b * T:(b + 1) * T, :] = _silu(acc)
        conv_ref[b] = ext_ref[b, T:T + CONV_W - 1, :]

    G = seg_scan(decay, jnp.add, 0.0)
    gam = jnp.exp(G)
    g_last = last_rep(G)
    k_decay = jnp.exp(g_last - G)
    s_decay = jnp.exp(g_last)
    dshift = [None] + [jnp.where(t >= s, jnp.exp(G - shift(G, s)), 0.0) for s in range(1, T)]

    for h in range(H_C):
        dl = LANE_DECAY + h
        bl = LANE_BETA + h
        q = _l2norm(cv_ref[:, h * HEAD_DIM:(h + 1) * HEAD_DIM]) * QK_SCALE
        k = _l2norm(cv_ref[:, D_C + h * HEAD_DIM:D_C + (h + 1) * HEAD_DIM])
        v = cv_ref[:, 2 * D_C + h * HEAD_DIM:2 * D_C + (h + 1) * HEAD_DIM]
        gz = proj_ref[:, COL_C + 3 * D_C + h * HEAD_DIM:COL_C + 3 * D_C + (h + 1) * HEAD_DIM]
        beta_h = beta[:, bl:bl + 1]
        ksh = [shift(k, s) for s in range(T)]
        a_sub = [None] + [beta_h * dshift[s][:, dl:dl + 1] * _lane_sum(k * ksh[s]) for s in range(1, T)]
        rhs_w = (beta_h * gam[:, dl:dl + 1]) * k
        rhs_u = beta_h * v
        w_sol, u_sol = rhs_w, rhs_u
        for i in range(1, T):
            upd_w = a_sub[1] * shift(w_sol, 1)
            upd_u = a_sub[1] * shift(u_sol, 1)
            for s in range(2, i + 1):
                upd_w = upd_w + a_sub[s] * shift(w_sol, s)
                upd_u = upd_u + a_sub[s] * shift(u_sol, s)
            w_sol = jnp.where(t == i, rhs_w - upd_w, w_sol)
            u_sol = jnp.where(t == i, rhs_u - upd_u, u_sol)
        for b in range(BB):
            both = _dot(jnp.concatenate([w_sol[b * T:(b + 1) * T], q[b * T:(b + 1) * T]], axis=0), g0_ref[b, h])
            sa_ref[b * T:(b + 1) * T, :] = both[:T]
            sb_ref[b * T:(b + 1) * T, :] = both[T:]
        u = u_sol - sa_ref[...]
        o = gam[:, dl:dl + 1] * sb_ref[...]
        for s in range(T):
            sc = _lane_sum(q * ksh[s])
            if s > 0:
                sc = sc * dshift[s][:, dl:dl + 1]
            o = o + sc * shift(u, s)
        kd = k * k_decay[:, dl:dl + 1]
        for b in range(BB):
            g_ref[b, h] = (s_decay[b * T:b * T + 1, dl:dl + 1] * g0_ref[b, h]
                           + _dot_ta(kd[b * T:(b + 1) * T], u[b * T:(b + 1) * T]))
        yc = _silu(gz) * _head_norm(o, nc_ref[...], False)
        y_ref[:, D_A + D_B + h * HEAD_DIM:D_A + D_B + (h + 1) * HEAD_DIM] = yc.astype(y_ref.dtype)


def _sample_tables(T, BB):
    lg = _retention_log_gamma()
    tt = np.arange(T, dtype=np.float64)
    rt = np.zeros((T, 128), np.float64)
    rt[:, 0:H_B] = np.exp((tt[:, None] + 1.0) * lg[None, :])
    rt[:, H_B:2 * H_B] = np.exp((T - 1.0 - tt[:, None]) * lg[None, :])
    cos, sin = _rope_tables(PAST_LEN + np.arange(T))
    rep = lambda a: jnp.asarray(np.tile(a, (BB, 1)), f32)
    return rep(cos), rep(sin), rep(rt)


def _sample_mixers(layer, proj, gates, y, tables, gp, cw, na, nb, nc, states, prev_out, batch, T, row0, BB):
    c0, nrep, mrep, r0, g0, conv0 = states
    depth = c0.shape[0]
    R = T * BB
    nblk = batch // BB
    blk0 = row0 // R
    cos, sin, rt = tables
    full = lambda shape: pl.BlockSpec(shape, lambda i: (0,) * len(shape))
    st4 = lambda heads: pl.BlockSpec((None, BB, heads, HEAD_DIM, HEAD_DIM), lambda i: (layer, i, 0, 0, 0))
    rows = lambda width: pl.BlockSpec((None, R, width), lambda i: (layer, i, 0))
    convspec = pl.BlockSpec((None, BB, CONV_W - 1, 3 * D_C), lambda i: (layer, i, 0, 0))
    anyspec = pl.BlockSpec(memory_space=pl.ANY)
    in_specs = [
        pl.BlockSpec((R, N_PROJ), lambda i: (blk0 + i, 0)),
        pl.BlockSpec((R, 128), lambda i: (blk0 + i, 0)),
        full((R, HEAD_DIM)), full((R, HEAD_DIM)), full((R, 128)),
        full((8, 128)), full((CONV_W, 3 * D_C)),
        full((H_A, HEAD_DIM)), full((H_B, HEAD_DIM)), full((1, HEAD_DIM)),
        st4(H_A), rows(D_A), rows(128), st4(H_B), st4(H_C), convspec,
    ]
    args = [proj, gates, cos, sin, rt, gp, cw, na, nb, nc, c0, nrep, mrep, r0, g0, conv0]
    out_shape = [
        jax.ShapeDtypeStruct(y.shape, y.dtype),
        jax.ShapeDtypeStruct(c0.shape, f32),
        jax.ShapeDtypeStruct(nrep.shape, f32),
        jax.ShapeDtypeStruct(mrep.shape, f32),
        jax.ShapeDtypeStruct(r0.shape, f32),
        jax.ShapeDtypeStruct(g0.shape, f32),
        jax.ShapeDtypeStruct(conv0.shape, f32),
    ]
    out_specs = [pl.BlockSpec((R, D_MODEL), lambda i: (blk0 + i, 0)),
                 st4(H_A), rows(D_A), rows(128), st4(H_B), st4(H_C), convspec]
    first = prev_out is None
    if first:
        in_specs.append(anyspec)
        args.append(y)
        aliases = {16: 0}
    else:
        in_specs += [anyspec] * 7
        args += [y] + list(prev_out)
        aliases = {16 + j: j for j in range(7)}
    del depth
    return pl.pallas_call(
        functools.partial(_sample_mixer_kernel, T, BB, first),
        grid=(nblk,),
        in_specs=in_specs,
        out_specs=out_specs,
        out_shape=out_shape,
        input_output_aliases=aliases,
        scratch_shapes=[pltpu.VMEM((R, 3 * D_C), f32), pltpu.VMEM((R, HEAD_DIM), f32), pltpu.VMEM((R, HEAD_DIM), f32),
                        pltpu.VMEM((BB, 8, 3 * D_C), f32)],
        compiler_params=_cparams(("parallel",)),
        name="sample_mixers",
    )(*args)


def _repack_w_in(w_in):
    n_a = 4 * D_A
    n_ag = n_a + 2 * H_A
    n_bc = 4 * D_B + 4 * D_C
    pad = jnp.zeros(w_in.shape[:2] + (128 - 2 * H_A - 2 * H_C,), w_in.dtype)
    main = jnp.concatenate([w_in[:, :, :n_a], w_in[:, :, n_ag:n_ag + n_bc]], axis=-1).astype(bf16)
    gate = jnp.concatenate([w_in[:, :, n_a:n_ag], w_in[:, :, n_ag + n_bc:], pad], axis=-1).astype(bf16)
    return main, gate


def _gate_params(gate_bias, dt_bias, a_log):
    depth = gate_bias.shape[0]
    gp = jnp.zeros((depth, 8, 128), f32)
    gp = gp.at[:, 0, LANE_I:LANE_I + 2 * H_A].set(gate_bias)
    gp = gp.at[:, 0, LANE_DECAY:LANE_DECAY + H_C].set(dt_bias)
    gp = gp.at[:, 1, LANE_DECAY:LANE_DECAY + H_C].set(a_log)
    return gp


def kernel(x_prompt, x_sample, state_mlstm_C, state_mlstm_n, state_mlstm_m, state_ret_S, state_gdn_S,
           state_gdn_conv, norm_mix_pre, norm_mix_post, norm_mlp_pre, norm_mlp_post, w_in, mlstm_gate_bias,
           gdn_conv_w, gdn_A_log, gdn_dt_bias, norm_mlstm, norm_ret, norm_gdn, w_out, w_up, w_down):
    bp, tp, d = x_prompt.shape
    bs, ts, _ = x_sample.shape
    depth = w_in.shape[0]
    rows_p = bp * tp
    rows_s = bs * ts
    rows = rows_p + rows_s

    x = jnp.concatenate([x_prompt.reshape(rows_p, d), x_sample.reshape(rows_s, d)], axis=0)
    w_in_p, w_gate_p = _repack_w_in(w_in)
    w_out_b = w_out.astype(bf16)
    w_down_b = w_down.astype(bf16)
    gp = _gate_params(mlstm_gate_bias, gdn_dt_bias, gdn_A_log)
    na = norm_mlstm.reshape(depth, H_A, HEAD_DIM)
    nb = norm_ret.reshape(depth, H_B, HEAD_DIM)
    nc = norm_gdn.reshape(depth, 1, HEAD_DIM)

    nrep = jnp.repeat(state_mlstm_n.reshape(depth, bs, D_A), ts, axis=1)
    mrep = jnp.pad(jnp.repeat(state_mlstm_m, ts, axis=1), ((0, 0), (0, 0), (0, 128 - H_A)))
    s_states = (state_mlstm_C, nrep, mrep, state_ret_S, state_gdn_S, state_gdn_conv)

    p_tables = _prompt_tables(tp, PROMPT_CHUNK)
    s_tables = _sample_tables(ts, SAMPLE_BLOCK)

    tm = rows // 8
    h = _rmsnorm_rows(x, norm_mix_pre[0], tm // 2)
    p_states = []
    s_out = None
    for l in range(depth):
        proj = _matmul_in(h, w_in_p, l, tm, 2048, "proj_in")
        gates = _matmul_in(h, w_gate_p, l, tm, 128, "proj_gates")
        outs = _prompt_mixers(proj, gates, p_tables, gp[l], gdn_conv_w[l], na[l], nb[l], nc[l], bp, tp, rows,
                              PROMPT_CHUNK)
        p_states.append(outs[1:])
        res = _sample_mixers(l, proj, gates, outs[0], s_tables, gp[l], gdn_conv_w[l], na[l], nb[l], nc[l],
                             s_states, s_out, bs, ts, rows_p, SAMPLE_BLOCK)
        y, s_out = res[0], res[1:]
        x, h = _matmul_out(y, w_out_b, l, x, norm_mix_post[l], norm_mlp_pre[l], tm // 2)
        u = _matmul_up(h, w_up, l, tm, 1024)
        x, h = _matmul_down(u, w_down_b, l, x, norm_mlp_post[l], norm_mix_pre[(l + 1) % depth], tm, 512)

    stk = lambda j: jnp.stack([s[j] for s in p_states], axis=0)
    sc, sn, sm, sr, sg, sconv = s_out
    return (
        x[:rows_p].reshape(bp, tp, d), x[rows_p:].reshape(bs, ts, d),
        stk(0), stk(1), stk(2)[:, :, 0, :H_A], stk(3), stk(4), stk(5),
        sc, sn[:, ts - 1::ts, :].reshape(depth, bs, H_A, HEAD_DIM), sm[:, ts - 1::ts, :H_A], sr, sg,
        sconv,
    )
```

```python
import functools
import math

import numpy as np
import jax
import jax.numpy as jnp
from jax import lax
from jax.experimental import pallas as pl
from jax.experimental.pallas import tpu as pltpu

f32 = jnp.float32
bf16 = jnp.bfloat16

D_MODEL = 2048
HEAD_DIM = 128
H_A, H_B, H_C = 4, 4, 8
D_A, D_B, D_C = H_A * HEAD_DIM, H_B * HEAD_DIM, H_C * HEAD_DIM
D_FF = 4 * D_MODEL
CONV_W = 4
PAST_LEN = 16384
ROPE_BASE = 10000.0
GATE_SOFTCAP = 15.0
NORM_EPS = 1e-6
QK_SCALE = HEAD_DIM ** -0.5

COL_A = 0
COL_B = 4 * D_A
COL_C = COL_B + 4 * D_B
N_PROJ = COL_C + 4 * D_C
LANE_I, LANE_F, LANE_DECAY, LANE_BETA = 0, H_A, 2 * H_A, 2 * H_A + H_C

PROMPT_CHUNK = 64
INV_BLOCK = 8
SAMPLE_BLOCK = 8
NEG_BIG = -1e30
VMEM_LIMIT = 60 * 1024 * 1024


def _cparams(sem):
    return pltpu.CompilerParams(dimension_semantics=sem, vmem_limit_bytes=VMEM_LIMIT)


def _rms(x, g):
    return x * lax.rsqrt(jnp.mean(x * x, axis=-1, keepdims=True) + NORM_EPS) * g


def _rmsnorm_kernel(x_ref, g_ref, o_ref):
    o_ref[...] = _rms(x_ref[...], g_ref[...]).astype(o_ref.dtype)


def _rmsnorm_rows(x, g, tm):
    m, d = x.shape
    return pl.pallas_call(
        _rmsnorm_kernel,
        grid=(m // tm,),
        in_specs=[pl.BlockSpec((tm, d), lambda i: (i, 0)), pl.BlockSpec((1, d), lambda i: (0, 0))],
        out_specs=pl.BlockSpec((tm, d), lambda i: (i, 0)),
        out_shape=jax.ShapeDtypeStruct((m, d), bf16),
        compiler_params=_cparams(("parallel",)),
        name="rmsnorm_in",
    )(x, g.reshape(1, d))


def _mm_kernel(a_ref, w_ref, o_ref):
    o_ref[...] = jnp.dot(a_ref[...], w_ref[...], preferred_element_type=f32)


def _matmul_in(h, w, layer, tm, tn, name):
    m, k = h.shape
    n = w.shape[2]
    tn = min(tn, n)
    return pl.pallas_call(
        _mm_kernel,
        grid=(n // tn, m // tm),
        in_specs=[pl.BlockSpec((tm, k), lambda j, i: (i, 0)), pl.BlockSpec((None, k, tn), lambda j, i: (layer, 0, j))],
        out_specs=pl.BlockSpec((tm, tn), lambda j, i: (i, j)),
        out_shape=jax.ShapeDtypeStruct((m, n), f32),
        compiler_params=_cparams(("parallel", "arbitrary")),
        name=name,
    )(h, w)


def _mm_up_kernel(a_ref, w_ref, o_ref, wb_ref):
    @pl.when(pl.program_id(1) == 0)
    def _():
        wb_ref[...] = w_ref[...].astype(bf16)

    z = jnp.maximum(jnp.dot(a_ref[...], wb_ref[...], preferred_element_type=f32), 0.0)
    o_ref[...] = (z * z).astype(o_ref.dtype)


def _matmul_up(h, w, layer, tm, tn):
    m, k = h.shape
    n = w.shape[2]
    return pl.pallas_call(
        _mm_up_kernel,
        grid=(n // tn, m // tm),
        in_specs=[pl.BlockSpec((tm, k), lambda j, i: (i, 0)), pl.BlockSpec((None, k, tn), lambda j, i: (layer, 0, j))],
        out_specs=pl.BlockSpec((tm, tn), lambda j, i: (i, j)),
        out_shape=jax.ShapeDtypeStruct((m, n), bf16),
        scratch_shapes=[pltpu.VMEM((k, tn), bf16)],
        compiler_params=_cparams(("parallel", "arbitrary")),
        name="mlp_up",
    )(h, w)


def _residual_epilogue(z, x_ref, gpost_ref, gnext_ref, xo_ref, ho_ref):
    x_new = x_ref[...] + _rms(z, gpost_ref[...])
    xo_ref[...] = x_new
    ho_ref[...] = _rms(x_new, gnext_ref[...]).astype(ho_ref.dtype)


def _mm_out_kernel(a_ref, w_ref, x_ref, gpost_ref, gnext_ref, xo_ref, ho_ref):
    z = jnp.dot(a_ref[...], w_ref[...], preferred_element_type=f32)
    _residual_epilogue(z, x_ref, gpost_ref, gnext_ref, xo_ref, ho_ref)


def _matmul_out(y, w, layer, x, g_post, g_next, tm):
    m, k = y.shape
    d = w.shape[2]
    row = lambda i: (i, 0)
    fixed = lambda i: (0, 0)
    return pl.pallas_call(
        _mm_out_kernel,
        grid=(m // tm,),
        in_specs=[pl.BlockSpec((tm, k), row), pl.BlockSpec((None, k, d), lambda i: (layer, 0, 0)),
                  pl.BlockSpec((tm, d), row),
                  pl.BlockSpec((1, d), fixed), pl.BlockSpec((1, d), fixed)],
        out_specs=[pl.BlockSpec((tm, d), row), pl.BlockSpec((tm, d), row)],
        out_shape=[jax.ShapeDtypeStruct((m, d), f32), jax.ShapeDtypeStruct((m, d), bf16)],
        compiler_params=_cparams(("parallel",)),
        name="proj_out",
    )(y, w, x, g_post.reshape(1, d), g_next.reshape(1, d))


def _mm_down_kernel(a_ref, w_ref, x_ref, gpost_ref, gnext_ref, xo_ref, ho_ref):
    kk = pl.program_id(1)

    @pl.when(kk == 0)
    def _():
        xo_ref[...] = jnp.zeros_like(xo_ref)

    xo_ref[...] += jnp.dot(a_ref[...], w_ref[...], preferred_element_type=f32)

    @pl.when(kk == pl.num_programs(1) - 1)
    def _():
        _residual_epilogue(xo_ref[...], x_ref, gpost_ref, gnext_ref, xo_ref, ho_ref)


def _matmul_down(u, w, layer, x, g_post, g_next, tm, tk):
    m, k = u.shape
    d = w.shape[2]
    row = lambda i, kk: (i, 0)
    fixed = lambda i, kk: (0, 0)
    return pl.pallas_call(
        _mm_down_kernel,
        grid=(m // tm, k // tk),
        in_specs=[pl.BlockSpec((tm, tk), lambda i, kk: (i, kk)),
                  pl.BlockSpec((None, tk, d), lambda i, kk: (layer, kk, 0)),
                  pl.BlockSpec((tm, d), row), pl.BlockSpec((1, d), fixed), pl.BlockSpec((1, d), fixed)],
        out_specs=[pl.BlockSpec((tm, d), row), pl.BlockSpec((tm, d), row)],
        out_shape=[jax.ShapeDtypeStruct((m, d), f32), jax.ShapeDtypeStruct((m, d), bf16)],
        compiler_params=_cparams(("parallel", "arbitrary")),
        name="mlp_down",
    )(u, w, x, g_post.reshape(1, d), g_next.reshape(1, d))


def _dot(a, b):
    return jnp.dot(a, b, preferred_element_type=f32)


def _dot_tb(a, b):
    return lax.dot_general(a, b, (((1,), (1,)), ((), ())), preferred_element_type=f32)


def _dot_ta(a, b):
    return lax.dot_general(a, b, (((0,), (0,)), ((), ())), preferred_element_type=f32)


def _lane_sum(x):
    return jnp.sum(x, axis=-1, keepdims=True)


def _softplus(z):
    return jnp.maximum(z, 0.0) + jnp.log1p(jnp.exp(-jnp.abs(z)))


def _gate_transform(gt, gp):
    z = gt + gp[0:1, :]
    capped = GATE_SOFTCAP * jnp.tanh(z / GATE_SOFTCAP)
    log_f = -_softplus(-capped)
    decay = -jnp.exp(gp[1:2, :]) * _softplus(z)
    beta = jax.nn.sigmoid(gt)
    return capped, log_f, decay, beta


def _head_norm(x, gain, center):
    if center:
        x = x - jnp.mean(x, axis=-1, keepdims=True)
    return x * lax.rsqrt(jnp.mean(x * x, axis=-1, keepdims=True) + NORM_EPS) * gain


def _l2norm(x):
    return x * lax.rsqrt(_lane_sum(x * x) + NORM_EPS)


def _silu(x):
    return x * jax.nn.sigmoid(x)


def _rotary(x, cos, sin_signed):
    return x * cos + pltpu.roll(x, HEAD_DIM // 2, 1) * sin_signed


def _retention_log_gamma():
    return np.log1p(-np.exp2(-5.0 - np.arange(H_B, dtype=np.float64)))


def _rope_tables(pos):
    half = HEAD_DIM // 2
    inv = ROPE_BASE ** (-np.arange(half, dtype=np.float64) / half)
    ang = np.asarray(pos, dtype=np.float64)[:, None] * inv[None, :]
    cos = np.concatenate([np.cos(ang), np.cos(ang)], axis=-1)
    sin = np.concatenate([-np.sin(ang), np.sin(ang)], axis=-1)
    return cos.astype(np.float32), sin.astype(np.float32)


def _scan_rows(x, row, length, op, fill):
    s = 1
    while s < length:
        x = op(x, jnp.where(row >= s, pltpu.roll(x, s, 0), fill))
        s *= 2
    return x


def _rows_to_lanes(x, length):
    if length < 128:
        x = jnp.concatenate([x, jnp.zeros((128 - length, 128), x.dtype)], axis=0)
    return x.T[:, :length]


def _prompt_mixer_kernel(L, proj_ref, gt_ref, cos_ref, sin_ref, rd_ref, rq_ref, rk_ref, gp_ref, cw_ref,
                         na_ref, nb_ref, nc_ref,
                         y_ref, c_ref, n_ref, m_ref, r_ref, g_ref, conv_ref, ext_ref):
    c = pl.program_id(1)

    @pl.when(c == 0)
    def _():
        c_ref[...] = jnp.zeros_like(c_ref)
        n_ref[...] = jnp.zeros_like(n_ref)
        m_ref[...] = jnp.zeros_like(m_ref)
        r_ref[...] = jnp.zeros_like(r_ref)
        g_ref[...] = jnp.zeros_like(g_ref)
        ext_ref[0:8, :] = jnp.zeros((8, 3 * D_C), f32)

    row = lax.broadcasted_iota(jnp.int32, (L, 128), 0)
    lane = lax.broadcasted_iota(jnp.int32, (L, 128), 1)
    ri = lax.broadcasted_iota(jnp.int32, (L, L), 0)
    ci = lax.broadcasted_iota(jnp.int32, (L, L), 1)
    incl = ri >= ci
    strict = ri > ci

    capped, log_f, decay, beta = _gate_transform(gt_ref[...], gp_ref[...])

    head_lane = lane < H_A
    ig = jnp.where(head_lane, capped, 0.0)
    lf = jnp.where(head_lane, pltpu.roll(log_f, 128 - LANE_F, 1), 0.0)
    mp = m_ref[0]
    F = _scan_rows(lf, row, L, jnp.add, 0.0)
    m = F + jnp.maximum(mp, _scan_rows(ig - F, row, L, jnp.maximum, NEG_BIG))
    m_new = m[L - 1:L, :]
    f_last = F[L - 1:L, :]
    a_rows = F - m
    b_lanes = _rows_to_lanes(ig - F, L)
    inter = jnp.exp(F + mp - m)
    inv_floor = jnp.exp(-m)
    wl = jnp.exp(ig + f_last - F - m_new)
    dec = jnp.exp(f_last + mp - m_new)
    m_ref[0] = m_new

    def mlstm_head(h):
        q = proj_ref[:, COL_A + h * HEAD_DIM:COL_A + (h + 1) * HEAD_DIM]
        k = proj_ref[:, COL_A + D_A + h * HEAD_DIM:COL_A + D_A + (h + 1) * HEAD_DIM] * QK_SCALE
        v = proj_ref[:, COL_A + 2 * D_A + h * HEAD_DIM:COL_A + 2 * D_A + (h + 1) * HEAD_DIM]
        og = proj_ref[:, COL_A + 3 * D_A + h * HEAD_DIM:COL_A + 3 * D_A + (h + 1) * HEAD_DIM]
        logw = a_rows[:, h:h + 1] + b_lanes[h:h + 1, :]
        s = _dot_tb(q, k) * jnp.exp(jnp.where(incl, logw, NEG_BIG))
        c_old = c_ref[0, h]
        n_old = n_ref[0, h:h + 1, :]
        inter_h = inter[:, h:h + 1]
        num = inter_h * _dot(q, c_old) + _dot(s, v)
        den = inter_h * _lane_sum(q * n_old) + _lane_sum(s)
        hh = num / jnp.maximum(jnp.abs(den), inv_floor[:, h:h + 1])
        kw = k * wl[:, h:h + 1]
        dec_h = dec[:, h:h + 1]
        c_ref[0, h] = dec_h * c_old + _dot_ta(kw, v)
        n_ref[0, h:h + 1, :] = dec_h * n_old + jnp.sum(kw, axis=0, keepdims=True)
        ya = jax.nn.sigmoid(og) * _head_norm(hh, na_ref[h:h + 1, :], False)
        y_ref[:, h * HEAD_DIM:(h + 1) * HEAD_DIM] = ya.astype(y_ref.dtype)

    lg = _retention_log_gamma()

    def retention_head(h):
        cos = cos_ref[...]
        sin = sin_ref[...]
        q = proj_ref[:, COL_B + h * HEAD_DIM:COL_B + (h + 1) * HEAD_DIM]
        k = proj_ref[:, COL_B + D_B + h * HEAD_DIM:COL_B + D_B + (h + 1) * HEAD_DIM]
        v = proj_ref[:, COL_B + 2 * D_B + h * HEAD_DIM:COL_B + 2 * D_B + (h + 1) * HEAD_DIM]
        gb = proj_ref[:, COL_B + 3 * D_B + h * HEAD_DIM:COL_B + 3 * D_B + (h + 1) * HEAD_DIM]
        qr = _rotary(q, cos, sin)
        kr = _rotary(k, cos, sin) * QK_SCALE
        s_old = r_ref[0, h]
        inner = _dot_tb(qr, kr) * rd_ref[h]
        o = _dot(inner, v) + rq_ref[h] * _dot(qr, s_old)
        r_ref[0, h] = float(np.exp(L * lg[h])) * s_old + _dot_ta(kr * rk_ref[h], v)
        yb = _silu(gb) * _head_norm(o, nb_ref[h:h + 1, :], True)
        y_ref[:, D_A + h * HEAD_DIM:D_A + (h + 1) * HEAD_DIM] = yb.astype(y_ref.dtype)

    fillers = [functools.partial(mlstm_head, h) for h in range(H_A)]
    fillers += [functools.partial(retention_head, h) for h in range(H_B)]

    def emit_fillers(count):
        for _ in range(min(count, len(fillers))):
            fillers.pop(0)()

    ext_ref[8:8 + L, :] = proj_ref[:, COL_C:COL_C + 3 * D_C]
    G = _scan_rows(decay, row, L, jnp.add, 0.0)
    g_lanes = _rows_to_lanes(G, L)
    gam = jnp.exp(G)
    g_last = G[L - 1:L, :]
    k_decay = jnp.exp(g_last - G)
    s_decay = jnp.exp(g_last)

    def conv_block(col):
        acc = cw_ref[CONV_W - 1:CONV_W, col:col + HEAD_DIM] * ext_ref[8:8 + L, col:col + HEAD_DIM]
        for w in range(CONV_W - 1):
            off = 8 - (CONV_W - 1) + w
            acc = acc + cw_ref[w:w + 1, col:col + HEAD_DIM] * ext_ref[off:off + L, col:col + HEAD_DIM]
        return _silu(acc)

    heads = range(H_C)
    dcol = [LANE_DECAY + h for h in heads]
    beta_c = [beta[:, LANE_BETA + h:LANE_BETA + h + 1] for h in heads]
    q = [_l2norm(conv_block(h * HEAD_DIM)) * QK_SCALE for h in heads]
    k = [_l2norm(conv_block(D_C + h * HEAD_DIM)) for h in heads]
    v = [conv_block(2 * D_C + h * HEAD_DIM) for h in heads]
    dec_in = [jnp.exp(jnp.where(incl, G[:, dcol[h]:dcol[h] + 1] - g_lanes[dcol[h]:dcol[h] + 1, :], NEG_BIG))
              for h in heads]
    a_mat = [jnp.where(strict, dec_in[h], 0.0) * beta_c[h] * _dot_tb(k[h], k[h]) for h in heads]
    qk = [_dot_tb(q[h], k[h]) * dec_in[h] for h in heads]
    blk = min(INV_BLOCK, L)
    same = (ri // blk) == (ci // blk)
    pw = [jnp.where(same, -a, 0.0) for a in a_mat]
    e_mat = list(pw)
    span = 1
    while 2 * span < blk:
        pw = [_dot(p, p) for p in pw]
        e_mat = [e_mat[h] + pw[h] + _dot(e_mat[h], pw[h]) for h in heads]
        emit_fillers(1)
        span *= 2
    while blk < L:
        wider = (ri // (2 * blk)) == (ci // (2 * blk))
        a_off = [jnp.where(wider & jnp.logical_not(same), a, 0.0) for a in a_mat]
        low = [a_off[h] + _dot(e_mat[h], a_off[h]) for h in heads]
        e_mat = [e_mat[h] - low[h] - _dot(low[h], e_mat[h]) for h in heads]
        emit_fillers(2)
        same = wider
        blk *= 2
    rhs = [jnp.concatenate([(beta_c[h] * gam[:, dcol[h]:dcol[h] + 1]) * k[h], beta_c[h] * v[h]], axis=1)
           for h in heads]
    sol = [rhs[h] + _dot(e_mat[h], rhs[h]) for h in heads]
    emit_fillers(len(fillers))
    s_old = [g_ref[0, h] for h in heads]
    both = [_dot(jnp.concatenate([sol[h][:, :HEAD_DIM], q[h]], axis=0), s_old[h]) for h in heads]
    u = [sol[h][:, HEAD_DIM:] - both[h][:L] for h in heads]
    o = [gam[:, dcol[h]:dcol[h] + 1] * both[h][L:] + _dot(qk[h], u[h]) for h in heads]
    for h in heads:
        g_ref[0, h] = s_decay[:, dcol[h]:dcol[h] + 1] * s_old[h] + _dot_ta(k[h] * k_decay[:, dcol[h]:dcol[h] + 1], u[h])
    for h in heads:
        gz = proj_ref[:, COL_C + 3 * D_C + h * HEAD_DIM:COL_C + 3 * D_C + (h + 1) * HEAD_DIM]
        yc = _silu(gz) * _head_norm(o[h], nc_ref[...], False)
        y_ref[:, D_A + D_B + h * HEAD_DIM:D_A + D_B + (h + 1) * HEAD_DIM] = yc.astype(y_ref.dtype)

    ext_ref[0:8, :] = ext_ref[L:L + 8, :]

    @pl.when(c == pl.num_programs(1) - 1)
    def _():
        conv_ref[0] = ext_ref[8 + L - (CONV_W - 1):8 + L, :]


def _prompt_tables(seq, L):
    lg = _retention_log_gamma()
    i = np.arange(L, dtype=np.float64)
    diff = i[:, None] - i[None, :]
    rd = np.where(diff >= 0, np.exp(np.maximum(diff, 0.0) * lg[:, None, None]), 0.0)
    rq = np.broadcast_to(np.exp((i + 1.0) * lg[:, None])[..., None], (H_B, L, HEAD_DIM))
    rk = np.broadcast_to(np.exp((L - 1.0 - i) * lg[:, None])[..., None], (H_B, L, HEAD_DIM))
    cos, sin = _rope_tables(np.arange(seq))
    return (jnp.asarray(cos), jnp.asarray(sin), jnp.asarray(rd, f32), jnp.asarray(rq, f32), jnp.asarray(rk, f32))


def _prompt_mixers(proj, gates, tables, gp, cw, na, nb, nc, batch, seq, total_rows, L):
    nchunk = seq // L
    cos, sin, rd, rq, rk = tables
    full = lambda shape: pl.BlockSpec(shape, lambda b, c: (0,) * len(shape))
    state4 = lambda heads: pl.BlockSpec((1, heads, HEAD_DIM, HEAD_DIM), lambda b, c: (b, 0, 0, 0))
    return pl.pallas_call(
        functools.partial(_prompt_mixer_kernel, L),
        grid=(batch, nchunk),
        in_specs=[
            pl.BlockSpec((L, N_PROJ), lambda b, c: (b * nchunk + c, 0)),
            pl.BlockSpec((L, 128), lambda b, c: (b * nchunk + c, 0)),
            pl.BlockSpec((L, HEAD_DIM), lambda b, c: (c, 0)),
            pl.BlockSpec((L, HEAD_DIM), lambda b, c: (c, 0)),
            full((H_B, L, L)), full((H_B, L, HEAD_DIM)), full((H_B, L, HEAD_DIM)),
            full((8, 128)), full((CONV_W, 3 * D_C)),
            full((H_A, HEAD_DIM)), full((H_B, HEAD_DIM)), full((1, HEAD_DIM)),
        ],
        out_specs=[
            pl.BlockSpec((L, D_MODEL), lambda b, c: (b * nchunk + c, 0)),
            state4(H_A),
            pl.BlockSpec((1, H_A, HEAD_DIM), lambda b, c: (b, 0, 0)),
            pl.BlockSpec((1, 1, 128), lambda b, c: (b, 0, 0)),
            state4(H_B),
            state4(H_C),
            pl.BlockSpec((1, CONV_W - 1, 3 * D_C), lambda b, c: (b, 0, 0)),
        ],
        out_shape=[
            jax.ShapeDtypeStruct((total_rows, D_MODEL), bf16),
            jax.ShapeDtypeStruct((batch, H_A, HEAD_DIM, HEAD_DIM), f32),
            jax.ShapeDtypeStruct((batch, H_A, HEAD_DIM), f32),
            jax.ShapeDtypeStruct((batch, 1, 128), f32),
            jax.ShapeDtypeStruct((batch, H_B, HEAD_DIM, HEAD_DIM), f32),
            jax.ShapeDtypeStruct((batch, H_C, HEAD_DIM, HEAD_DIM), f32),
            jax.ShapeDtypeStruct((batch, CONV_W - 1, 3 * D_C), f32),
        ],
        scratch_shapes=[pltpu.VMEM((L + 8, 3 * D_C), f32)],
        compiler_params=_cparams(("parallel", "arbitrary")),
        name="prompt_mixers",
    )(proj, gates, cos, sin, rd, rq, rk, gp, cw, na, nb, nc)


def _sample_mixer_kernel(T, BB, layer_first, *refs):
    (proj_ref, gt_ref, cos_ref, sin_ref, rt_ref, gp_ref, cw_ref, na_ref, nb_ref, nc_ref,
     c0_ref, nrep_ref, mrep_ref, r0_ref, g0_ref, conv0_ref) = refs[:16]
    rest = refs[16 + (1 if layer_first else 7):]
    (y_ref, c_ref, nout_ref, mout_ref, r_ref, g_ref, conv_ref, cv_ref, sa_ref, ext_ref) = rest
    R = T * BB
    row = lax.broadcasted_iota(jnp.int32, (R, 128), 0)
    lane = lax.broadcasted_iota(jnp.int32, (R, 128), 1)
    t = lax.rem(row, T)
    tcol = t[:, 0:1]

    def shift(x, s):
        return x if s == 0 else pltpu.roll(x, s, 0)

    def seg_scan(x, op, fill):
        s = 1
        while s < T:
            x = op(x, jnp.where(t >= s, shift(x, s), fill))
            s *= 2
        return x

    def last_rep(x):
        x_last = jnp.where(t == T - 1, x, 0.0)
        out = x_last
        for s in range(1, T):
            out = out + pltpu.roll(x_last, R - s, 0)
        return out

    capped, log_f, decay, beta = _gate_transform(gt_ref[...], gp_ref[...])

    head_lane = lane < H_A
    ig = jnp.where(head_lane, capped, 0.0)
    lf = jnp.where(head_lane, pltpu.roll(log_f, 128 - LANE_F, 1), 0.0)
    mp = mrep_ref[...]
    F = seg_scan(lf, jnp.add, 0.0)
    m = F + jnp.maximum(mp, seg_scan(ig - F, jnp.maximum, NEG_BIG))
    m_new = last_rep(m)
    f_last = last_rep(F)
    inter = jnp.exp(F + mp - m)
    inv_floor = jnp.exp(-m)
    wl = jnp.exp(ig + f_last - F - m_new)
    dec = jnp.exp(f_last + mp - m_new)
    mout_ref[...] = m_new
    pw = [jnp.where(t >= s, jnp.exp(F - shift(F, s) + shift(ig, s) - m), 0.0) for s in range(T)]

    def seq(x, b):
        return x[b * T:(b + 1) * T]

    slot_a, slot_b, slot_w, slot_q = 0, H_A, H_A + H_B, H_A + H_B + H_C

    qa = [proj_ref[:, COL_A + h * HEAD_DIM:COL_A + (h + 1) * HEAD_DIM] for h in range(H_A)]
    for h in range(H_A):
        for b in range(BB):
            sa_ref[slot_a + h, b * T:(b + 1) * T, :] = _dot(seq(qa[h], b), c0_ref[b, h])
    cos = cos_ref[...]
    sin = sin_ref[...]
    lg = _retention_log_gamma()
    qr = [_rotary(proj_ref[:, COL_B + h * HEAD_DIM:COL_B + (h + 1) * HEAD_DIM], cos, sin) for h in range(H_B)]
    kr = [_rotary(proj_ref[:, COL_B + D_B + h * HEAD_DIM:COL_B + D_B + (h + 1) * HEAD_DIM], cos, sin) * QK_SCALE
          for h in range(H_B)]
    for h in range(H_B):
        for b in range(BB):
            sa_ref[slot_b + h, b * T:(b + 1) * T, :] = _dot(seq(qr[h], b), r0_ref[b, h])

    ka = [proj_ref[:, COL_A + D_A + h * HEAD_DIM:COL_A + D_A + (h + 1) * HEAD_DIM] * QK_SCALE for h in range(H_A)]
    va = [proj_ref[:, COL_A + 2 * D_A + h * HEAD_DIM:COL_A + 2 * D_A + (h + 1) * HEAD_DIM] for h in range(H_A)]
    n_old = [nrep_ref[:, h * HEAD_DIM:(h + 1) * HEAD_DIM] for h in range(H_A)]
    kw = [ka[h] * wl[:, h:h + 1] for h in range(H_A)]
    for h in range(H_A):
        for b in range(BB):
            dec_bh = dec[b * T:b * T + 1, h:h + 1]
            c_ref[b, h] = dec_bh * c0_ref[b, h] + _dot_ta(seq(kw[h], b), seq(va[h], b))
        nout_ref[:, h * HEAD_DIM:(h + 1) * HEAD_DIM] = dec[:, h:h + 1] * n_old[h] + seg_scan(kw[h], jnp.add, 0.0)
    vb = [proj_ref[:, COL_B + 2 * D_B + h * HEAD_DIM:COL_B + 2 * D_B + (h + 1) * HEAD_DIM] for h in range(H_B)]
    for h in range(H_B):
        kd = kr[h] * rt_ref[:, H_B + h:H_B + h + 1]
        for b in range(BB):
            r_ref[b, h] = float(np.exp(T * lg[h])) * r0_ref[b, h] + _dot_ta(seq(kd, b), seq(vb[h], b))

    for b in range(BB):
        ext_ref[b, 0:CONV_W - 1, :] = conv0_ref[b]
        ext_ref[b, CONV_W - 1:CONV_W - 1 + T, :] = proj_ref[b * T:(b + 1) * T, COL_C:COL_C + 3 * D_C]
        acc = cw_ref[0:1, :] * ext_ref[b, 0:T, :]
        for w in range(1, CONV_W):
            acc = acc + cw_ref[w:w + 1, :] * ext_ref[b, w:w + T, :]
        cv_ref[b * T:(b + 1) * T, :] = _silu(acc)
        conv_ref[b] = ext_ref[b, T:T + CONV_W - 1, :]

    G = seg_scan(decay, jnp.add, 0.0)
    gam = jnp.exp(G)
    g_last = last_rep(G)
    k_decay = jnp.exp(g_last - G)
    s_decay = jnp.exp(g_last)
    dshift = [None] + [jnp.where(t >= s, jnp.exp(G - shift(G, s)), 0.0) for s in range(1, T)]

    heads = range(H_C)
    dcol = [LANE_DECAY + h for h in heads]
    beta_c = [beta[:, LANE_BETA + h:LANE_BETA + h + 1] for h in heads]
    qg = [_l2norm(cv_ref[:, h * HEAD_DIM:(h + 1) * HEAD_DIM]) * QK_SCALE for h in heads]
    kg = [_l2norm(cv_ref[:, D_C + h * HEAD_DIM:D_C + (h + 1) * HEAD_DIM]) for h in heads]
    vg = [cv_ref[:, 2 * D_C + h * HEAD_DIM:2 * D_C + (h + 1) * HEAD_DIM] for h in heads]
    a_sub = [[None] + [beta_c[h] * dshift[s][:, dcol[h]:dcol[h] + 1] * _lane_sum(kg[h] * shift(kg[h], s))
                       for s in range(1, T)] for h in heads]
    rhs_w = [(beta_c[h] * gam[:, dcol[h]:dcol[h] + 1]) * kg[h] for h in heads]
    rhs_u = [beta_c[h] * vg[h] for h in heads]
    w_sol, u_sol = list(rhs_w), list(rhs_u)
    for i in range(1, T):
        for h in heads:
            upd_w = a_sub[h][1] * shift(w_sol[h], 1)
            upd_u = a_sub[h][1] * shift(u_sol[h], 1)
            for s in range(2, i + 1):
                upd_w = upd_w + a_sub[h][s] * shift(w_sol[h], s)
                upd_u = upd_u + a_sub[h][s] * shift(u_sol[h], s)
            w_sol[h] = jnp.where(t == i, rhs_w[h] - upd_w, w_sol[h])
            u_sol[h] = jnp.where(t == i, rhs_u[h] - upd_u, u_sol[h])
    for h in heads:
        for b in range(BB):
            both = _dot(jnp.concatenate([seq(w_sol[h], b), seq(qg[h], b)], axis=0), g0_ref[b, h])
            sa_ref[slot_w + h, b * T:(b + 1) * T, :] = both[:T]
            sa_ref[slot_q + h, b * T:(b + 1) * T, :] = both[T:]

    ug = [u_sol[h] - sa_ref[slot_w + h] for h in heads]
    for h in heads:
        kd = kg[h] * k_decay[:, dcol[h]:dcol[h] + 1]
        for b in range(BB):
            g_ref[b, h] = (s_decay[b * T:b * T + 1, dcol[h]:dcol[h] + 1] * g0_ref[b, h]
                           + _dot_ta(seq(kd, b), seq(ug[h], b)))

    sc_a = [[_lane_sum(qa[h] * shift(ka[h], s)) * pw[s][:, h:h + 1] for s in range(T)] for h in range(H_A)]
    qn = [_lane_sum(qa[h] * n_old[h]) for h in range(H_A)]
    sc_b = [[jnp.where(tcol >= s, _lane_sum(qr[h] * shift(kr[h], s)) * float(np.exp(s * lg[h])), 0.0)
             for s in range(T)] for h in range(H_B)]
    sc_c = [[_lane_sum(qg[h] * shift(kg[h], s)) * (1.0 if s == 0 else dshift[s][:, dcol[h]:dcol[h] + 1])
             for s in range(T)] for h in heads]
    hid_a, hid_b, hid_c = [], [], []
    for h in range(H_A):
        inter_h = inter[:, h:h + 1]
        num = inter_h * sa_ref[slot_a + h]
        den = inter_h * qn[h]
        for s in range(T):
            num = num + sc_a[h][s] * shift(va[h], s)
            den = den + sc_a[h][s]
        hid_a.append(num / jnp.maximum(jnp.abs(den), inv_floor[:, h:h + 1]))
    for h in range(H_B):
        o = rt_ref[:, h:h + 1] * sa_ref[slot_b + h]
        for s in range(T):
            o = o + sc_b[h][s] * shift(vb[h], s)
        hid_b.append(o)
    for h in heads:
        o = gam[:, dcol[h]:dcol[h] + 1] * sa_ref[slot_q + h]
        for s in range(T):
            o = o + sc_c[h][s] * shift(ug[h], s)
        hid_c.append(o)
    mean_b = [jnp.mean(x, axis=-1, keepdims=True) for x in hid_b]
    hid_b = [x - mu for x, mu in zip(hid_b, mean_b)]
    hidden = hid_a + hid_b + hid_c
    inv_rms = [lax.rsqrt(jnp.mean(x * x, axis=-1, keepdims=True) + NORM_EPS) for x in hidden]
    for h in range(H_A):
        og = proj_ref[:, COL_A + 3 * D_A + h * HEAD_DIM:COL_A + 3 * D_A + (h + 1) * HEAD_DIM]
        ya = jax.nn.sigmoid(og) * (hidden[h] * inv_rms[h] * na_ref[h:h + 1, :])
        y_ref[:, h * HEAD_DIM:(h + 1) * HEAD_DIM] = ya.astype(y_ref.dtype)
    for h in range(H_B):
        gb = proj_ref[:, COL_B + 3 * D_B + h * HEAD_DIM:COL_B + 3 * D_B + (h + 1) * HEAD_DIM]
        yb = _silu(gb) * (hidden[H_A + h] * inv_rms[H_A + h] * nb_ref[h:h + 1, :])
        y_ref[:, D_A + h * HEAD_DIM:D_A + (h + 1) * HEAD_DIM] = yb.astype(y_ref.dtype)
    for h in heads:
        gz = proj_ref[:, COL_C + 3 * D_C + h * HEAD_DIM:COL_C + 3 * D_C + (h + 1) * HEAD_DIM]
        yc = _silu(gz) * (hidden[H_A + H_B + h] * inv_rms[H_A + H_B + h] * nc_ref[...])
        y_ref[:, D_A + D_B + h * HEAD_DIM:D_A + D_B + (h + 1) * HEAD_DIM] = yc.astype(y_ref.dtype)


def _sample_tables(T, BB):
    lg = _retention_log_gamma()
    tt = np.arange(T, dtype=np.float64)
    rt = np.zeros((T, 128), np.float64)
    rt[:, 0:H_B] = np.exp((tt[:, None] + 1.0) * lg[None, :])
    rt[:, H_B:2 * H_B] = np.exp((T - 1.0 - tt[:, None]) * lg[None, :])
    cos, sin = _rope_tables(PAST_LEN + np.arange(T))
    rep = lambda a: jnp.asarray(np.tile(a, (BB, 1)), f32)
    return rep(cos), rep(sin), rep(rt)


def _sample_mixers(layer, proj, gates, y, tables, gp, cw, na, nb, nc, states, prev_out, batch, T, row0, BB):
    c0, nrep, mrep, r0, g0, conv0 = states
    depth = c0.shape[0]
    R = T * BB
    nblk = batch // BB
    blk0 = row0 // R
    cos, sin, rt = tables
    full = lambda shape: pl.BlockSpec(shape, lambda i: (0,) * len(shape))
    st4 = lambda heads: pl.BlockSpec((None, BB, heads, HEAD_DIM, HEAD_DIM), lambda i: (layer, i, 0, 0, 0))
    rows = lambda width: pl.BlockSpec((None, R, width), lambda i: (layer, i, 0))
    convspec = pl.BlockSpec((None, BB, CONV_W - 1, 3 * D_C), lambda i: (layer, i, 0, 0))
    anyspec = pl.BlockSpec(memory_space=pl.ANY)
    in_specs = [
        pl.BlockSpec((R, N_PROJ), lambda i: (blk0 + i, 0)),
        pl.BlockSpec((R, 128), lambda i: (blk0 + i, 0)),
        full((R, HEAD_DIM)), full((R, HEAD_DIM)), full((R, 128)),
        full((8, 128)), full((CONV_W, 3 * D_C)),
        full((H_A, HEAD_DIM)), full((H_B, HEAD_DIM)), full((1, HEAD_DIM)),
        st4(H_A), rows(D_A), rows(128), st4(H_B), st4(H_C), convspec,
    ]
    args = [proj, gates, cos, sin, rt, gp, cw, na, nb, nc, c0, nrep, mrep, r0, g0, conv0]
    out_shape = [
        jax.ShapeDtypeStruct(y.shape, y.dtype),
        jax.ShapeDtypeStruct(c0.shape, f32),
        jax.ShapeDtypeStruct(nrep.shape, f32),
        jax.ShapeDtypeStruct(mrep.shape, f32),
        jax.ShapeDtypeStruct(r0.shape, f32),
        jax.ShapeDtypeStruct(g0.shape, f32),
        jax.ShapeDtypeStruct(conv0.shape, f32),
    ]
    out_specs = [pl.BlockSpec((R, D_MODEL), lambda i: (blk0 + i, 0)),
                 st4(H_A), rows(D_A), rows(128), st4(H_B), st4(H_C), convspec]
    first = prev_out is None
    if first:
        in_specs.append(anyspec)
        args.append(y)
        aliases = {16: 0}
    else:
        in_specs += [anyspec] * 7
        args += [y] + list(prev_out)
        aliases = {16 + j: j for j in range(7)}
    del depth
    return pl.pallas_call(
        functools.partial(_sample_mixer_kernel, T, BB, first),
        grid=(nblk,),
        in_specs=in_specs,
        out_specs=out_specs,
        out_shape=out_shape,
        input_output_aliases=aliases,
        scratch_shapes=[pltpu.VMEM((R, 3 * D_C), f32), pltpu.VMEM((H_A + H_B + 2 * H_C, R, HEAD_DIM), f32),
                        pltpu.VMEM((BB, 8, 3 * D_C), f32)],
        compiler_params=_cparams(("parallel",)),
        name="sample_mixers",
    )(*args)


def _repack_w_in(w_in):
    n_a = 4 * D_A
    n_ag = n_a + 2 * H_A
    n_bc = 4 * D_B + 4 * D_C
    pad = jnp.zeros(w_in.shape[:2] + (128 - 2 * H_A - 2 * H_C,), w_in.dtype)
    main = jnp.concatenate([w_in[:, :, :n_a], w_in[:, :, n_ag:n_ag + n_bc]], axis=-1).astype(bf16)
    gate = jnp.concatenate([w_in[:, :, n_a:n_ag], w_in[:, :, n_ag + n_bc:], pad], axis=-1).astype(bf16)
    return main, gate


def _gate_params(gate_bias, dt_bias, a_log):
    depth = gate_bias.shape[0]
    gp = jnp.zeros((depth, 8, 128), f32)
    gp = gp.at[:, 0, LANE_I:LANE_I + 2 * H_A].set(gate_bias)
    gp = gp.at[:, 0, LANE_DECAY:LANE_DECAY + H_C].set(dt_bias)
    gp = gp.at[:, 1, LANE_DECAY:LANE_DECAY + H_C].set(a_log)
    return gp


def kernel(x_prompt, x_sample, state_mlstm_C, state_mlstm_n, state_mlstm_m, state_ret_S, state_gdn_S,
           state_gdn_conv, norm_mix_pre, norm_mix_post, norm_mlp_pre, norm_mlp_post, w_in, mlstm_gate_bias,
           gdn_conv_w, gdn_A_log, gdn_dt_bias, norm_mlstm, norm_ret, norm_gdn, w_out, w_up, w_down):
    bp, tp, d = x_prompt.shape
    bs, ts, _ = x_sample.shape
    depth = w_in.shape[0]
    rows_p = bp * tp
    rows_s = bs * ts
    rows = rows_p + rows_s

    x = jnp.concatenate([x_prompt.reshape(rows_p, d), x_sample.reshape(rows_s, d)], axis=0)
    w_in_p, w_gate_p = _repack_w_in(w_in)
    w_out_b = w_out.astype(bf16)
    w_down_b = w_down.astype(bf16)
    gp = _gate_params(mlstm_gate_bias, gdn_dt_bias, gdn_A_log)
    na = norm_mlstm.reshape(depth, H_A, HEAD_DIM)
    nb = norm_ret.reshape(depth, H_B, HEAD_DIM)
    nc = norm_gdn.reshape(depth, 1, HEAD_DIM)

    nrep = jnp.repeat(state_mlstm_n.reshape(depth, bs, D_A), ts, axis=1)
    mrep = jnp.pad(jnp.repeat(state_mlstm_m, ts, axis=1), ((0, 0), (0, 0), (0, 128 - H_A)))
    s_states = (state_mlstm_C, nrep, mrep, state_ret_S, state_gdn_S, state_gdn_conv)

    p_tables = _prompt_tables(tp, PROMPT_CHUNK)
    s_tables = _sample_tables(ts, SAMPLE_BLOCK)

    tm = rows // 8
    h = _rmsnorm_rows(x, norm_mix_pre[0], tm // 2)
    p_states = []
    s_out = None
    for l in range(depth):
        proj = _matmul_in(h, w_in_p, l, tm, 2048, "proj_in")
        gates = _matmul_in(h, w_gate_p, l, tm, 128, "proj_gates")
        outs = _prompt_mixers(proj, gates, p_tables, gp[l], gdn_conv_w[l], na[l], nb[l], nc[l], bp, tp, rows,
                              PROMPT_CHUNK)
        p_states.append(outs[1:])
        res = _sample_mixers(l, proj, gates, outs[0], s_tables, gp[l], gdn_conv_w[l], na[l], nb[l], nc[l],
                             s_states, s_out, bs, ts, rows_p, SAMPLE_BLOCK)
        y, s_out = res[0], res[1:]
        x, h = _matmul_out(y, w_out_b, l, x, norm_mix_post[l], norm_mlp_pre[l], tm // 2)
        u = _matmul_up(h, w_up, l, tm, 1024)
        x, h = _matmul_down(u, w_down_b, l, x, norm_mlp_post[l], norm_mix_pre[(l + 1) % depth], tm, 1024)

    stk = lambda j: jnp.stack([s[j] for s in p_states], axis=0)
    sc, sn, sm, sr, sg, sconv = s_out
    return (
        x[:rows_p].reshape(bp, tp, d), x[rows_p:].reshape(bs, ts, d),
        stk(0), stk(1), stk(2)[:, :, 0, :H_A], stk(3), stk(4), stk(5),
        sc, sn[:, ts - 1::ts, :].reshape(depth, bs, H_A, HEAD_DIM), sm[:, ts - 1::ts, :H_A], sr, sg,
        sconv,
    )
```

```python
import functools
import math

import numpy as np
import jax
import jax.numpy as jnp
from jax import lax
from jax.experimental import pallas as pl
from jax.experimental.pallas import tpu as pltpu

f32 = jnp.float32
bf16 = jnp.bfloat16

D_MODEL = 2048
HEAD_DIM = 128
H_A, H_B, H_C = 4, 4, 8
D_A, D_B, D_C = H_A * HEAD_DIM, H_B * HEAD_DIM, H_C * HEAD_DIM
D_FF = 4 * D_MODEL
CONV_W = 4
PAST_LEN = 16384
ROPE_BASE = 10000.0
GATE_SOFTCAP = 15.0
NORM_EPS = 1e-6
QK_SCALE = HEAD_DIM ** -0.5

COL_A = 0
COL_B = 4 * D_A
COL_C = COL_B + 4 * D_B
N_PROJ = COL_C + 4 * D_C
LANE_I, LANE_F, LANE_DECAY, LANE_BETA = 0, H_A, 2 * H_A, 2 * H_A + H_C

PROMPT_CHUNK = 64
INV_BLOCK = 8
SAMPLE_BLOCK = 8
NEG_BIG = -1e30
VMEM_LIMIT = 60 * 1024 * 1024


def _cparams(sem):
    return pltpu.CompilerParams(dimension_semantics=sem, vmem_limit_bytes=VMEM_LIMIT)


def _rms(x, g):
    return x * lax.rsqrt(jnp.mean(x * x, axis=-1, keepdims=True) + NORM_EPS) * g


def _rmsnorm_kernel(x_ref, g_ref, o_ref):
    o_ref[...] = _rms(x_ref[...], g_ref[...]).astype(o_ref.dtype)


def _rmsnorm_rows(x, g, tm):
    m, d = x.shape
    return pl.pallas_call(
        _rmsnorm_kernel,
        grid=(m // tm,),
        in_specs=[pl.BlockSpec((tm, d), lambda i: (i, 0)), pl.BlockSpec((1, d), lambda i: (0, 0))],
        out_specs=pl.BlockSpec((tm, d), lambda i: (i, 0)),
        out_shape=jax.ShapeDtypeStruct((m, d), bf16),
        compiler_params=_cparams(("parallel",)),
        name="rmsnorm_in",
    )(x, g.reshape(1, d))


def _mm_kernel(a_ref, w_ref, o_ref):
    o_ref[...] = jnp.dot(a_ref[...], w_ref[...], preferred_element_type=f32)


def _matmul_gates(h, w, tm):
    m, k = h.shape
    n = w.shape[1]
    return pl.pallas_call(
        _mm_kernel,
        grid=(m // tm,),
        in_specs=[pl.BlockSpec((tm, k), lambda i: (i, 0)), pl.BlockSpec((k, n), lambda i: (0, 0))],
        out_specs=pl.BlockSpec((tm, n), lambda i: (i, 0)),
        out_shape=jax.ShapeDtypeStruct((m, n), f32),
        compiler_params=_cparams(("parallel",)),
        name="proj_gates",
    )(h, w)


def _mm_in_kernel(a_ref, w_ref, wo_ref, o_ref, wob_ref):
    o_ref[...] = jnp.dot(a_ref[...], w_ref[...], preferred_element_type=f32)
    wob_ref[...] = wo_ref[...].astype(bf16)


def _matmul_in(h, w, w_out, layer, tm, tn):
    m, k = h.shape
    n = w.shape[1]
    nj, ni = n // tn, m // tm
    d = w_out.shape[1]
    slab = d // (nj * ni)
    step = lambda j, i: j * ni + i
    return pl.pallas_call(
        _mm_in_kernel,
        grid=(nj, ni),
        in_specs=[pl.BlockSpec((tm, k), lambda j, i: (i, 0)), pl.BlockSpec((k, tn), lambda j, i: (0, j)),
                  pl.BlockSpec((None, slab, d), lambda j, i: (layer, step(j, i), 0))],
        out_specs=[pl.BlockSpec((tm, tn), lambda j, i: (i, j)),
                   pl.BlockSpec((slab, d), lambda j, i: (step(j, i), 0))],
        out_shape=[jax.ShapeDtypeStruct((m, n), f32), jax.ShapeDtypeStruct((d, d), bf16)],
        compiler_params=_cparams(("parallel", "arbitrary")),
        name="proj_in",
    )(h, w, w_out)


def _repack_rows(win_ref, main_ref, gate_ref):
    n_a = 4 * D_A
    n_ag = n_a + 2 * H_A
    n_bc = 4 * D_B + 4 * D_C
    main_ref[:, 0:n_a] = win_ref[:, 0:n_a].astype(bf16)
    main_ref[:, n_a:n_a + n_bc] = win_ref[:, n_ag:n_ag + n_bc].astype(bf16)
    gate_ref[...] = jnp.zeros_like(gate_ref)
    gate_ref[:, LANE_I:LANE_I + 2 * H_A] = win_ref[:, n_a:n_ag].astype(bf16)
    gate_ref[:, LANE_DECAY:LANE_DECAY + 2 * H_C] = win_ref[:, n_ag + n_bc:n_ag + n_bc + 2 * H_C].astype(bf16)


def _mm_up_kernel(with_repack, a_ref, w_ref, wd_ref, *rest):
    if with_repack:
        win_ref, o_ref, wdb_ref, main_ref, gate_ref, wb_ref = rest
        _repack_rows(win_ref, main_ref, gate_ref)
    else:
        o_ref, wdb_ref, wb_ref = rest

    @pl.when(pl.program_id(1) == 0)
    def _():
        wb_ref[...] = w_ref[...].astype(bf16)

    z = jnp.maximum(jnp.dot(a_ref[...], wb_ref[...], preferred_element_type=f32), 0.0)
    o_ref[...] = (z * z).astype(o_ref.dtype)
    wdb_ref[...] = wd_ref[...].astype(bf16)


def _matmul_up(h, w, w_down, w_in, layer, tm, tn):
    m, k = h.shape
    n = w.shape[2]
    nj, ni = n // tn, m // tm
    steps = nj * ni
    step = lambda j, i: j * ni + i
    with_repack = layer + 1 < w_in.shape[0]
    dslab = w_down.shape[1] // steps
    d = w_down.shape[2]
    in_specs = [pl.BlockSpec((tm, k), lambda j, i: (i, 0)), pl.BlockSpec((None, k, tn), lambda j, i: (layer, 0, j)),
                pl.BlockSpec((None, dslab, d), lambda j, i: (layer, step(j, i), 0))]
    out_specs = [pl.BlockSpec((tm, tn), lambda j, i: (i, j)), pl.BlockSpec((dslab, d), lambda j, i: (step(j, i), 0))]
    out_shape = [jax.ShapeDtypeStruct((m, n), bf16), jax.ShapeDtypeStruct(w_down.shape[1:], bf16)]
    args = [h, w, w_down]
    if with_repack:
        islab = w_in.shape[1] // steps
        in_specs.append(pl.BlockSpec((None, islab, w_in.shape[2]), lambda j, i: (layer + 1, step(j, i), 0)))
        out_specs += [pl.BlockSpec((islab, N_PROJ), lambda j, i: (step(j, i), 0)),
                      pl.BlockSpec((islab, 128), lambda j, i: (step(j, i), 0))]
        out_shape += [jax.ShapeDtypeStruct((w_in.shape[1], N_PROJ), bf16),
                      jax.ShapeDtypeStruct((w_in.shape[1], 128), bf16)]
        args.append(w_in)
    return pl.pallas_call(
        functools.partial(_mm_up_kernel, with_repack),
        grid=(nj, ni),
        in_specs=in_specs,
        out_specs=out_specs,
        out_shape=out_shape,
        scratch_shapes=[pltpu.VMEM((k, tn), bf16)],
        compiler_params=_cparams(("parallel", "arbitrary")),
        name="mlp_up",
    )(*args)


def _residual_epilogue(z, x_ref, gpost_ref, gnext_ref, xo_ref, ho_ref):
    x_new = x_ref[...] + _rms(z, gpost_ref[...])
    xo_ref[...] = x_new
    ho_ref[...] = _rms(x_new, gnext_ref[...]).astype(ho_ref.dtype)


def _mm_out_kernel(a_ref, w_ref, x_ref, gpost_ref, gnext_ref, xo_ref, ho_ref):
    z = jnp.dot(a_ref[...], w_ref[...], preferred_element_type=f32)
    _residual_epilogue(z, x_ref, gpost_ref, gnext_ref, xo_ref, ho_ref)


def _matmul_out(y, w, x, g_post, g_next, tm):
    m, k = y.shape
    d = w.shape[1]
    row = lambda i: (i, 0)
    fixed = lambda i: (0, 0)
    return pl.pallas_call(
        _mm_out_kernel,
        grid=(m // tm,),
        in_specs=[pl.BlockSpec((tm, k), row), pl.BlockSpec((k, d), fixed),
                  pl.BlockSpec((tm, d), row),
                  pl.BlockSpec((1, d), fixed), pl.BlockSpec((1, d), fixed)],
        out_specs=[pl.BlockSpec((tm, d), row), pl.BlockSpec((tm, d), row)],
        out_shape=[jax.ShapeDtypeStruct((m, d), f32), jax.ShapeDtypeStruct((m, d), bf16)],
        compiler_params=_cparams(("parallel",)),
        name="proj_out",
    )(y, w, x, g_post.reshape(1, d), g_next.reshape(1, d))


def _mm_down_kernel(a_ref, w_ref, x_ref, gpost_ref, gnext_ref, xo_ref, ho_ref):
    kk = pl.program_id(1)

    @pl.when(kk == 0)
    def _():
        xo_ref[...] = jnp.zeros_like(xo_ref)

    xo_ref[...] += jnp.dot(a_ref[...], w_ref[...], preferred_element_type=f32)

    @pl.when(kk == pl.num_programs(1) - 1)
    def _():
        _residual_epilogue(xo_ref[...], x_ref, gpost_ref, gnext_ref, xo_ref, ho_ref)


def _matmul_down(u, w, x, g_post, g_next, tm, tk):
    m, k = u.shape
    d = w.shape[1]
    row = lambda i, kk: (i, 0)
    fixed = lambda i, kk: (0, 0)
    return pl.pallas_call(
        _mm_down_kernel,
        grid=(m // tm, k // tk),
        in_specs=[pl.BlockSpec((tm, tk), lambda i, kk: (i, kk)),
                  pl.BlockSpec((tk, d), lambda i, kk: (kk, 0)),
                  pl.BlockSpec((tm, d), row), pl.BlockSpec((1, d), fixed), pl.BlockSpec((1, d), fixed)],
        out_specs=[pl.BlockSpec((tm, d), row), pl.BlockSpec((tm, d), row)],
        out_shape=[jax.ShapeDtypeStruct((m, d), f32), jax.ShapeDtypeStruct((m, d), bf16)],
        compiler_params=_cparams(("parallel", "arbitrary")),
        name="mlp_down",
    )(u, w, x, g_post.reshape(1, d), g_next.reshape(1, d))


def _dot(a, b):
    return jnp.dot(a, b, preferred_element_type=f32)


def _dot_tb(a, b):
    return lax.dot_general(a, b, (((1,), (1,)), ((), ())), preferred_element_type=f32)


def _dot_ta(a, b):
    return lax.dot_general(a, b, (((0,), (0,)), ((), ())), preferred_element_type=f32)


def _lane_sum(x):
    return jnp.sum(x, axis=-1, keepdims=True)


def _softplus(z):
    return jnp.maximum(z, 0.0) + jnp.log1p(jnp.exp(-jnp.abs(z)))


def _gate_transform(gt, gp):
    z = gt + gp[0:1, :]
    capped = GATE_SOFTCAP * jnp.tanh(z / GATE_SOFTCAP)
    log_f = -_softplus(-capped)
    decay = -jnp.exp(gp[1:2, :]) * _softplus(z)
    beta = jax.nn.sigmoid(gt)
    return capped, log_f, decay, beta


def _head_norm(x, gain, center):
    if center:
        x = x - jnp.mean(x, axis=-1, keepdims=True)
    return x * lax.rsqrt(jnp.mean(x * x, axis=-1, keepdims=True) + NORM_EPS) * gain


def _l2norm(x):
    return x * lax.rsqrt(_lane_sum(x * x) + NORM_EPS)


def _silu(x):
    return x * jax.nn.sigmoid(x)


def _rotary(x, cos, sin_signed):
    return x * cos + pltpu.roll(x, HEAD_DIM // 2, 1) * sin_signed


def _retention_log_gamma():
    return np.log1p(-np.exp2(-5.0 - np.arange(H_B, dtype=np.float64)))


def _rope_tables(pos):
    half = HEAD_DIM // 2
    inv = ROPE_BASE ** (-np.arange(half, dtype=np.float64) / half)
    ang = np.asarray(pos, dtype=np.float64)[:, None] * inv[None, :]
    cos = np.concatenate([np.cos(ang), np.cos(ang)], axis=-1)
    sin = np.concatenate([-np.sin(ang), np.sin(ang)], axis=-1)
    return cos.astype(np.float32), sin.astype(np.float32)


def _scan_rows(x, row, length, op, fill):
    s = 1
    while s < length:
        x = op(x, jnp.where(row >= s, pltpu.roll(x, s, 0), fill))
        s *= 2
    return x


def _rows_to_lanes(x, length):
    if length < 128:
        x = jnp.concatenate([x, jnp.zeros((128 - length, 128), x.dtype)], axis=0)
    return x.T[:, :length]


def _prompt_mixer_kernel(L, proj_ref, gt_ref, cos_ref, sin_ref, rd_ref, rq_ref, rk_ref, gp_ref, cw_ref,
                         na_ref, nb_ref, nc_ref,
                         y_ref, c_ref, n_ref, m_ref, r_ref, g_ref, conv_ref, ext_ref):
    c = pl.program_id(1)

    @pl.when(c == 0)
    def _():
        c_ref[...] = jnp.zeros_like(c_ref)
        n_ref[...] = jnp.zeros_like(n_ref)
        m_ref[...] = jnp.zeros_like(m_ref)
        r_ref[...] = jnp.zeros_like(r_ref)
        g_ref[...] = jnp.zeros_like(g_ref)
        ext_ref[0:8, :] = jnp.zeros((8, 3 * D_C), f32)

    row = lax.broadcasted_iota(jnp.int32, (L, 128), 0)
    lane = lax.broadcasted_iota(jnp.int32, (L, 128), 1)
    ri = lax.broadcasted_iota(jnp.int32, (L, L), 0)
    ci = lax.broadcasted_iota(jnp.int32, (L, L), 1)
    incl = ri >= ci
    strict = ri > ci

    capped, log_f, decay, beta = _gate_transform(gt_ref[...], gp_ref[...])

    head_lane = lane < H_A
    ig = jnp.where(head_lane, capped, 0.0)
    lf = jnp.where(head_lane, pltpu.roll(log_f, 128 - LANE_F, 1), 0.0)
    mp = m_ref[0]
    F = _scan_rows(lf, row, L, jnp.add, 0.0)
    m = F + jnp.maximum(mp, _scan_rows(ig - F, row, L, jnp.maximum, NEG_BIG))
    m_new = m[L - 1:L, :]
    f_last = F[L - 1:L, :]
    a_rows = F - m
    b_lanes = _rows_to_lanes(ig - F, L)
    inter = jnp.exp(F + mp - m)
    inv_floor = jnp.exp(-m)
    wl = jnp.exp(ig + f_last - F - m_new)
    dec = jnp.exp(f_last + mp - m_new)
    m_ref[0] = m_new

    def mlstm_head(h):
        q = proj_ref[:, COL_A + h * HEAD_DIM:COL_A + (h + 1) * HEAD_DIM]
        k = proj_ref[:, COL_A + D_A + h * HEAD_DIM:COL_A + D_A + (h + 1) * HEAD_DIM] * QK_SCALE
        v = proj_ref[:, COL_A + 2 * D_A + h * HEAD_DIM:COL_A + 2 * D_A + (h + 1) * HEAD_DIM]
        og = proj_ref[:, COL_A + 3 * D_A + h * HEAD_DIM:COL_A + 3 * D_A + (h + 1) * HEAD_DIM]
        logw = a_rows[:, h:h + 1] + b_lanes[h:h + 1, :]
        s = _dot_tb(q, k) * jnp.exp(jnp.where(incl, logw, NEG_BIG))
        c_old = c_ref[0, h]
        n_old = n_ref[0, h:h + 1, :]
        inter_h = inter[:, h:h + 1]
        num = inter_h * _dot(q, c_old) + _dot(s, v)
        den = inter_h * _lane_sum(q * n_old) + _lane_sum(s)
        hh = num / jnp.maximum(jnp.abs(den), inv_floor[:, h:h + 1])
        kw = k * wl[:, h:h + 1]
        dec_h = dec[:, h:h + 1]
        c_ref[0, h] = dec_h * c_old + _dot_ta(kw, v)
        n_ref[0, h:h + 1, :] = dec_h * n_old + jnp.sum(kw, axis=0, keepdims=True)
        ya = jax.nn.sigmoid(og) * _head_norm(hh, na_ref[h:h + 1, :], False)
        y_ref[:, h * HEAD_DIM:(h + 1) * HEAD_DIM] = ya.astype(y_ref.dtype)

    lg = _retention_log_gamma()

    def retention_head(h):
        cos = cos_ref[...]
        sin = sin_ref[...]
        q = proj_ref[:, COL_B + h * HEAD_DIM:COL_B + (h + 1) * HEAD_DIM]
        k = proj_ref[:, COL_B + D_B + h * HEAD_DIM:COL_B + D_B + (h + 1) * HEAD_DIM]
        v = proj_ref[:, COL_B + 2 * D_B + h * HEAD_DIM:COL_B + 2 * D_B + (h + 1) * HEAD_DIM]
        gb = proj_ref[:, COL_B + 3 * D_B + h * HEAD_DIM:COL_B + 3 * D_B + (h + 1) * HEAD_DIM]
        qr = _rotary(q, cos, sin)
        kr = _rotary(k, cos, sin) * QK_SCALE
        s_old = r_ref[0, h]
        inner = _dot_tb(qr, kr) * rd_ref[h]
        o = _dot(inner, v) + rq_ref[h] * _dot(qr, s_old)
        r_ref[0, h] = float(np.exp(L * lg[h])) * s_old + _dot_ta(kr * rk_ref[h], v)
        yb = _silu(gb) * _head_norm(o, nb_ref[h:h + 1, :], True)
        y_ref[:, D_A + h * HEAD_DIM:D_A + (h + 1) * HEAD_DIM] = yb.astype(y_ref.dtype)

    fillers = [functools.partial(mlstm_head, h) for h in range(H_A)]
    fillers += [functools.partial(retention_head, h) for h in range(H_B)]

    def emit_fillers(count):
        for _ in range(min(count, len(fillers))):
            fillers.pop(0)()

    ext_ref[8:8 + L, :] = proj_ref[:, COL_C:COL_C + 3 * D_C]
    G = _scan_rows(decay, row, L, jnp.add, 0.0)
    g_lanes = _rows_to_lanes(G, L)
    gam = jnp.exp(G)
    g_last = G[L - 1:L, :]
    k_decay = jnp.exp(g_last - G)
    s_decay = jnp.exp(g_last)

    def conv_block(col):
        acc = cw_ref[CONV_W - 1:CONV_W, col:col + HEAD_DIM] * ext_ref[8:8 + L, col:col + HEAD_DIM]
        for w in range(CONV_W - 1):
            off = 8 - (CONV_W - 1) + w
            acc = acc + cw_ref[w:w + 1, col:col + HEAD_DIM] * ext_ref[off:off + L, col:col + HEAD_DIM]
        return _silu(acc)

    heads = range(H_C)
    dcol = [LANE_DECAY + h for h in heads]
    beta_c = [beta[:, LANE_BETA + h:LANE_BETA + h + 1] for h in heads]
    q = [_l2norm(conv_block(h * HEAD_DIM)) * QK_SCALE for h in heads]
    k = [_l2norm(conv_block(D_C + h * HEAD_DIM)) for h in heads]
    v = [conv_block(2 * D_C + h * HEAD_DIM) for h in heads]
    dec_in = [jnp.exp(jnp.where(incl, G[:, dcol[h]:dcol[h] + 1] - g_lanes[dcol[h]:dcol[h] + 1, :], NEG_BIG))
              for h in heads]
    a_mat = [jnp.where(strict, dec_in[h], 0.0) * beta_c[h] * _dot_tb(k[h], k[h]) for h in heads]
    qk = [_dot_tb(q[h], k[h]) * dec_in[h] for h in heads]
    blk = min(INV_BLOCK, L)
    same = (ri // blk) == (ci // blk)
    pw = [jnp.where(same, -a, 0.0) for a in a_mat]
    e_mat = list(pw)
    span = 1
    while 2 * span < blk:
        pw = [_dot(p, p) for p in pw]
        e_mat = [e_mat[h] + pw[h] + _dot(e_mat[h], pw[h]) for h in heads]
        emit_fillers(1)
        span *= 2
    while blk < L:
        wider = (ri // (2 * blk)) == (ci // (2 * blk))
        a_off = [jnp.where(wider & jnp.logical_not(same), a, 0.0) for a in a_mat]
        low = [a_off[h] + _dot(e_mat[h], a_off[h]) for h in heads]
        e_mat = [e_mat[h] - low[h] - _dot(low[h], e_mat[h]) for h in heads]
        emit_fillers(2)
        same = wider
        blk *= 2
    rhs = [jnp.concatenate([(beta_c[h] * gam[:, dcol[h]:dcol[h] + 1]) * k[h], beta_c[h] * v[h]], axis=1)
           for h in heads]
    sol = [rhs[h] + _dot(e_mat[h], rhs[h]) for h in heads]
    emit_fillers(len(fillers))
    s_old = [g_ref[0, h] for h in heads]
    both = [_dot(jnp.concatenate([sol[h][:, :HEAD_DIM], q[h]], axis=0), s_old[h]) for h in heads]
    u = [sol[h][:, HEAD_DIM:] - both[h][:L] for h in heads]
    o = [gam[:, dcol[h]:dcol[h] + 1] * both[h][L:] + _dot(qk[h], u[h]) for h in heads]
    for h in heads:
        g_ref[0, h] = s_decay[:, dcol[h]:dcol[h] + 1] * s_old[h] + _dot_ta(k[h] * k_decay[:, dcol[h]:dcol[h] + 1], u[h])
    for h in heads:
        gz = proj_ref[:, COL_C + 3 * D_C + h * HEAD_DIM:COL_C + 3 * D_C + (h + 1) * HEAD_DIM]
        yc = _silu(gz) * _head_norm(o[h], nc_ref[...], False)
        y_ref[:, D_A + D_B + h * HEAD_DIM:D_A + D_B + (h + 1) * HEAD_DIM] = yc.astype(y_ref.dtype)

    ext_ref[0:8, :] = ext_ref[L:L + 8, :]

    @pl.when(c == pl.num_programs(1) - 1)
    def _():
        conv_ref[0] = ext_ref[8 + L - (CONV_W - 1):8 + L, :]


def _prompt_tables(seq, L):
    lg = _retention_log_gamma()
    i = np.arange(L, dtype=np.float64)
    diff = i[:, None] - i[None, :]
    rd = np.where(diff >= 0, np.exp(np.maximum(diff, 0.0) * lg[:, None, None]), 0.0)
    rq = np.broadcast_to(np.exp((i + 1.0) * lg[:, None])[..., None], (H_B, L, HEAD_DIM))
    rk = np.broadcast_to(np.exp((L - 1.0 - i) * lg[:, None])[..., None], (H_B, L, HEAD_DIM))
    cos, sin = _rope_tables(np.arange(seq))
    return (jnp.asarray(cos), jnp.asarray(sin), jnp.asarray(rd, f32), jnp.asarray(rq, f32), jnp.asarray(rk, f32))


def _prompt_mixers(proj, gates, tables, gp, cw, na, nb, nc, batch, seq, total_rows, L):
    nchunk = seq // L
    cos, sin, rd, rq, rk = tables
    full = lambda shape: pl.BlockSpec(shape, lambda b, c: (0,) * len(shape))
    state4 = lambda heads: pl.BlockSpec((1, heads, HEAD_DIM, HEAD_DIM), lambda b, c: (b, 0, 0, 0))
    return pl.pallas_call(
        functools.partial(_prompt_mixer_kernel, L),
        grid=(batch, nchunk),
        in_specs=[
            pl.BlockSpec((L, N_PROJ), lambda b, c: (b * nchunk + c, 0)),
            pl.BlockSpec((L, 128), lambda b, c: (b * nchunk + c, 0)),
            pl.BlockSpec((L, HEAD_DIM), lambda b, c: (c, 0)),
            pl.BlockSpec((L, HEAD_DIM), lambda b, c: (c, 0)),
            full((H_B, L, L)), full((H_B, L, HEAD_DIM)), full((H_B, L, HEAD_DIM)),
            full((8, 128)), full((CONV_W, 3 * D_C)),
            full((H_A, HEAD_DIM)), full((H_B, HEAD_DIM)), full((1, HEAD_DIM)),
        ],
        out_specs=[
            pl.BlockSpec((L, D_MODEL), lambda b, c: (b * nchunk + c, 0)),
            state4(H_A),
            pl.BlockSpec((1, H_A, HEAD_DIM), lambda b, c: (b, 0, 0)),
            pl.BlockSpec((1, 1, 128), lambda b, c: (b, 0, 0)),
            state4(H_B),
            state4(H_C),
            pl.BlockSpec((1, CONV_W - 1, 3 * D_C), lambda b, c: (b, 0, 0)),
        ],
        out_shape=[
            jax.ShapeDtypeStruct((total_rows, D_MODEL), bf16),
            jax.ShapeDtypeStruct((batch, H_A, HEAD_DIM, HEAD_DIM), f32),
            jax.ShapeDtypeStruct((batch, H_A, HEAD_DIM), f32),
            jax.ShapeDtypeStruct((batch, 1, 128), f32),
            jax.ShapeDtypeStruct((batch, H_B, HEAD_DIM, HEAD_DIM), f32),
            jax.ShapeDtypeStruct((batch, H_C, HEAD_DIM, HEAD_DIM), f32),
            jax.ShapeDtypeStruct((batch, CONV_W - 1, 3 * D_C), f32),
        ],
        scratch_shapes=[pltpu.VMEM((L + 8, 3 * D_C), f32)],
        compiler_params=_cparams(("parallel", "arbitrary")),
        name="prompt_mixers",
    )(proj, gates, cos, sin, rd, rq, rk, gp, cw, na, nb, nc)


def _sample_mixer_kernel(T, BB, layer_first, *refs):
    (proj_ref, gt_ref, cos_ref, sin_ref, rt_ref, gp_ref, cw_ref, na_ref, nb_ref, nc_ref,
     c0_ref, nrep_ref, mrep_ref, r0_ref, g0_ref, conv0_ref) = refs[:16]
    rest = refs[16 + (1 if layer_first else 7):]
    (y_ref, c_ref, nout_ref, mout_ref, r_ref, g_ref, conv_ref, cv_ref, sa_ref, ext_ref) = rest
    R = T * BB
    row = lax.broadcasted_iota(jnp.int32, (R, 128), 0)
    lane = lax.broadcasted_iota(jnp.int32, (R, 128), 1)
    t = lax.rem(row, T)
    tcol = t[:, 0:1]

    def shift(x, s):
        return x if s == 0 else pltpu.roll(x, s, 0)

    def seg_scan(x, op, fill):
        s = 1
        while s < T:
            x = op(x, jnp.where(t >= s, shift(x, s), fill))
            s *= 2
        return x

    def last_rep(x):
        x_last = jnp.where(t == T - 1, x, 0.0)
        out = x_last
        for s in range(1, T):
            out = out + pltpu.roll(x_last, R - s, 0)
        return out

    capped, log_f, decay, beta = _gate_transform(gt_ref[...], gp_ref[...])

    head_lane = lane < H_A
    ig = jnp.where(head_lane, capped, 0.0)
    lf = jnp.where(head_lane, pltpu.roll(log_f, 128 - LANE_F, 1), 0.0)
    mp = mrep_ref[...]
    F = seg_scan(lf, jnp.add, 0.0)
    m = F + jnp.maximum(mp, seg_scan(ig - F, jnp.maximum, NEG_BIG))
    m_new = last_rep(m)
    f_last = last_rep(F)
    inter = jnp.exp(F + mp - m)
    inv_floor = jnp.exp(-m)
    wl = jnp.exp(ig + f_last - F - m_new)
    dec = jnp.exp(f_last + mp - m_new)
    mout_ref[...] = m_new
    pw = [jnp.where(t >= s, jnp.exp(F - shift(F, s) + shift(ig, s) - m), 0.0) for s in range(T)]

    def seq(x, b):
        return x[b * T:(b + 1) * T]

    slot_a, slot_b, slot_w, slot_q = 0, H_A, H_A + H_B, H_A + H_B + H_C

    qa = [proj_ref[:, COL_A + h * HEAD_DIM:COL_A + (h + 1) * HEAD_DIM] for h in range(H_A)]
    for h in range(H_A):
        for b in range(BB):
            sa_ref[slot_a + h, b * T:(b + 1) * T, :] = _dot(seq(qa[h], b), c0_ref[b, h])
    cos = cos_ref[...]
    sin = sin_ref[...]
    lg = _retention_log_gamma()
    qr = [_rotary(proj_ref[:, COL_B + h * HEAD_DIM:COL_B + (h + 1) * HEAD_DIM], cos, sin) for h in range(H_B)]
    kr = [_rotary(proj_ref[:, COL_B + D_B + h * HEAD_DIM:COL_B + D_B + (h + 1) * HEAD_DIM], cos, sin) * QK_SCALE
          for h in range(H_B)]
    for h in range(H_B):
        for b in range(BB):
            sa_ref[slot_b + h, b * T:(b + 1) * T, :] = _dot(seq(qr[h], b), r0_ref[b, h])

    ka = [proj_ref[:, COL_A + D_A + h * HEAD_DIM:COL_A + D_A + (h + 1) * HEAD_DIM] * QK_SCALE for h in range(H_A)]
    va = [proj_ref[:, COL_A + 2 * D_A + h * HEAD_DIM:COL_A + 2 * D_A + (h + 1) * HEAD_DIM] for h in range(H_A)]
    n_old = [nrep_ref[:, h * HEAD_DIM:(h + 1) * HEAD_DIM] for h in range(H_A)]
    kw = [ka[h] * wl[:, h:h + 1] for h in range(H_A)]
    for h in range(H_A):
        for b in range(BB):
            dec_bh = dec[b * T:b * T + 1, h:h + 1]
            c_ref[b, h] = dec_bh * c0_ref[b, h] + _dot_ta(seq(kw[h], b), seq(va[h], b))
        nout_ref[:, h * HEAD_DIM:(h + 1) * HEAD_DIM] = dec[:, h:h + 1] * n_old[h] + seg_scan(kw[h], jnp.add, 0.0)
    vb = [proj_ref[:, COL_B + 2 * D_B + h * HEAD_DIM:COL_B + 2 * D_B + (h + 1) * HEAD_DIM] for h in range(H_B)]
    for h in range(H_B):
        kd = kr[h] * rt_ref[:, H_B + h:H_B + h + 1]
        for b in range(BB):
            r_ref[b, h] = float(np.exp(T * lg[h])) * r0_ref[b, h] + _dot_ta(seq(kd, b), seq(vb[h], b))

    for b in range(BB):
        ext_ref[b, 0:CONV_W - 1, :] = conv0_ref[b]
        ext_ref[b, CONV_W - 1:CONV_W - 1 + T, :] = proj_ref[b * T:(b + 1) * T, COL_C:COL_C + 3 * D_C]
        acc = cw_ref[0:1, :] * ext_ref[b, 0:T, :]
        for w in range(1, CONV_W):
            acc = acc + cw_ref[w:w + 1, :] * ext_ref[b, w:w + T, :]
        cv_ref[b * T:(b + 1) * T, :] = _silu(acc)
        conv_ref[b] = ext_ref[b, T:T + CONV_W - 1, :]

    G = seg_scan(decay, jnp.add, 0.0)
    gam = jnp.exp(G)
    g_last = last_rep(G)
    k_decay = jnp.exp(g_last - G)
    s_decay = jnp.exp(g_last)
    dshift = [None] + [jnp.where(t >= s, jnp.exp(G - shift(G, s)), 0.0) for s in range(1, T)]

    heads = range(H_C)
    dcol = [LANE_DECAY + h for h in heads]
    beta_c = [beta[:, LANE_BETA + h:LANE_BETA + h + 1] for h in heads]
    qg = [_l2norm(cv_ref[:, h * HEAD_DIM:(h + 1) * HEAD_DIM]) * QK_SCALE for h in heads]
    kg = [_l2norm(cv_ref[:, D_C + h * HEAD_DIM:D_C + (h + 1) * HEAD_DIM]) for h in heads]
    vg = [cv_ref[:, 2 * D_C + h * HEAD_DIM:2 * D_C + (h + 1) * HEAD_DIM] for h in heads]
    a_sub = [[None] + [beta_c[h] * dshift[s][:, dcol[h]:dcol[h] + 1] * _lane_sum(kg[h] * shift(kg[h], s))
                       for s in range(1, T)] for h in heads]
    rhs_w = [(beta_c[h] * gam[:, dcol[h]:dcol[h] + 1]) * kg[h] for h in heads]
    rhs_u = [beta_c[h] * vg[h] for h in heads]
    w_sol, u_sol = list(rhs_w), list(rhs_u)
    for i in range(1, T):
        for h in heads:
            upd_w = a_sub[h][1] * shift(w_sol[h], 1)
            upd_u = a_sub[h][1] * shift(u_sol[h], 1)
            for s in range(2, i + 1):
                upd_w = upd_w + a_sub[h][s] * shift(w_sol[h], s)
                upd_u = upd_u + a_sub[h][s] * shift(u_sol[h], s)
            w_sol[h] = jnp.where(t == i, rhs_w[h] - upd_w, w_sol[h])
            u_sol[h] = jnp.where(t == i, rhs_u[h] - upd_u, u_sol[h])
    for h in heads:
        for b in range(BB):
            both = _dot(jnp.concatenate([seq(w_sol[h], b), seq(qg[h], b)], axis=0), g0_ref[b, h])
            sa_ref[slot_w + h, b * T:(b + 1) * T, :] = both[:T]
            sa_ref[slot_q + h, b * T:(b + 1) * T, :] = both[T:]

    ug = [u_sol[h] - sa_ref[slot_w + h] for h in heads]
    for h in heads:
        kd = kg[h] * k_decay[:, dcol[h]:dcol[h] + 1]
        for b in range(BB):
            g_ref[b, h] = (s_decay[b * T:b * T + 1, dcol[h]:dcol[h] + 1] * g0_ref[b, h]
                           + _dot_ta(seq(kd, b), seq(ug[h], b)))

    sc_a = [[_lane_sum(qa[h] * shift(ka[h], s)) * pw[s][:, h:h + 1] for s in range(T)] for h in range(H_A)]
    qn = [_lane_sum(qa[h] * n_old[h]) for h in range(H_A)]
    sc_b = [[jnp.where(tcol >= s, _lane_sum(qr[h] * shift(kr[h], s)) * float(np.exp(s * lg[h])), 0.0)
             for s in range(T)] for h in range(H_B)]
    sc_c = [[_lane_sum(qg[h] * shift(kg[h], s)) * (1.0 if s == 0 else dshift[s][:, dcol[h]:dcol[h] + 1])
             for s in range(T)] for h in heads]
    hid_a, hid_b, hid_c = [], [], []
    for h in range(H_A):
        inter_h = inter[:, h:h + 1]
        num = inter_h * sa_ref[slot_a + h]
        den = inter_h * qn[h]
        for s in range(T):
            num = num + sc_a[h][s] * shift(va[h], s)
            den = den + sc_a[h][s]
        hid_a.append(num / jnp.maximum(jnp.abs(den), inv_floor[:, h:h + 1]))
    for h in range(H_B):
        o = rt_ref[:, h:h + 1] * sa_ref[slot_b + h]
        for s in range(T):
            o = o + sc_b[h][s] * shift(vb[h], s)
        hid_b.append(o)
    for h in heads:
        o = gam[:, dcol[h]:dcol[h] + 1] * sa_ref[slot_q + h]
        for s in range(T):
            o = o + sc_c[h][s] * shift(ug[h], s)
        hid_c.append(o)
    mean_b = [jnp.mean(x, axis=-1, keepdims=True) for x in hid_b]
    hid_b = [x - mu for x, mu in zip(hid_b, mean_b)]
    hidden = hid_a + hid_b + hid_c
    inv_rms = [lax.rsqrt(jnp.mean(x * x, axis=-1, keepdims=True) + NORM_EPS) for x in hidden]
    for h in range(H_A):
        og = proj_ref[:, COL_A + 3 * D_A + h * HEAD_DIM:COL_A + 3 * D_A + (h + 1) * HEAD_DIM]
        ya = jax.nn.sigmoid(og) * (hidden[h] * inv_rms[h] * na_ref[h:h + 1, :])
        y_ref[:, h * HEAD_DIM:(h + 1) * HEAD_DIM] = ya.astype(y_ref.dtype)
    for h in range(H_B):
        gb = proj_ref[:, COL_B + 3 * D_B + h * HEAD_DIM:COL_B + 3 * D_B + (h + 1) * HEAD_DIM]
        yb = _silu(gb) * (hidden[H_A + h] * inv_rms[H_A + h] * nb_ref[h:h + 1, :])
        y_ref[:, D_A + h * HEAD_DIM:D_A + (h + 1) * HEAD_DIM] = yb.astype(y_ref.dtype)
    for h in heads:
        gz = proj_ref[:, COL_C + 3 * D_C + h * HEAD_DIM:COL_C + 3 * D_C + (h + 1) * HEAD_DIM]
        yc = _silu(gz) * (hidden[H_A + H_B + h] * inv_rms[H_A + H_B + h] * nc_ref[...])
        y_ref[:, D_A + D_B + h * HEAD_DIM:D_A + D_B + (h + 1) * HEAD_DIM] = yc.astype(y_ref.dtype)


def _sample_tables(T, BB):
    lg = _retention_log_gamma()
    tt = np.arange(T, dtype=np.float64)
    rt = np.zeros((T, 128), np.float64)
    rt[:, 0:H_B] = np.exp((tt[:, None] + 1.0) * lg[None, :])
    rt[:, H_B:2 * H_B] = np.exp((T - 1.0 - tt[:, None]) * lg[None, :])
    cos, sin = _rope_tables(PAST_LEN + np.arange(T))
    rep = lambda a: jnp.asarray(np.tile(a, (BB, 1)), f32)
    return rep(cos), rep(sin), rep(rt)


def _sample_mixers(layer, proj, gates, y, tables, gp, cw, na, nb, nc, states, prev_out, batch, T, row0, BB):
    c0, nrep, mrep, r0, g0, conv0 = states
    depth = c0.shape[0]
    R = T * BB
    nblk = batch // BB
    blk0 = row0 // R
    cos, sin, rt = tables
    full = lambda shape: pl.BlockSpec(shape, lambda i: (0,) * len(shape))
    st4 = lambda heads: pl.BlockSpec((None, BB, heads, HEAD_DIM, HEAD_DIM), lambda i: (layer, i, 0, 0, 0))
    rows = lambda width: pl.BlockSpec((None, R, width), lambda i: (layer, i, 0))
    convspec = pl.BlockSpec((None, BB, CONV_W - 1, 3 * D_C), lambda i: (layer, i, 0, 0))
    anyspec = pl.BlockSpec(memory_space=pl.ANY)
    in_specs = [
        pl.BlockSpec((R, N_PROJ), lambda i: (blk0 + i, 0)),
        pl.BlockSpec((R, 128), lambda i: (blk0 + i, 0)),
        full((R, HEAD_DIM)), full((R, HEAD_DIM)), full((R, 128)),
        full((8, 128)), full((CONV_W, 3 * D_C)),
        full((H_A, HEAD_DIM)), full((H_B, HEAD_DIM)), full((1, HEAD_DIM)),
        st4(H_A), rows(D_A), rows(128), st4(H_B), st4(H_C), convspec,
    ]
    args = [proj, gates, cos, sin, rt, gp, cw, na, nb, nc, c0, nrep, mrep, r0, g0, conv0]
    out_shape = [
        jax.ShapeDtypeStruct(y.shape, y.dtype),
        jax.ShapeDtypeStruct(c0.shape, f32),
        jax.ShapeDtypeStruct(nrep.shape, f32),
        jax.ShapeDtypeStruct(mrep.shape, f32),
        jax.ShapeDtypeStruct(r0.shape, f32),
        jax.ShapeDtypeStruct(g0.shape, f32),
        jax.ShapeDtypeStruct(conv0.shape, f32),
    ]
    out_specs = [pl.BlockSpec((R, D_MODEL), lambda i: (blk0 + i, 0)),
                 st4(H_A), rows(D_A), rows(128), st4(H_B), st4(H_C), convspec]
    first = prev_out is None
    if first:
        in_specs.append(anyspec)
        args.append(y)
        aliases = {16: 0}
    else:
        in_specs += [anyspec] * 7
        args += [y] + list(prev_out)
        aliases = {16 + j: j for j in range(7)}
    del depth
    return pl.pallas_call(
        functools.partial(_sample_mixer_kernel, T, BB, first),
        grid=(nblk,),
        in_specs=in_specs,
        out_specs=out_specs,
        out_shape=out_shape,
        input_output_aliases=aliases,
        scratch_shapes=[pltpu.VMEM((R, 3 * D_C), f32), pltpu.VMEM((H_A + H_B + 2 * H_C, R, HEAD_DIM), f32),
                        pltpu.VMEM((BB, 8, 3 * D_C), f32)],
        compiler_params=_cparams(("parallel",)),
        name="sample_mixers",
    )(*args)


def _repack_kernel(win_ref, main_ref, gate_ref):
    _repack_rows(win_ref, main_ref, gate_ref)


def _repack_w_in(w_in, layer, slab):
    _, d, n_in = w_in.shape
    return pl.pallas_call(
        _repack_kernel,
        grid=(d // slab,),
        in_specs=[pl.BlockSpec((None, slab, n_in), lambda i: (layer, i, 0))],
        out_specs=[pl.BlockSpec((slab, N_PROJ), lambda i: (i, 0)), pl.BlockSpec((slab, 128), lambda i: (i, 0))],
        out_shape=[jax.ShapeDtypeStruct((d, N_PROJ), bf16), jax.ShapeDtypeStruct((d, 128), bf16)],
        compiler_params=_cparams(("parallel",)),
        name="repack_w_in",
    )(w_in)


def _gate_params(gate_bias, dt_bias, a_log):
    depth = gate_bias.shape[0]
    gp = jnp.zeros((depth, 8, 128), f32)
    gp = gp.at[:, 0, LANE_I:LANE_I + 2 * H_A].set(gate_bias)
    gp = gp.at[:, 0, LANE_DECAY:LANE_DECAY + H_C].set(dt_bias)
    gp = gp.at[:, 1, LANE_DECAY:LANE_DECAY + H_C].set(a_log)
    return gp


def kernel(x_prompt, x_sample, state_mlstm_C, state_mlstm_n, state_mlstm_m, state_ret_S, state_gdn_S,
           state_gdn_conv, norm_mix_pre, norm_mix_post, norm_mlp_pre, norm_mlp_post, w_in, mlstm_gate_bias,
           gdn_conv_w, gdn_A_log, gdn_dt_bias, norm_mlstm, norm_ret, norm_gdn, w_out, w_up, w_down):
    bp, tp, d = x_prompt.shape
    bs, ts, _ = x_sample.shape
    depth = w_in.shape[0]
    rows_p = bp * tp
    rows_s = bs * ts
    rows = rows_p + rows_s

    x = jnp.concatenate([x_prompt.reshape(rows_p, d), x_sample.reshape(rows_s, d)], axis=0)
    w_in_p, w_gate_p = _repack_w_in(w_in, 0, 256)
    gp = _gate_params(mlstm_gate_bias, gdn_dt_bias, gdn_A_log)
    na = norm_mlstm.reshape(depth, H_A, HEAD_DIM)
    nb = norm_ret.reshape(depth, H_B, HEAD_DIM)
    nc = norm_gdn.reshape(depth, 1, HEAD_DIM)

    nrep = jnp.repeat(state_mlstm_n.reshape(depth, bs, D_A), ts, axis=1)
    mrep = jnp.pad(jnp.repeat(state_mlstm_m, ts, axis=1), ((0, 0), (0, 0), (0, 128 - H_A)))
    s_states = (state_mlstm_C, nrep, mrep, state_ret_S, state_gdn_S, state_gdn_conv)

    p_tables = _prompt_tables(tp, PROMPT_CHUNK)
    s_tables = _sample_tables(ts, SAMPLE_BLOCK)

    tm = rows // 8
    h = _rmsnorm_rows(x, norm_mix_pre[0], tm // 2)
    p_states = []
    s_out = None
    for l in range(depth):
        proj, w_out_b = _matmul_in(h, w_in_p, w_out, l, tm, 2048)
        gates = _matmul_gates(h, w_gate_p, tm)
        outs = _prompt_mixers(proj, gates, p_tables, gp[l], gdn_conv_w[l], na[l], nb[l], nc[l], bp, tp, rows,
                              PROMPT_CHUNK)
        p_states.append(outs[1:])
        res = _sample_mixers(l, proj, gates, outs[0], s_tables, gp[l], gdn_conv_w[l], na[l], nb[l], nc[l],
                             s_states, s_out, bs, ts, rows_p, SAMPLE_BLOCK)
        y, s_out = res[0], res[1:]
        x, h = _matmul_out(y, w_out_b, x, norm_mix_post[l], norm_mlp_pre[l], tm // 2)
        up = _matmul_up(h, w_up, w_down, w_in, l, tm, 1024)
        u, w_down_b = up[0], up[1]
        if l + 1 < depth:
            w_in_p, w_gate_p = up[2], up[3]
        x, h = _matmul_down(u, w_down_b, x, norm_mlp_post[l], norm_mix_pre[(l + 1) % depth], tm, 1024)

    stk = lambda j: jnp.stack([s[j] for s in p_states], axis=0)
    sc, sn, sm, sr, sg, sconv = s_out
    return (
        x[:rows_p].reshape(bp, tp, d), x[rows_p:].reshape(bs, ts, d),
        stk(0), stk(1), stk(2)[:, :, 0, :H_A], stk(3), stk(4), stk(5),
        sc, sn[:, ts - 1::ts, :].reshape(depth, bs, H_A, HEAD_DIM), sm[:, ts - 1::ts, :H_A], sr, sg,
        sconv,
    )
```

```python
import functools
import math

import numpy as np
import jax
import jax.numpy as jnp
from jax import lax
from jax.experimental import pallas as pl
from jax.experimental.pallas import tpu as pltpu

f32 = jnp.float32
bf16 = jnp.bfloat16

D_MODEL = 2048
HEAD_DIM = 128
H_A, H_B, H_C = 4, 4, 8
D_A, D_B, D_C = H_A * HEAD_DIM, H_B * HEAD_DIM, H_C * HEAD_DIM
D_FF = 4 * D_MODEL
CONV_W = 4
PAST_LEN = 16384
ROPE_BASE = 10000.0
GATE_SOFTCAP = 15.0
NORM_EPS = 1e-6
QK_SCALE = HEAD_DIM ** -0.5

COL_A = 0
COL_B = 4 * D_A
COL_C = COL_B + 4 * D_B
N_PROJ = COL_C + 4 * D_C
LANE_I, LANE_F, LANE_DECAY, LANE_BETA = 0, H_A, 2 * H_A, 2 * H_A + H_C

PROMPT_CHUNK = 64
INV_BLOCK = 8
SAMPLE_BLOCK = 8
NEG_BIG = -1e30
VMEM_LIMIT = 60 * 1024 * 1024


def _cparams(sem):
    return pltpu.CompilerParams(dimension_semantics=sem, vmem_limit_bytes=VMEM_LIMIT)


def _rms(x, g):
    return x * lax.rsqrt(jnp.mean(x * x, axis=-1, keepdims=True) + NORM_EPS) * g


def _rmsnorm_kernel(x_ref, g_ref, o_ref):
    o_ref[...] = _rms(x_ref[...], g_ref[...]).astype(o_ref.dtype)


def _rmsnorm_rows(x, g, tm):
    m, d = x.shape
    return pl.pallas_call(
        _rmsnorm_kernel,
        grid=(m // tm,),
        in_specs=[pl.BlockSpec((tm, d), lambda i: (i, 0)), pl.BlockSpec((1, d), lambda i: (0, 0))],
        out_specs=pl.BlockSpec((tm, d), lambda i: (i, 0)),
        out_shape=jax.ShapeDtypeStruct((m, d), bf16),
        compiler_params=_cparams(("parallel",)),
        name="rmsnorm_in",
    )(x, g.reshape(1, d))


def _dot_nt(a, w):
    return lax.dot_general(a, w, (((1,), (1,)), ((), ())), preferred_element_type=f32)


def _mm_gates_kernel(a_ref, w_ref, o_ref):
    o_ref[...] = _dot_nt(a_ref[...], w_ref[...])


def _matmul_gates(h, w_t, tm):
    m, k = h.shape
    n = w_t.shape[0]
    return pl.pallas_call(
        _mm_gates_kernel,
        grid=(m // tm,),
        in_specs=[pl.BlockSpec((tm, k), lambda i: (i, 0)), pl.BlockSpec((n, k), lambda i: (0, 0))],
        out_specs=pl.BlockSpec((tm, n), lambda i: (i, 0)),
        out_shape=jax.ShapeDtypeStruct((m, n), f32),
        compiler_params=_cparams(("parallel",)),
        name="proj_gates",
    )(h, w_t)


def _mm_in_kernel(a_ref, w_ref, wo_ref, o_ref, wob_ref):
    o_ref[...] = _dot_nt(a_ref[...], w_ref[...])
    wob_ref[...] = wo_ref[...].astype(bf16)


def _matmul_in(h, w_t, w_out, layer, tm, tn):
    m, k = h.shape
    n = w_t.shape[0]
    nj, ni = n // tn, m // tm
    d = w_out.shape[1]
    slab = d // (nj * ni)
    step = lambda j, i: j * ni + i
    return pl.pallas_call(
        _mm_in_kernel,
        grid=(nj, ni),
        in_specs=[pl.BlockSpec((tm, k), lambda j, i: (i, 0)), pl.BlockSpec((tn, k), lambda j, i: (j, 0)),
                  pl.BlockSpec((None, slab, d), lambda j, i: (layer, step(j, i), 0))],
        out_specs=[pl.BlockSpec((tm, tn), lambda j, i: (i, j)),
                   pl.BlockSpec((slab, d), lambda j, i: (step(j, i), 0))],
        out_shape=[jax.ShapeDtypeStruct((m, n), f32), jax.ShapeDtypeStruct((d, d), bf16)],
        compiler_params=_cparams(("parallel", "arbitrary")),
        name="proj_in",
    )(h, w_t, w_out)


REPACK_LANES = 128


def _repack_slab(win_ref, main_ref, gate_ref):
    n_a = 4 * D_A
    n_ag = n_a + 2 * H_A
    n_bc = 4 * D_B + 4 * D_C
    main_ref[0:n_a, :] = win_ref[0:n_a, :].astype(bf16)
    main_ref[n_a:n_a + n_bc, :] = win_ref[n_ag:n_ag + n_bc, :].astype(bf16)
    gate_ref[...] = jnp.zeros_like(gate_ref)
    gate_ref[LANE_I:LANE_I + 2 * H_A, :] = win_ref[n_a:n_ag, :].astype(bf16)
    gate_ref[LANE_DECAY:LANE_DECAY + 2 * H_C, :] = win_ref[n_ag + n_bc:n_ag + n_bc + 2 * H_C, :].astype(bf16)


def _mm_up_kernel(repack_every, a_ref, w_ref, wd_ref, *rest):
    if repack_every:
        win_ref, o_ref, wdb_ref, main_ref, gate_ref, wb_ref = rest
        step = pl.program_id(0) * pl.num_programs(1) + pl.program_id(1)

        @pl.when(step % repack_every == 0)
        def _():
            _repack_slab(win_ref, main_ref, gate_ref)
    else:
        o_ref, wdb_ref, wb_ref = rest

    @pl.when(pl.program_id(1) == 0)
    def _():
        wb_ref[...] = w_ref[...].astype(bf16)

    z = jnp.maximum(jnp.dot(a_ref[...], wb_ref[...], preferred_element_type=f32), 0.0)
    o_ref[...] = (z * z).astype(o_ref.dtype)
    wdb_ref[...] = wd_ref[...].astype(bf16)


def _matmul_up(h, w, w_down, w_in_t, layer, tm, tn):
    m, k = h.shape
    n = w.shape[2]
    nj, ni = n // tn, m // tm
    steps = nj * ni
    step = lambda j, i: j * ni + i
    with_repack = layer + 1 < w_in_t.shape[0]
    dslab = w_down.shape[1] // steps
    d = w_down.shape[2]
    in_specs = [pl.BlockSpec((tm, k), lambda j, i: (i, 0)), pl.BlockSpec((None, k, tn), lambda j, i: (layer, 0, j)),
                pl.BlockSpec((None, dslab, d), lambda j, i: (layer, step(j, i), 0))]
    out_specs = [pl.BlockSpec((tm, tn), lambda j, i: (i, j)), pl.BlockSpec((dslab, d), lambda j, i: (step(j, i), 0))]
    out_shape = [jax.ShapeDtypeStruct((m, n), bf16), jax.ShapeDtypeStruct(w_down.shape[1:], bf16)]
    args = [h, w, w_down]
    every = 0
    if with_repack:
        n_in, dm = w_in_t.shape[1:]
        every = steps // (dm // REPACK_LANES)
        slab = lambda j, i: step(j, i) // every
        in_specs.append(pl.BlockSpec((None, n_in, REPACK_LANES), lambda j, i: (layer + 1, 0, slab(j, i))))
        out_specs += [pl.BlockSpec((N_PROJ, REPACK_LANES), lambda j, i: (0, slab(j, i))),
                      pl.BlockSpec((128, REPACK_LANES), lambda j, i: (0, slab(j, i)))]
        out_shape += [jax.ShapeDtypeStruct((N_PROJ, dm), bf16), jax.ShapeDtypeStruct((128, dm), bf16)]
        args.append(w_in_t)
    return pl.pallas_call(
        functools.partial(_mm_up_kernel, every),
        grid=(nj, ni),
        in_specs=in_specs,
        out_specs=out_specs,
        out_shape=out_shape,
        scratch_shapes=[pltpu.VMEM((k, tn), bf16)],
        compiler_params=_cparams(("parallel", "arbitrary")),
        name="mlp_up",
    )(*args)


def _residual_epilogue(z, x_ref, gpost_ref, gnext_ref, xo_ref, ho_ref):
    x_new = x_ref[...] + _rms(z, gpost_ref[...])
    xo_ref[...] = x_new
    ho_ref[...] = _rms(x_new, gnext_ref[...]).astype(ho_ref.dtype)


def _mm_out_kernel(a_ref, w_ref, x_ref, gpost_ref, gnext_ref, xo_ref, ho_ref):
    z = jnp.dot(a_ref[...], w_ref[...], preferred_element_type=f32)
    _residual_epilogue(z, x_ref, gpost_ref, gnext_ref, xo_ref, ho_ref)


def _matmul_out(y, w, x, g_post, g_next, tm):
    m, k = y.shape
    d = w.shape[1]
    row = lambda i: (i, 0)
    fixed = lambda i: (0, 0)
    return pl.pallas_call(
        _mm_out_kernel,
        grid=(m // tm,),
        in_specs=[pl.BlockSpec((tm, k), row), pl.BlockSpec((k, d), fixed),
                  pl.BlockSpec((tm, d), row),
                  pl.BlockSpec((1, d), fixed), pl.BlockSpec((1, d), fixed)],
        out_specs=[pl.BlockSpec((tm, d), row), pl.BlockSpec((tm, d), row)],
        out_shape=[jax.ShapeDtypeStruct((m, d), f32), jax.ShapeDtypeStruct((m, d), bf16)],
        compiler_params=_cparams(("parallel",)),
        name="proj_out",
    )(y, w, x, g_post.reshape(1, d), g_next.reshape(1, d))


def _mm_down_kernel(a_ref, w_ref, x_ref, gpost_ref, gnext_ref, xo_ref, ho_ref):
    kk = pl.program_id(1)

    @pl.when(kk == 0)
    def _():
        xo_ref[...] = jnp.zeros_like(xo_ref)

    xo_ref[...] += jnp.dot(a_ref[...], w_ref[...], preferred_element_type=f32)

    @pl.when(kk == pl.num_programs(1) - 1)
    def _():
        _residual_epilogue(xo_ref[...], x_ref, gpost_ref, gnext_ref, xo_ref, ho_ref)


def _matmul_down(u, w, x, g_post, g_next, tm, tk):
    m, k = u.shape
    d = w.shape[1]
    row = lambda i, kk: (i, 0)
    fixed = lambda i, kk: (0, 0)
    return pl.pallas_call(
        _mm_down_kernel,
        grid=(m // tm, k // tk),
        in_specs=[pl.BlockSpec((tm, tk), lambda i, kk: (i, kk)),
                  pl.BlockSpec((tk, d), lambda i, kk: (kk, 0)),
                  pl.BlockSpec((tm, d), row), pl.BlockSpec((1, d), fixed), pl.BlockSpec((1, d), fixed)],
        out_specs=[pl.BlockSpec((tm, d), row), pl.BlockSpec((tm, d), row)],
        out_shape=[jax.ShapeDtypeStruct((m, d), f32), jax.ShapeDtypeStruct((m, d), bf16)],
        compiler_params=_cparams(("parallel", "arbitrary")),
        name="mlp_down",
    )(u, w, x, g_post.reshape(1, d), g_next.reshape(1, d))


def _dot(a, b):
    return jnp.dot(a, b, preferred_element_type=f32)


def _dot_tb(a, b):
    return lax.dot_general(a, b, (((1,), (1,)), ((), ())), preferred_element_type=f32)


def _dot_ta(a, b):
    return lax.dot_general(a, b, (((0,), (0,)), ((), ())), preferred_element_type=f32)


def _lane_sum(x):
    return jnp.sum(x, axis=-1, keepdims=True)


def _softplus(z):
    return jnp.maximum(z, 0.0) + jnp.log1p(jnp.exp(-jnp.abs(z)))


def _gate_transform(gt, gp):
    z = gt + gp[0:1, :]
    capped = GATE_SOFTCAP * jnp.tanh(z / GATE_SOFTCAP)
    log_f = -_softplus(-capped)
    decay = -jnp.exp(gp[1:2, :]) * _softplus(z)
    beta = jax.nn.sigmoid(gt)
    return capped, log_f, decay, beta


def _head_norm(x, gain, center):
    if center:
        x = x - jnp.mean(x, axis=-1, keepdims=True)
    return x * lax.rsqrt(jnp.mean(x * x, axis=-1, keepdims=True) + NORM_EPS) * gain


def _l2norm(x):
    return x * lax.rsqrt(_lane_sum(x * x) + NORM_EPS)


def _silu(x):
    return x * jax.nn.sigmoid(x)


def _rotary(x, cos, sin_signed):
    return x * cos + pltpu.roll(x, HEAD_DIM // 2, 1) * sin_signed


def _retention_log_gamma():
    return np.log1p(-np.exp2(-5.0 - np.arange(H_B, dtype=np.float64)))


def _rope_tables(pos):
    half = HEAD_DIM // 2
    inv = ROPE_BASE ** (-np.arange(half, dtype=np.float64) / half)
    ang = np.asarray(pos, dtype=np.float64)[:, None] * inv[None, :]
    cos = np.concatenate([np.cos(ang), np.cos(ang)], axis=-1)
    sin = np.concatenate([-np.sin(ang), np.sin(ang)], axis=-1)
    return cos.astype(np.float32), sin.astype(np.float32)


def _scan_rows(x, row, length, op, fill):
    s = 1
    while s < length:
        x = op(x, jnp.where(row >= s, pltpu.roll(x, s, 0), fill))
        s *= 2
    return x


def _rows_to_lanes(x, length):
    if length < 128:
        x = jnp.concatenate([x, jnp.zeros((128 - length, 128), x.dtype)], axis=0)
    return x.T[:, :length]


def _prompt_mixer_kernel(L, proj_ref, gt_ref, cos_ref, sin_ref, rd_ref, rq_ref, rk_ref, gp_ref, cw_ref,
                         na_ref, nb_ref, nc_ref,
                         y_ref, c_ref, n_ref, m_ref, r_ref, g_ref, conv_ref, ext_ref):
    c = pl.program_id(1)

    @pl.when(c == 0)
    def _():
        c_ref[...] = jnp.zeros_like(c_ref)
        n_ref[...] = jnp.zeros_like(n_ref)
        m_ref[...] = jnp.zeros_like(m_ref)
        r_ref[...] = jnp.zeros_like(r_ref)
        g_ref[...] = jnp.zeros_like(g_ref)
        ext_ref[0:8, :] = jnp.zeros((8, 3 * D_C), f32)

    row = lax.broadcasted_iota(jnp.int32, (L, 128), 0)
    lane = lax.broadcasted_iota(jnp.int32, (L, 128), 1)
    ri = lax.broadcasted_iota(jnp.int32, (L, L), 0)
    ci = lax.broadcasted_iota(jnp.int32, (L, L), 1)
    incl = ri >= ci
    strict = ri > ci

    capped, log_f, decay, beta = _gate_transform(gt_ref[...], gp_ref[...])

    head_lane = lane < H_A
    ig = jnp.where(head_lane, capped, 0.0)
    lf = jnp.where(head_lane, pltpu.roll(log_f, 128 - LANE_F, 1), 0.0)
    mp = m_ref[0]
    F = _scan_rows(lf, row, L, jnp.add, 0.0)
    m = F + jnp.maximum(mp, _scan_rows(ig - F, row, L, jnp.maximum, NEG_BIG))
    m_new = m[L - 1:L, :]
    f_last = F[L - 1:L, :]
    a_rows = F - m
    b_lanes = _rows_to_lanes(ig - F, L)
    inter = jnp.exp(F + mp - m)
    inv_floor = jnp.exp(-m)
    wl = jnp.exp(ig + f_last - F - m_new)
    dec = jnp.exp(f_last + mp - m_new)
    m_ref[0] = m_new

    def mlstm_head(h):
        q = proj_ref[:, COL_A + h * HEAD_DIM:COL_A + (h + 1) * HEAD_DIM]
        k = proj_ref[:, COL_A + D_A + h * HEAD_DIM:COL_A + D_A + (h + 1) * HEAD_DIM] * QK_SCALE
        v = proj_ref[:, COL_A + 2 * D_A + h * HEAD_DIM:COL_A + 2 * D_A + (h + 1) * HEAD_DIM]
        og = proj_ref[:, COL_A + 3 * D_A + h * HEAD_DIM:COL_A + 3 * D_A + (h + 1) * HEAD_DIM]
        logw = a_rows[:, h:h + 1] + b_lanes[h:h + 1, :]
        s = _dot_tb(q, k) * jnp.exp(jnp.where(incl, logw, NEG_BIG))
        c_old = c_ref[0, h]
        n_old = n_ref[0, h:h + 1, :]
        inter_h = inter[:, h:h + 1]
        num = inter_h * _dot(q, c_old) + _dot(s, v)
        den = inter_h * _lane_sum(q * n_old) + _lane_sum(s)
        hh = num / jnp.maximum(jnp.abs(den), inv_floor[:, h:h + 1])
        kw = k * wl[:, h:h + 1]
        dec_h = dec[:, h:h + 1]
        c_ref[0, h] = dec_h * c_old + _dot_ta(kw, v)
        n_ref[0, h:h + 1, :] = dec_h * n_old + jnp.sum(kw, axis=0, keepdims=True)
        ya = jax.nn.sigmoid(og) * _head_norm(hh, na_ref[h:h + 1, :], False)
        y_ref[:, h * HEAD_DIM:(h + 1) * HEAD_DIM] = ya.astype(y_ref.dtype)

    lg = _retention_log_gamma()

    def retention_head(h):
        cos = cos_ref[...]
        sin = sin_ref[...]
        q = proj_ref[:, COL_B + h * HEAD_DIM:COL_B + (h + 1) * HEAD_DIM]
        k = proj_ref[:, COL_B + D_B + h * HEAD_DIM:COL_B + D_B + (h + 1) * HEAD_DIM]
        v = proj_ref[:, COL_B + 2 * D_B + h * HEAD_DIM:COL_B + 2 * D_B + (h + 1) * HEAD_DIM]
        gb = proj_ref[:, COL_B + 3 * D_B + h * HEAD_DIM:COL_B + 3 * D_B + (h + 1) * HEAD_DIM]
        qr = _rotary(q, cos, sin)
        kr = _rotary(k, cos, sin) * QK_SCALE
        s_old = r_ref[0, h]
        inner = _dot_tb(qr, kr) * rd_ref[h]
        o = _dot(inner, v) + rq_ref[h] * _dot(qr, s_old)
        r_ref[0, h] = float(np.exp(L * lg[h])) * s_old + _dot_ta(kr * rk_ref[h], v)
        yb = _silu(gb) * _head_norm(o, nb_ref[h:h + 1, :], True)
        y_ref[:, D_A + h * HEAD_DIM:D_A + (h + 1) * HEAD_DIM] = yb.astype(y_ref.dtype)

    fillers = [functools.partial(mlstm_head, h) for h in range(H_A)]
    fillers += [functools.partial(retention_head, h) for h in range(H_B)]

    def emit_fillers(count):
        for _ in range(min(count, len(fillers))):
            fillers.pop(0)()

    ext_ref[8:8 + L, :] = proj_ref[:, COL_C:COL_C + 3 * D_C]
    G = _scan_rows(decay, row, L, jnp.add, 0.0)
    g_lanes = _rows_to_lanes(G, L)
    gam = jnp.exp(G)
    g_last = G[L - 1:L, :]
    k_decay = jnp.exp(g_last - G)
    s_decay = jnp.exp(g_last)

    def conv_block(col):
        acc = cw_ref[CONV_W - 1:CONV_W, col:col + HEAD_DIM] * ext_ref[8:8 + L, col:col + HEAD_DIM]
        for w in range(CONV_W - 1):
            off = 8 - (CONV_W - 1) + w
            acc = acc + cw_ref[w:w + 1, col:col + HEAD_DIM] * ext_ref[off:off + L, col:col + HEAD_DIM]
        return _silu(acc)

    heads = range(H_C)
    dcol = [LANE_DECAY + h for h in heads]
    beta_c = [beta[:, LANE_BETA + h:LANE_BETA + h + 1] for h in heads]
    q = [_l2norm(conv_block(h * HEAD_DIM)) * QK_SCALE for h in heads]
    k = [_l2norm(conv_block(D_C + h * HEAD_DIM)) for h in heads]
    v = [conv_block(2 * D_C + h * HEAD_DIM) for h in heads]
    dec_in = [jnp.exp(jnp.where(incl, G[:, dcol[h]:dcol[h] + 1] - g_lanes[dcol[h]:dcol[h] + 1, :], NEG_BIG))
              for h in heads]
    a_mat = [jnp.where(strict, dec_in[h], 0.0) * beta_c[h] * _dot_tb(k[h], k[h]) for h in heads]
    qk = [_dot_tb(q[h], k[h]) * dec_in[h] for h in heads]
    blk = min(INV_BLOCK, L)
    same = (ri // blk) == (ci // blk)
    pw = [jnp.where(same, -a, 0.0) for a in a_mat]
    e_mat = list(pw)
    span = 1
    while 2 * span < blk:
        pw = [_dot(p, p) for p in pw]
        e_mat = [e_mat[h] + pw[h] + _dot(e_mat[h], pw[h]) for h in heads]
        emit_fillers(1)
        span *= 2
    while blk < L:
        wider = (ri // (2 * blk)) == (ci // (2 * blk))
        a_off = [jnp.where(wider & jnp.logical_not(same), a, 0.0) for a in a_mat]
        low = [a_off[h] + _dot(e_mat[h], a_off[h]) for h in heads]
        e_mat = [e_mat[h] - low[h] - _dot(low[h], e_mat[h]) for h in heads]
        emit_fillers(2)
        same = wider
        blk *= 2
    rhs = [jnp.concatenate([(beta_c[h] * gam[:, dcol[h]:dcol[h] + 1]) * k[h], beta_c[h] * v[h]], axis=1)
           for h in heads]
    sol = [rhs[h] + _dot(e_mat[h], rhs[h]) for h in heads]
    emit_fillers(len(fillers))
    s_old = [g_ref[0, h] for h in heads]
    both = [_dot(jnp.concatenate([sol[h][:, :HEAD_DIM], q[h]], axis=0), s_old[h]) for h in heads]
    u = [sol[h][:, HEAD_DIM:] - both[h][:L] for h in heads]
    o = [gam[:, dcol[h]:dcol[h] + 1] * both[h][L:] + _dot(qk[h], u[h]) for h in heads]
    for h in heads:
        g_ref[0, h] = s_decay[:, dcol[h]:dcol[h] + 1] * s_old[h] + _dot_ta(k[h] * k_decay[:, dcol[h]:dcol[h] + 1], u[h])
    for h in heads:
        gz = proj_ref[:, COL_C + 3 * D_C + h * HEAD_DIM:COL_C + 3 * D_C + (h + 1) * HEAD_DIM]
        yc = _silu(gz) * _head_norm(o[h], nc_ref[...], False)
        y_ref[:, D_A + D_B + h * HEAD_DIM:D_A + D_B + (h + 1) * HEAD_DIM] = yc.astype(y_ref.dtype)

    ext_ref[0:8, :] = ext_ref[L:L + 8, :]

    @pl.when(c == pl.num_programs(1) - 1)
    def _():
        conv_ref[0] = ext_ref[8 + L - (CONV_W - 1):8 + L, :]


def _prompt_tables(seq, L):
    lg = _retention_log_gamma()
    i = np.arange(L, dtype=np.float64)
    diff = i[:, None] - i[None, :]
    rd = np.where(diff >= 0, np.exp(np.maximum(diff, 0.0) * lg[:, None, None]), 0.0)
    rq = np.broadcast_to(np.exp((i + 1.0) * lg[:, None])[..., None], (H_B, L, HEAD_DIM))
    rk = np.broadcast_to(np.exp((L - 1.0 - i) * lg[:, None])[..., None], (H_B, L, HEAD_DIM))
    cos, sin = _rope_tables(np.arange(seq))
    return (jnp.asarray(cos), jnp.asarray(sin), jnp.asarray(rd, f32), jnp.asarray(rq, f32), jnp.asarray(rk, f32))


def _prompt_mixers(proj, gates, tables, gp, cw, na, nb, nc, batch, seq, total_rows, L):
    nchunk = seq // L
    cos, sin, rd, rq, rk = tables
    full = lambda shape: pl.BlockSpec(shape, lambda b, c: (0,) * len(shape))
    state4 = lambda heads: pl.BlockSpec((1, heads, HEAD_DIM, HEAD_DIM), lambda b, c: (b, 0, 0, 0))
    return pl.pallas_call(
        functools.partial(_prompt_mixer_kernel, L),
        grid=(batch, nchunk),
        in_specs=[
            pl.BlockSpec((L, N_PROJ), lambda b, c: (b * nchunk + c, 0)),
            pl.BlockSpec((L, 128), lambda b, c: (b * nchunk + c, 0)),
            pl.BlockSpec((L, HEAD_DIM), lambda b, c: (c, 0)),
            pl.BlockSpec((L, HEAD_DIM), lambda b, c: (c, 0)),
            full((H_B, L, L)), full((H_B, L, HEAD_DIM)), full((H_B, L, HEAD_DIM)),
            full((8, 128)), full((CONV_W, 3 * D_C)),
            full((H_A, HEAD_DIM)), full((H_B, HEAD_DIM)), full((1, HEAD_DIM)),
        ],
        out_specs=[
            pl.BlockSpec((L, D_MODEL), lambda b, c: (b * nchunk + c, 0)),
            state4(H_A),
            pl.BlockSpec((1, H_A, HEAD_DIM), lambda b, c: (b, 0, 0)),
            pl.BlockSpec((1, 1, 128), lambda b, c: (b, 0, 0)),
            state4(H_B),
            state4(H_C),
            pl.BlockSpec((1, CONV_W - 1, 3 * D_C), lambda b, c: (b, 0, 0)),
        ],
        out_shape=[
            jax.ShapeDtypeStruct((total_rows, D_MODEL), bf16),
            jax.ShapeDtypeStruct((batch, H_A, HEAD_DIM, HEAD_DIM), f32),
            jax.ShapeDtypeStruct((batch, H_A, HEAD_DIM), f32),
            jax.ShapeDtypeStruct((batch, 1, 128), f32),
            jax.ShapeDtypeStruct((batch, H_B, HEAD_DIM, HEAD_DIM), f32),
            jax.ShapeDtypeStruct((batch, H_C, HEAD_DIM, HEAD_DIM), f32),
            jax.ShapeDtypeStruct((batch, CONV_W - 1, 3 * D_C), f32),
        ],
        scratch_shapes=[pltpu.VMEM((L + 8, 3 * D_C), f32)],
        compiler_params=_cparams(("parallel", "arbitrary")),
        name="prompt_mixers",
    )(proj, gates, cos, sin, rd, rq, rk, gp, cw, na, nb, nc)


def _sample_mixer_kernel(T, BB, layer_first, *refs):
    (proj_ref, gt_ref, cos_ref, sin_ref, rt_ref, gp_ref, cw_ref, na_ref, nb_ref, nc_ref,
     c0_ref, nrep_ref, mrep_ref, r0_ref, g0_ref, conv0_ref) = refs[:16]
    rest = refs[16 + (1 if layer_first else 7):]
    (y_ref, c_ref, nout_ref, mout_ref, r_ref, g_ref, conv_ref, cv_ref, sa_ref, ext_ref) = rest
    R = T * BB
    row = lax.broadcasted_iota(jnp.int32, (R, 128), 0)
    lane = lax.broadcasted_iota(jnp.int32, (R, 128), 1)
    t = lax.rem(row, T)
    tcol = t[:, 0:1]

    def shift(x, s):
        return x if s == 0 else pltpu.roll(x, s, 0)

    def seg_scan(x, op, fill):
        s = 1
        while s < T:
            x = op(x, jnp.where(t >= s, shift(x, s), fill))
            s *= 2
        return x

    def last_rep(x):
        x_last = jnp.where(t == T - 1, x, 0.0)
        out = x_last
        for s in range(1, T):
            out = out + pltpu.roll(x_last, R - s, 0)
        return out

    capped, log_f, decay, beta = _gate_transform(gt_ref[...], gp_ref[...])

    head_lane = lane < H_A
    ig = jnp.where(head_lane, capped, 0.0)
    lf = jnp.where(head_lane, pltpu.roll(log_f, 128 - LANE_F, 1), 0.0)
    mp = mrep_ref[...]
    F = seg_scan(lf, jnp.add, 0.0)
    m = F + jnp.maximum(mp, seg_scan(ig - F, jnp.maximum, NEG_BIG))
    m_new = last_rep(m)
    f_last = last_rep(F)
    inter = jnp.exp(F + mp - m)
    inv_floor = jnp.exp(-m)
    wl = jnp.exp(ig + f_last - F - m_new)
    dec = jnp.exp(f_last + mp - m_new)
    mout_ref[...] = m_new
    pw = [jnp.where(t >= s, jnp.exp(F - shift(F, s) + shift(ig, s) - m), 0.0) for s in range(T)]

    def seq(x, b):
        return x[b * T:(b + 1) * T]

    slot_a, slot_b, slot_w, slot_q = 0, H_A, H_A + H_B, H_A + H_B + H_C

    qa = [proj_ref[:, COL_A + h * HEAD_DIM:COL_A + (h + 1) * HEAD_DIM] for h in range(H_A)]
    for h in range(H_A):
        for b in range(BB):
            sa_ref[slot_a + h, b * T:(b + 1) * T, :] = _dot(seq(qa[h], b), c0_ref[b, h])
    cos = cos_ref[...]
    sin = sin_ref[...]
    lg = _retention_log_gamma()
    qr = [_rotary(proj_ref[:, COL_B + h * HEAD_DIM:COL_B + (h + 1) * HEAD_DIM], cos, sin) for h in range(H_B)]
    kr = [_rotary(proj_ref[:, COL_B + D_B + h * HEAD_DIM:COL_B + D_B + (h + 1) * HEAD_DIM], cos, sin) * QK_SCALE
          for h in range(H_B)]
    for h in range(H_B):
        for b in range(BB):
            sa_ref[slot_b + h, b * T:(b + 1) * T, :] = _dot(seq(qr[h], b), r0_ref[b, h])

    ka = [proj_ref[:, COL_A + D_A + h * HEAD_DIM:COL_A + D_A + (h + 1) * HEAD_DIM] * QK_SCALE for h in range(H_A)]
    va = [proj_ref[:, COL_A + 2 * D_A + h * HEAD_DIM:COL_A + 2 * D_A + (h + 1) * HEAD_DIM] for h in range(H_A)]
    n_old = [nrep_ref[:, h * HEAD_DIM:(h + 1) * HEAD_DIM] for h in range(H_A)]
    kw = [ka[h] * wl[:, h:h + 1] for h in range(H_A)]
    for h in range(H_A):
        for b in range(BB):
            dec_bh = dec[b * T:b * T + 1, h:h + 1]
            c_ref[b, h] = dec_bh * c0_ref[b, h] + _dot_ta(seq(kw[h], b), seq(va[h], b))
        nout_ref[:, h * HEAD_DIM:(h + 1) * HEAD_DIM] = dec[:, h:h + 1] * n_old[h] + seg_scan(kw[h], jnp.add, 0.0)
    vb = [proj_ref[:, COL_B + 2 * D_B + h * HEAD_DIM:COL_B + 2 * D_B + (h + 1) * HEAD_DIM] for h in range(H_B)]
    for h in range(H_B):
        kd = kr[h] * rt_ref[:, H_B + h:H_B + h + 1]
        for b in range(BB):
            r_ref[b, h] = float(np.exp(T * lg[h])) * r0_ref[b, h] + _dot_ta(seq(kd, b), seq(vb[h], b))

    for b in range(BB):
        ext_ref[b, 0:CONV_W - 1, :] = conv0_ref[b]
        ext_ref[b, CONV_W - 1:CONV_W - 1 + T, :] = proj_ref[b * T:(b + 1) * T, COL_C:COL_C + 3 * D_C]
        acc = cw_ref[0:1, :] * ext_ref[b, 0:T, :]
        for w in range(1, CONV_W):
            acc = acc + cw_ref[w:w + 1, :] * ext_ref[b, w:w + T, :]
        cv_ref[b * T:(b + 1) * T, :] = _silu(acc)
        conv_ref[b] = ext_ref[b, T:T + CONV_W - 1, :]

    G = seg_scan(decay, jnp.add, 0.0)
    gam = jnp.exp(G)
    g_last = last_rep(G)
    k_decay = jnp.exp(g_last - G)
    s_decay = jnp.exp(g_last)
    dshift = [None] + [jnp.where(t >= s, jnp.exp(G - shift(G, s)), 0.0) for s in range(1, T)]

    heads = range(H_C)
    dcol = [LANE_DECAY + h for h in heads]
    beta_c = [beta[:, LANE_BETA + h:LANE_BETA + h + 1] for h in heads]
    qg = [_l2norm(cv_ref[:, h * HEAD_DIM:(h + 1) * HEAD_DIM]) * QK_SCALE for h in heads]
    kg = [_l2norm(cv_ref[:, D_C + h * HEAD_DIM:D_C + (h + 1) * HEAD_DIM]) for h in heads]
    vg = [cv_ref[:, 2 * D_C + h * HEAD_DIM:2 * D_C + (h + 1) * HEAD_DIM] for h in heads]
    a_sub = [[None] + [beta_c[h] * dshift[s][:, dcol[h]:dcol[h] + 1] * _lane_sum(kg[h] * shift(kg[h], s))
                       for s in range(1, T)] for h in heads]
    rhs_w = [(beta_c[h] * gam[:, dcol[h]:dcol[h] + 1]) * kg[h] for h in heads]
    rhs_u = [beta_c[h] * vg[h] for h in heads]
    w_sol, u_sol = list(rhs_w), list(rhs_u)
    for i in range(1, T):
        for h in heads:
            upd_w = a_sub[h][1] * shift(w_sol[h], 1)
            upd_u = a_sub[h][1] * shift(u_sol[h], 1)
            for s in range(2, i + 1):
                upd_w = upd_w + a_sub[h][s] * shift(w_sol[h], s)
                upd_u = upd_u + a_sub[h][s] * shift(u_sol[h], s)
            w_sol[h] = jnp.where(t == i, rhs_w[h] - upd_w, w_sol[h])
            u_sol[h] = jnp.where(t == i, rhs_u[h] - upd_u, u_sol[h])
    for h in heads:
        for b in range(BB):
            both = _dot(jnp.concatenate([seq(w_sol[h], b), seq(qg[h], b)], axis=0), g0_ref[b, h])
            sa_ref[slot_w + h, b * T:(b + 1) * T, :] = both[:T]
            sa_ref[slot_q + h, b * T:(b + 1) * T, :] = both[T:]

    ug = [u_sol[h] - sa_ref[slot_w + h] for h in heads]
    for h in heads:
        kd = kg[h] * k_decay[:, dcol[h]:dcol[h] + 1]
        for b in range(BB):
            g_ref[b, h] = (s_decay[b * T:b * T + 1, dcol[h]:dcol[h] + 1] * g0_ref[b, h]
                           + _dot_ta(seq(kd, b), seq(ug[h], b)))

    sc_a = [[_lane_sum(qa[h] * shift(ka[h], s)) * pw[s][:, h:h + 1] for s in range(T)] for h in range(H_A)]
    qn = [_lane_sum(qa[h] * n_old[h]) for h in range(H_A)]
    sc_b = [[jnp.where(tcol >= s, _lane_sum(qr[h] * shift(kr[h], s)) * float(np.exp(s * lg[h])), 0.0)
             for s in range(T)] for h in range(H_B)]
    sc_c = [[_lane_sum(qg[h] * shift(kg[h], s)) * (1.0 if s == 0 else dshift[s][:, dcol[h]:dcol[h] + 1])
             for s in range(T)] for h in heads]
    hid_a, hid_b, hid_c = [], [], []
    for h in range(H_A):
        inter_h = inter[:, h:h + 1]
        num = inter_h * sa_ref[slot_a + h]
        den = inter_h * qn[h]
        for s in range(T):
            num = num + sc_a[h][s] * shift(va[h], s)
            den = den + sc_a[h][s]
        hid_a.append(num / jnp.maximum(jnp.abs(den), inv_floor[:, h:h + 1]))
    for h in range(H_B):
        o = rt_ref[:, h:h + 1] * sa_ref[slot_b + h]
        for s in range(T):
            o = o + sc_b[h][s] * shift(vb[h], s)
        hid_b.append(o)
    for h in heads:
        o = gam[:, dcol[h]:dcol[h] + 1] * sa_ref[slot_q + h]
        for s in range(T):
            o = o + sc_c[h][s] * shift(ug[h], s)
        hid_c.append(o)
    mean_b = [jnp.mean(x, axis=-1, keepdims=True) for x in hid_b]
    hid_b = [x - mu for x, mu in zip(hid_b, mean_b)]
    hidden = hid_a + hid_b + hid_c
    inv_rms = [lax.rsqrt(jnp.mean(x * x, axis=-1, keepdims=True) + NORM_EPS) for x in hidden]
    for h in range(H_A):
        og = proj_ref[:, COL_A + 3 * D_A + h * HEAD_DIM:COL_A + 3 * D_A + (h + 1) * HEAD_DIM]
        ya = jax.nn.sigmoid(og) * (hidden[h] * inv_rms[h] * na_ref[h:h + 1, :])
        y_ref[:, h * HEAD_DIM:(h + 1) * HEAD_DIM] = ya.astype(y_ref.dtype)
    for h in range(H_B):
        gb = proj_ref[:, COL_B + 3 * D_B + h * HEAD_DIM:COL_B + 3 * D_B + (h + 1) * HEAD_DIM]
        yb = _silu(gb) * (hidden[H_A + h] * inv_rms[H_A + h] * nb_ref[h:h + 1, :])
        y_ref[:, D_A + h * HEAD_DIM:D_A + (h + 1) * HEAD_DIM] = yb.astype(y_ref.dtype)
    for h in heads:
        gz = proj_ref[:, COL_C + 3 * D_C + h * HEAD_DIM:COL_C + 3 * D_C + (h + 1) * HEAD_DIM]
        yc = _silu(gz) * (hidden[H_A + H_B + h] * inv_rms[H_A + H_B + h] * nc_ref[...])
        y_ref[:, D_A + D_B + h * HEAD_DIM:D_A + D_B + (h + 1) * HEAD_DIM] = yc.astype(y_ref.dtype)


def _sample_tables(T, BB):
    lg = _retention_log_gamma()
    tt = np.arange(T, dtype=np.float64)
    rt = np.zeros((T, 128), np.float64)
    rt[:, 0:H_B] = np.exp((tt[:, None] + 1.0) * lg[None, :])
    rt[:, H_B:2 * H_B] = np.exp((T - 1.0 - tt[:, None]) * lg[None, :])
    cos, sin = _rope_tables(PAST_LEN + np.arange(T))
    rep = lambda a: jnp.asarray(np.tile(a, (BB, 1)), f32)
    return rep(cos), rep(sin), rep(rt)


def _sample_mixers(layer, proj, gates, y, tables, gp, cw, na, nb, nc, states, prev_out, batch, T, row0, BB):
    c0, nrep, mrep, r0, g0, conv0 = states
    depth = c0.shape[0]
    R = T * BB
    nblk = batch // BB
    blk0 = row0 // R
    cos, sin, rt = tables
    full = lambda shape: pl.BlockSpec(shape, lambda i: (0,) * len(shape))
    st4 = lambda heads: pl.BlockSpec((None, BB, heads, HEAD_DIM, HEAD_DIM), lambda i: (layer, i, 0, 0, 0))
    rows = lambda width: pl.BlockSpec((None, R, width), lambda i: (layer, i, 0))
    convspec = pl.BlockSpec((None, BB, CONV_W - 1, 3 * D_C), lambda i: (layer, i, 0, 0))
    anyspec = pl.BlockSpec(memory_space=pl.ANY)
    in_specs = [
        pl.BlockSpec((R, N_PROJ), lambda i: (blk0 + i, 0)),
        pl.BlockSpec((R, 128), lambda i: (blk0 + i, 0)),
        full((R, HEAD_DIM)), full((R, HEAD_DIM)), full((R, 128)),
        full((8, 128)), full((CONV_W, 3 * D_C)),
        full((H_A, HEAD_DIM)), full((H_B, HEAD_DIM)), full((1, HEAD_DIM)),
        st4(H_A), rows(D_A), rows(128), st4(H_B), st4(H_C), convspec,
    ]
    args = [proj, gates, cos, sin, rt, gp, cw, na, nb, nc, c0, nrep, mrep, r0, g0, conv0]
    out_shape = [
        jax.ShapeDtypeStruct(y.shape, y.dtype),
        jax.ShapeDtypeStruct(c0.shape, f32),
        jax.ShapeDtypeStruct(nrep.shape, f32),
        jax.ShapeDtypeStruct(mrep.shape, f32),
        jax.ShapeDtypeStruct(r0.shape, f32),
        jax.ShapeDtypeStruct(g0.shape, f32),
        jax.ShapeDtypeStruct(conv0.shape, f32),
    ]
    out_specs = [pl.BlockSpec((R, D_MODEL), lambda i: (blk0 + i, 0)),
                 st4(H_A), rows(D_A), rows(128), st4(H_B), st4(H_C), convspec]
    first = prev_out is None
    if first:
        in_specs.append(anyspec)
        args.append(y)
        aliases = {16: 0}
    else:
        in_specs += [anyspec] * 7
        args += [y] + list(prev_out)
        aliases = {16 + j: j for j in range(7)}
    del depth
    return pl.pallas_call(
        functools.partial(_sample_mixer_kernel, T, BB, first),
        grid=(nblk,),
        in_specs=in_specs,
        out_specs=out_specs,
        out_shape=out_shape,
        input_output_aliases=aliases,
        scratch_shapes=[pltpu.VMEM((R, 3 * D_C), f32), pltpu.VMEM((H_A + H_B + 2 * H_C, R, HEAD_DIM), f32),
                        pltpu.VMEM((BB, 8, 3 * D_C), f32)],
        compiler_params=_cparams(("parallel",)),
        name="sample_mixers",
    )(*args)


def _repack_kernel(win_ref, main_ref, gate_ref):
    _repack_slab(win_ref, main_ref, gate_ref)


def _repack_w_in(w_in_t, layer):
    _, n_in, dm = w_in_t.shape
    return pl.pallas_call(
        _repack_kernel,
        grid=(dm // REPACK_LANES,),
        in_specs=[pl.BlockSpec((None, n_in, REPACK_LANES), lambda i: (layer, 0, i))],
        out_specs=[pl.BlockSpec((N_PROJ, REPACK_LANES), lambda i: (0, i)),
                   pl.BlockSpec((128, REPACK_LANES), lambda i: (0, i))],
        out_shape=[jax.ShapeDtypeStruct((N_PROJ, dm), bf16), jax.ShapeDtypeStruct((128, dm), bf16)],
        compiler_params=_cparams(("parallel",)),
        name="repack_w_in",
    )(w_in_t)


def _gate_params(gate_bias, dt_bias, a_log):
    depth = gate_bias.shape[0]
    gp = jnp.zeros((depth, 8, 128), f32)
    gp = gp.at[:, 0, LANE_I:LANE_I + 2 * H_A].set(gate_bias)
    gp = gp.at[:, 0, LANE_DECAY:LANE_DECAY + H_C].set(dt_bias)
    gp = gp.at[:, 1, LANE_DECAY:LANE_DECAY + H_C].set(a_log)
    return gp


def kernel(x_prompt, x_sample, state_mlstm_C, state_mlstm_n, state_mlstm_m, state_ret_S, state_gdn_S,
           state_gdn_conv, norm_mix_pre, norm_mix_post, norm_mlp_pre, norm_mlp_post, w_in, mlstm_gate_bias,
           gdn_conv_w, gdn_A_log, gdn_dt_bias, norm_mlstm, norm_ret, norm_gdn, w_out, w_up, w_down):
    bp, tp, d = x_prompt.shape
    bs, ts, _ = x_sample.shape
    depth = w_in.shape[0]
    rows_p = bp * tp
    rows_s = bs * ts
    rows = rows_p + rows_s

    x = jnp.concatenate([x_prompt.reshape(rows_p, d), x_sample.reshape(rows_s, d)], axis=0)
    w_in_t = jnp.swapaxes(w_in, 1, 2)
    w_in_p, w_gate_p = _repack_w_in(w_in_t, 0)
    gp = _gate_params(mlstm_gate_bias, gdn_dt_bias, gdn_A_log)
    na = norm_mlstm.reshape(depth, H_A, HEAD_DIM)
    nb = norm_ret.reshape(depth, H_B, HEAD_DIM)
    nc = norm_gdn.reshape(depth, 1, HEAD_DIM)

    nrep = jnp.repeat(state_mlstm_n.reshape(depth, bs, D_A), ts, axis=1)
    mrep = jnp.pad(jnp.repeat(state_mlstm_m, ts, axis=1), ((0, 0), (0, 0), (0, 128 - H_A)))
    s_states = (state_mlstm_C, nrep, mrep, state_ret_S, state_gdn_S, state_gdn_conv)

    p_tables = _prompt_tables(tp, PROMPT_CHUNK)
    s_tables = _sample_tables(ts, SAMPLE_BLOCK)

    tm = rows // 8
    h = _rmsnorm_rows(x, norm_mix_pre[0], tm // 2)
    p_states = []
    s_out = None
    for l in range(depth):
        proj, w_out_b = _matmul_in(h, w_in_p, w_out, l, tm, 2048)
        gates = _matmul_gates(h, w_gate_p, tm)
        outs = _prompt_mixers(proj, gates, p_tables, gp[l], gdn_conv_w[l], na[l], nb[l], nc[l], bp, tp, rows,
                              PROMPT_CHUNK)
        p_states.append(outs[1:])
        res = _sample_mixers(l, proj, gates, outs[0], s_tables, gp[l], gdn_conv_w[l], na[l], nb[l], nc[l],
                             s_states, s_out, bs, ts, rows_p, SAMPLE_BLOCK)
        y, s_out = res[0], res[1:]
        x, h = _matmul_out(y, w_out_b, x, norm_mix_post[l], norm_mlp_pre[l], tm // 2)
        up = _matmul_up(h, w_up, w_down, w_in_t, l, tm, 1024)
        u, w_down_b = up[0], up[1]
        if l + 1 < depth:
            w_in_p, w_gate_p = up[2], up[3]
        x, h = _matmul_down(u, w_down_b, x, norm_mlp_post[l], norm_mix_pre[(l + 1) % depth], tm, 1024)

    stk = lambda j: jnp.stack([s[j] for s in p_states], axis=0)
    sc, sn, sm, sr, sg, sconv = s_out
    return (
        x[:rows_p].reshape(bp, tp, d), x[rows_p:].reshape(bs, ts, d),
        stk(0), stk(1), stk(2)[:, :, 0, :H_A], stk(3), stk(4), stk(5),
        sc, sn[:, ts - 1::ts, :].reshape(depth, bs, H_A, HEAD_DIM), sm[:, ts - 1::ts, :H_A], sr, sg,
        sconv,
    )
```

```python
import functools
import math
import types

import numpy as np
import jax
import jax.numpy as jnp
from jax import lax
from jax.experimental import pallas as pl
from jax.experimental.pallas import tpu as pltpu

f32 = jnp.float32
bf16 = jnp.bfloat16

D_MODEL = 2048
HEAD_DIM = 128
H_A, H_B, H_C = 4, 4, 8
D_A, D_B, D_C = H_A * HEAD_DIM, H_B * HEAD_DIM, H_C * HEAD_DIM
D_FF = 4 * D_MODEL
CONV_W = 4
PAST_LEN = 16384
ROPE_BASE = 10000.0
GATE_SOFTCAP = 15.0
NORM_EPS = 1e-6
QK_SCALE = HEAD_DIM ** -0.5

COL_A = 0
COL_B = 4 * D_A
COL_C = COL_B + 4 * D_B
N_PROJ = COL_C + 4 * D_C
LANE_I, LANE_F, LANE_DECAY, LANE_BETA = 0, H_A, 2 * H_A, 2 * H_A + H_C

PROMPT_CHUNK = 64
PROMPT_SEQS = 2
INV_BLOCK = 8
SAMPLE_BLOCK = 8
NEG_BIG = -1e30
VMEM_LIMIT = 60 * 1024 * 1024


def _cparams(sem):
    return pltpu.CompilerParams(dimension_semantics=sem, vmem_limit_bytes=VMEM_LIMIT)


def _rms(x, g):
    return x * lax.rsqrt(jnp.mean(x * x, axis=-1, keepdims=True) + NORM_EPS) * g


def _rmsnorm_kernel(x_ref, g_ref, o_ref):
    o_ref[...] = _rms(x_ref[...], g_ref[...]).astype(o_ref.dtype)


def _rmsnorm_rows(x, g, tm):
    m, d = x.shape
    return pl.pallas_call(
        _rmsnorm_kernel,
        grid=(m // tm,),
        in_specs=[pl.BlockSpec((tm, d), lambda i: (i, 0)), pl.BlockSpec((1, d), lambda i: (0, 0))],
        out_specs=pl.BlockSpec((tm, d), lambda i: (i, 0)),
        out_shape=jax.ShapeDtypeStruct((m, d), bf16),
        compiler_params=_cparams(("parallel",)),
        name="rmsnorm_in",
    )(x, g.reshape(1, d))


def _dot_nt(a, w):
    return lax.dot_general(a, w, (((1,), (1,)), ((), ())), preferred_element_type=f32)


def _mm_gates_kernel(a_ref, w_ref, o_ref):
    o_ref[...] = _dot_nt(a_ref[...], w_ref[...])


def _matmul_gates(h, w_t, tm):
    m, k = h.shape
    n = w_t.shape[0]
    return pl.pallas_call(
        _mm_gates_kernel,
        grid=(m // tm,),
        in_specs=[pl.BlockSpec((tm, k), lambda i: (i, 0)), pl.BlockSpec((n, k), lambda i: (0, 0))],
        out_specs=pl.BlockSpec((tm, n), lambda i: (i, 0)),
        out_shape=jax.ShapeDtypeStruct((m, n), f32),
        compiler_params=_cparams(("parallel",)),
        name="proj_gates",
    )(h, w_t)


def _mm_in_kernel(a_ref, w_ref, wo_ref, o_ref, wob_ref):
    o_ref[...] = _dot_nt(a_ref[...], w_ref[...])
    wob_ref[...] = wo_ref[...].astype(bf16)


def _matmul_in(h, w_t, w_out, layer, tm, tn):
    m, k = h.shape
    n = w_t.shape[0]
    nj, ni = n // tn, m // tm
    d = w_out.shape[1]
    slab = d // (nj * ni)
    step = lambda j, i: j * ni + i
    return pl.pallas_call(
        _mm_in_kernel,
        grid=(nj, ni),
        in_specs=[pl.BlockSpec((tm, k), lambda j, i: (i, 0)), pl.BlockSpec((tn, k), lambda j, i: (j, 0)),
                  pl.BlockSpec((None, slab, d), lambda j, i: (layer, step(j, i), 0))],
        out_specs=[pl.BlockSpec((tm, tn), lambda j, i: (i, j)),
                   pl.BlockSpec((slab, d), lambda j, i: (step(j, i), 0))],
        out_shape=[jax.ShapeDtypeStruct((m, n), f32), jax.ShapeDtypeStruct((d, d), bf16)],
        compiler_params=_cparams(("parallel", "arbitrary")),
        name="proj_in",
    )(h, w_t, w_out)


REPACK_LANES = 128


def _repack_slab(win_ref, main_ref, gate_ref):
    n_a = 4 * D_A
    n_ag = n_a + 2 * H_A
    n_bc = 4 * D_B + 4 * D_C
    main_ref[0:n_a, :] = win_ref[0:n_a, :].astype(bf16)
    main_ref[n_a:n_a + n_bc, :] = win_ref[n_ag:n_ag + n_bc, :].astype(bf16)
    gate_ref[...] = jnp.zeros_like(gate_ref)
    gate_ref[LANE_I:LANE_I + 2 * H_A, :] = win_ref[n_a:n_ag, :].astype(bf16)
    gate_ref[LANE_DECAY:LANE_DECAY + 2 * H_C, :] = win_ref[n_ag + n_bc:n_ag + n_bc + 2 * H_C, :].astype(bf16)


def _mm_up_kernel(repack_every, a_ref, w_ref, wd_ref, *rest):
    if repack_every:
        win_ref, o_ref, wdb_ref, main_ref, gate_ref, wb_ref = rest
        step = pl.program_id(0) * pl.num_programs(1) + pl.program_id(1)

        @pl.when(step % repack_every == 0)
        def _():
            _repack_slab(win_ref, main_ref, gate_ref)
    else:
        o_ref, wdb_ref, wb_ref = rest

    @pl.when(pl.program_id(1) == 0)
    def _():
        wb_ref[...] = w_ref[...].astype(bf16)

    z = jnp.maximum(jnp.dot(a_ref[...], wb_ref[...], preferred_element_type=f32), 0.0)
    o_ref[...] = (z * z).astype(o_ref.dtype)
    wdb_ref[...] = wd_ref[...].astype(bf16)


def _matmul_up(h, w, w_down, w_in_t, layer, tm, tn):
    m, k = h.shape
    n = w.shape[2]
    nj, ni = n // tn, m // tm
    steps = nj * ni
    step = lambda j, i: j * ni + i
    with_repack = layer + 1 < w_in_t.shape[0]
    dslab = w_down.shape[1] // steps
    d = w_down.shape[2]
    in_specs = [pl.BlockSpec((tm, k), lambda j, i: (i, 0)), pl.BlockSpec((None, k, tn), lambda j, i: (layer, 0, j)),
                pl.BlockSpec((None, dslab, d), lambda j, i: (layer, step(j, i), 0))]
    out_specs = [pl.BlockSpec((tm, tn), lambda j, i: (i, j)), pl.BlockSpec((dslab, d), lambda j, i: (step(j, i), 0))]
    out_shape = [jax.ShapeDtypeStruct((m, n), bf16), jax.ShapeDtypeStruct(w_down.shape[1:], bf16)]
    args = [h, w, w_down]
    every = 0
    if with_repack:
        n_in, dm = w_in_t.shape[1:]
        every = steps // (dm // REPACK_LANES)
        slab = lambda j, i: step(j, i) // every
        in_specs.append(pl.BlockSpec((None, n_in, REPACK_LANES), lambda j, i: (layer + 1, 0, slab(j, i))))
        out_specs += [pl.BlockSpec((N_PROJ, REPACK_LANES), lambda j, i: (0, slab(j, i))),
                      pl.BlockSpec((128, REPACK_LANES), lambda j, i: (0, slab(j, i)))]
        out_shape += [jax.ShapeDtypeStruct((N_PROJ, dm), bf16), jax.ShapeDtypeStruct((128, dm), bf16)]
        args.append(w_in_t)
    return pl.pallas_call(
        functools.partial(_mm_up_kernel, every),
        grid=(nj, ni),
        in_specs=in_specs,
        out_specs=out_specs,
        out_shape=out_shape,
        scratch_shapes=[pltpu.VMEM((k, tn), bf16)],
        compiler_params=_cparams(("parallel", "arbitrary")),
        name="mlp_up",
    )(*args)


def _residual_epilogue(z, x_ref, gpost_ref, gnext_ref, xo_ref, ho_ref):
    x_new = x_ref[...] + _rms(z, gpost_ref[...])
    xo_ref[...] = x_new
    ho_ref[...] = _rms(x_new, gnext_ref[...]).astype(ho_ref.dtype)


def _mm_out_kernel(n_first, a_ref, b_ref, w_ref, x_ref, gpost_ref, gnext_ref, xo_ref, ho_ref):
    a = jnp.where(pl.program_id(0) < n_first, a_ref[...], b_ref[...])
    z = jnp.dot(a, w_ref[...], preferred_element_type=f32)
    _residual_epilogue(z, x_ref, gpost_ref, gnext_ref, xo_ref, ho_ref)


def _matmul_out(y_p, y_s, w, x, g_post, g_next, tm):
    k = y_p.shape[1]
    n_p, n_s = y_p.shape[0] // tm, y_s.shape[0] // tm
    m = (n_p + n_s) * tm
    d = w.shape[1]
    row = lambda i: (i, 0)
    fixed = lambda i: (0, 0)
    return pl.pallas_call(
        functools.partial(_mm_out_kernel, n_p),
        grid=(n_p + n_s,),
        in_specs=[pl.BlockSpec((tm, k), lambda i: (jnp.minimum(i, n_p - 1), 0)),
                  pl.BlockSpec((tm, k), lambda i: (jnp.maximum(i - n_p, 0), 0)),
                  pl.BlockSpec((k, d), fixed), pl.BlockSpec((tm, d), row),
                  pl.BlockSpec((1, d), fixed), pl.BlockSpec((1, d), fixed)],
        out_specs=[pl.BlockSpec((tm, d), row), pl.BlockSpec((tm, d), row)],
        out_shape=[jax.ShapeDtypeStruct((m, d), f32), jax.ShapeDtypeStruct((m, d), bf16)],
        compiler_params=_cparams(("parallel",)),
        name="proj_out",
    )(y_p, y_s, w, x, g_post.reshape(1, d), g_next.reshape(1, d))


def _mm_down_kernel(a_ref, w_ref, x_ref, gpost_ref, gnext_ref, xo_ref, ho_ref):
    kk = pl.program_id(1)

    @pl.when(kk == 0)
    def _():
        xo_ref[...] = jnp.zeros_like(xo_ref)

    xo_ref[...] += jnp.dot(a_ref[...], w_ref[...], preferred_element_type=f32)

    @pl.when(kk == pl.num_programs(1) - 1)
    def _():
        _residual_epilogue(xo_ref[...], x_ref, gpost_ref, gnext_ref, xo_ref, ho_ref)


def _matmul_down(u, w, x, g_post, g_next, tm, tk):
    m, k = u.shape
    d = w.shape[1]
    row = lambda i, kk: (i, 0)
    fixed = lambda i, kk: (0, 0)
    return pl.pallas_call(
        _mm_down_kernel,
        grid=(m // tm, k // tk),
        in_specs=[pl.BlockSpec((tm, tk), lambda i, kk: (i, kk)),
                  pl.BlockSpec((tk, d), lambda i, kk: (kk, 0)),
                  pl.BlockSpec((tm, d), row), pl.BlockSpec((1, d), fixed), pl.BlockSpec((1, d), fixed)],
        out_specs=[pl.BlockSpec((tm, d), row), pl.BlockSpec((tm, d), row)],
        out_shape=[jax.ShapeDtypeStruct((m, d), f32), jax.ShapeDtypeStruct((m, d), bf16)],
        compiler_params=_cparams(("parallel", "arbitrary")),
        name="mlp_down",
    )(u, w, x, g_post.reshape(1, d), g_next.reshape(1, d))


def _dot(a, b):
    return jnp.dot(a, b, preferred_element_type=f32)


def _dot_tb(a, b):
    return lax.dot_general(a, b, (((1,), (1,)), ((), ())), preferred_element_type=f32)


def _dot_ta(a, b):
    return lax.dot_general(a, b, (((0,), (0,)), ((), ())), preferred_element_type=f32)


def _lane_sum(x):
    return jnp.sum(x, axis=-1, keepdims=True)


def _softplus(z):
    return jnp.maximum(z, 0.0) + jnp.log1p(jnp.exp(-jnp.abs(z)))


def _gate_transform(gt, gp):
    z = gt + gp[0:1, :]
    capped = GATE_SOFTCAP * jnp.tanh(z / GATE_SOFTCAP)
    log_f = -_softplus(-capped)
    decay = -jnp.exp(gp[1:2, :]) * _softplus(z)
    beta = jax.nn.sigmoid(gt)
    return capped, log_f, decay, beta


def _head_norm(x, gain, center):
    if center:
        x = x - jnp.mean(x, axis=-1, keepdims=True)
    return x * lax.rsqrt(jnp.mean(x * x, axis=-1, keepdims=True) + NORM_EPS) * gain


def _l2norm(x):
    return x * lax.rsqrt(_lane_sum(x * x) + NORM_EPS)


def _silu(x):
    return x * jax.nn.sigmoid(x)


def _rotary(x, cos, sin_signed):
    return x * cos + pltpu.roll(x, HEAD_DIM // 2, 1) * sin_signed


def _retention_log_gamma():
    return np.log1p(-np.exp2(-5.0 - np.arange(H_B, dtype=np.float64)))


def _rope_tables(pos):
    half = HEAD_DIM // 2
    inv = ROPE_BASE ** (-np.arange(half, dtype=np.float64) / half)
    ang = np.asarray(pos, dtype=np.float64)[:, None] * inv[None, :]
    cos = np.concatenate([np.cos(ang), np.cos(ang)], axis=-1)
    sin = np.concatenate([-np.sin(ang), np.sin(ang)], axis=-1)
    return cos.astype(np.float32), sin.astype(np.float32)


def _scan_rows(x, row, length, op, fill):
    s = 1
    while s < length:
        x = op(x, jnp.where(row >= s, pltpu.roll(x, s, 0), fill))
        s *= 2
    return x


def _rows_to_lanes(x, length):
    if length < 128:
        x = jnp.concatenate([x, jnp.zeros((128 - length, 128), x.dtype)], axis=0)
    return x.T[:, :length]


def _prompt_mixer_kernel(L, NS, *refs):
    proj_refs, gt_refs = refs[:NS], refs[NS:2 * NS]
    (cos_ref, sin_ref, rd_ref, rq_ref, rk_ref, gp_ref, cw_ref, na_ref, nb_ref, nc_ref,
     y_ref, c_ref, n_ref, m_ref, r_ref, g_ref, conv_ref, ext_ref) = refs[2 * NS:]
    c = pl.program_id(1)

    @pl.when(c == 0)
    def _():
        c_ref[...] = jnp.zeros_like(c_ref)
        n_ref[...] = jnp.zeros_like(n_ref)
        m_ref[...] = jnp.zeros_like(m_ref)
        r_ref[...] = jnp.zeros_like(r_ref)
        g_ref[...] = jnp.zeros_like(g_ref)
        ext_ref[:, 0:8, :] = jnp.zeros((NS, 8, 3 * D_C), f32)

    row = lax.broadcasted_iota(jnp.int32, (L, 128), 0)
    lane = lax.broadcasted_iota(jnp.int32, (L, 128), 1)
    ri = lax.broadcasted_iota(jnp.int32, (L, L), 0)
    ci = lax.broadcasted_iota(jnp.int32, (L, L), 1)
    incl = ri >= ci
    strict = ri > ci
    head_lane = lane < H_A
    seqs = range(NS)

    def gate_block(s):
        g = types.SimpleNamespace()
        capped, log_f, decay, g.beta = _gate_transform(gt_refs[s][...], gp_ref[...])
        ig = jnp.where(head_lane, capped, 0.0)
        lf = jnp.where(head_lane, pltpu.roll(log_f, 128 - LANE_F, 1), 0.0)
        mp = m_ref[s]
        F = _scan_rows(lf, row, L, jnp.add, 0.0)
        m = F + jnp.maximum(mp, _scan_rows(ig - F, row, L, jnp.maximum, NEG_BIG))
        m_new = m[L - 1:L, :]
        f_last = F[L - 1:L, :]
        g.a_rows = F - m
        g.b_lanes = _rows_to_lanes(ig - F, L)
        g.inter = jnp.exp(F + mp - m)
        g.inv_floor = jnp.exp(-m)
        g.wl = jnp.exp(ig + f_last - F - m_new)
        g.dec = jnp.exp(f_last + mp - m_new)
        m_ref[s] = m_new
        g.G = _scan_rows(decay, row, L, jnp.add, 0.0)
        g.g_lanes = _rows_to_lanes(g.G, L)
        g.gam = jnp.exp(g.G)
        g_last = g.G[L - 1:L, :]
        g.k_decay = jnp.exp(g_last - g.G)
        g.s_decay = jnp.exp(g_last)
        return g

    gs = [gate_block(s) for s in seqs]

    def mlstm_head(s, h):
        proj_ref, g = proj_refs[s], gs[s]
        q = proj_ref[:, COL_A + h * HEAD_DIM:COL_A + (h + 1) * HEAD_DIM]
        k = proj_ref[:, COL_A + D_A + h * HEAD_DIM:COL_A + D_A + (h + 1) * HEAD_DIM] * QK_SCALE
        v = proj_ref[:, COL_A + 2 * D_A + h * HEAD_DIM:COL_A + 2 * D_A + (h + 1) * HEAD_DIM]
        og = proj_ref[:, COL_A + 3 * D_A + h * HEAD_DIM:COL_A + 3 * D_A + (h + 1) * HEAD_DIM]
        logw = g.a_rows[:, h:h + 1] + g.b_lanes[h:h + 1, :]
        sc = _dot_tb(q, k) * jnp.exp(jnp.where(incl, logw, NEG_BIG))
        c_old = c_ref[s, h]
        n_old = n_ref[s, h:h + 1, :]
        inter_h = g.inter[:, h:h + 1]
        num = inter_h * _dot(q, c_old) + _dot(sc, v)
        den = inter_h * _lane_sum(q * n_old) + _lane_sum(sc)
        hh = num / jnp.maximum(jnp.abs(den), g.inv_floor[:, h:h + 1])
        kw = k * g.wl[:, h:h + 1]
        dec_h = g.dec[:, h:h + 1]
        c_ref[s, h] = dec_h * c_old + _dot_ta(kw, v)
        n_ref[s, h:h + 1, :] = dec_h * n_old + jnp.sum(kw, axis=0, keepdims=True)
        ya = jax.nn.sigmoid(og) * _head_norm(hh, na_ref[h:h + 1, :], False)
        y_ref[s, :, h * HEAD_DIM:(h + 1) * HEAD_DIM] = ya.astype(y_ref.dtype)

    lg = _retention_log_gamma()

    def retention_head(s, h):
        proj_ref = proj_refs[s]
        cos = cos_ref[...]
        sin = sin_ref[...]
        q = proj_ref[:, COL_B + h * HEAD_DIM:COL_B + (h + 1) * HEAD_DIM]
        k = proj_ref[:, COL_B + D_B + h * HEAD_DIM:COL_B + D_B + (h + 1) * HEAD_DIM]
        v = proj_ref[:, COL_B + 2 * D_B + h * HEAD_DIM:COL_B + 2 * D_B + (h + 1) * HEAD_DIM]
        gb = proj_ref[:, COL_B + 3 * D_B + h * HEAD_DIM:COL_B + 3 * D_B + (h + 1) * HEAD_DIM]
        qr = _rotary(q, cos, sin)
        kr = _rotary(k, cos, sin) * QK_SCALE
        s_old = r_ref[s, h]
        inner = _dot_tb(qr, kr) * rd_ref[h]
        o = _dot(inner, v) + rq_ref[h] * _dot(qr, s_old)
        r_ref[s, h] = float(np.exp(L * lg[h])) * s_old + _dot_ta(kr * rk_ref[h], v)
        yb = _silu(gb) * _head_norm(o, nb_ref[h:h + 1, :], True)
        y_ref[s, :, D_A + h * HEAD_DIM:D_A + (h + 1) * HEAD_DIM] = yb.astype(y_ref.dtype)

    fillers = [functools.partial(mlstm_head, s, h) for h in range(H_A) for s in seqs]
    fillers += [functools.partial(retention_head, s, h) for h in range(H_B) for s in seqs]
    n_stages = max(int(math.log2(min(INV_BLOCK, L))) - 1, 0) + int(math.log2(L // min(INV_BLOCK, L)))
    per_stage = -(-len(fillers) // max(n_stages, 1))

    def emit_fillers(count):
        for _ in range(min(count, len(fillers))):
            fillers.pop(0)()

    for s in seqs:
        ext_ref[s, 8:8 + L, :] = proj_refs[s][:, COL_C:COL_C + 3 * D_C]

    def conv_block(s, col):
        acc = cw_ref[CONV_W - 1:CONV_W, col:col + HEAD_DIM] * ext_ref[s, 8:8 + L, col:col + HEAD_DIM]
        for w in range(CONV_W - 1):
            off = 8 - (CONV_W - 1) + w
            acc = acc + cw_ref[w:w + 1, col:col + HEAD_DIM] * ext_ref[s, off:off + L, col:col + HEAD_DIM]
        return _silu(acc)

    pairs = [(s, h) for h in range(H_C) for s in seqs]
    heads = range(len(pairs))
    dcol = [LANE_DECAY + h for _, h in pairs]
    gam = [gs[s].gam for s, _ in pairs]
    beta_c = [gs[s].beta[:, LANE_BETA + h:LANE_BETA + h + 1] for s, h in pairs]
    q = [_l2norm(conv_block(s, h * HEAD_DIM)) * QK_SCALE for s, h in pairs]
    k = [_l2norm(conv_block(s, D_C + h * HEAD_DIM)) for s, h in pairs]
    v = [conv_block(s, 2 * D_C + h * HEAD_DIM) for s, h in pairs]
    dec_in = [jnp.exp(jnp.where(incl, gs[s].G[:, LANE_DECAY + h:LANE_DECAY + h + 1]
                                - gs[s].g_lanes[LANE_DECAY + h:LANE_DECAY + h + 1, :], NEG_BIG)) for s, h in pairs]
    a_mat = [jnp.where(strict, dec_in[h], 0.0) * beta_c[h] * _dot_tb(k[h], k[h]) for h in heads]
    qk = [_dot_tb(q[h], k[h]) * dec_in[h] for h in heads]
    blk = min(INV_BLOCK, L)
    same = (ri // blk) == (ci // blk)
    pw = [jnp.where(same, -a, 0.0) for a in a_mat]
    e_mat = list(pw)
    span = 1
    while 2 * span < blk:
        pw = [_dot(p, p) for p in pw]
        e_mat = [e_mat[h] + pw[h] + _dot(e_mat[h], pw[h]) for h in heads]
        emit_fillers(per_stage)
        span *= 2
    while blk < L:
        wider = (ri // (2 * blk)) == (ci // (2 * blk))
        a_off = [jnp.where(wider & jnp.logical_not(same), a, 0.0) for a in a_mat]
        low = [a_off[h] + _dot(e_mat[h], a_off[h]) for h in heads]
        e_mat = [e_mat[h] - low[h] - _dot(low[h], e_mat[h]) for h in heads]
        emit_fillers(per_stage)
        same = wider
        blk *= 2
    rhs = [jnp.concatenate([(beta_c[h] * gam[h][:, dcol[h]:dcol[h] + 1]) * k[h], beta_c[h] * v[h]], axis=1)
           for h in heads]
    sol = [rhs[h] + _dot(e_mat[h], rhs[h]) for h in heads]
    emit_fillers(len(fillers))
    s_old = [g_ref[s, hd] for s, hd in pairs]
    both = [_dot(jnp.concatenate([sol[h][:, :HEAD_DIM], q[h]], axis=0), s_old[h]) for h in heads]
    u = [sol[h][:, HEAD_DIM:] - both[h][:L] for h in heads]
    o = [gam[h][:, dcol[h]:dcol[h] + 1] * both[h][L:] + _dot(qk[h], u[h]) for h in heads]
    for h, (s, hd) in enumerate(pairs):
        g_ref[s, hd] = (gs[s].s_decay[:, dcol[h]:dcol[h] + 1] * s_old[h]
                        + _dot_ta(k[h] * gs[s].k_decay[:, dcol[h]:dcol[h] + 1], u[h]))
    for h, (s, hd) in enumerate(pairs):
        gz = proj_refs[s][:, COL_C + 3 * D_C + hd * HEAD_DIM:COL_C + 3 * D_C + (hd + 1) * HEAD_DIM]
        yc = _silu(gz) * _head_norm(o[h], nc_ref[...], False)
        y_ref[s, :, D_A + D_B + hd * HEAD_DIM:D_A + D_B + (hd + 1) * HEAD_DIM] = yc.astype(y_ref.dtype)

    for s in seqs:
        ext_ref[s, 0:8, :] = ext_ref[s, L:L + 8, :]

    @pl.when(c == pl.num_programs(1) - 1)
    def _():
        for s in seqs:
            conv_ref[s] = ext_ref[s, 8 + L - (CONV_W - 1):8 + L, :]


def _prompt_tables(seq, L):
    lg = _retention_log_gamma()
    i = np.arange(L, dtype=np.float64)
    diff = i[:, None] - i[None, :]
    rd = np.where(diff >= 0, np.exp(np.maximum(diff, 0.0) * lg[:, None, None]), 0.0)
    rq = np.broadcast_to(np.exp((i + 1.0) * lg[:, None])[..., None], (H_B, L, HEAD_DIM))
    rk = np.broadcast_to(np.exp((L - 1.0 - i) * lg[:, None])[..., None], (H_B, L, HEAD_DIM))
    cos, sin = _rope_tables(np.arange(seq))
    return (jnp.asarray(cos), jnp.asarray(sin), jnp.asarray(rd, f32), jnp.asarray(rq, f32), jnp.asarray(rk, f32))


def _prompt_mixers(proj, gates, tables, gp, cw, na, nb, nc, batch, seq, L, NS):
    nchunk = seq // L
    cos, sin, rd, rq, rk = tables
    full = lambda shape: pl.BlockSpec(shape, lambda b, c: (0,) * len(shape))
    state4 = lambda heads: pl.BlockSpec((NS, heads, HEAD_DIM, HEAD_DIM), lambda b, c: (b, 0, 0, 0))
    rows_of = lambda s, width: pl.BlockSpec((L, width), lambda b, c: ((b * NS + s) * nchunk + c, 0))
    outs = pl.pallas_call(
        functools.partial(_prompt_mixer_kernel, L, NS),
        grid=(batch // NS, nchunk),
        in_specs=[rows_of(s, N_PROJ) for s in range(NS)] + [rows_of(s, 128) for s in range(NS)] + [
            pl.BlockSpec((L, HEAD_DIM), lambda b, c: (c, 0)),
            pl.BlockSpec((L, HEAD_DIM), lambda b, c: (c, 0)),
            full((H_B, L, L)), full((H_B, L, HEAD_DIM)), full((H_B, L, HEAD_DIM)),
            full((8, 128)), full((CONV_W, 3 * D_C)),
            full((H_A, HEAD_DIM)), full((H_B, HEAD_DIM)), full((1, HEAD_DIM)),
        ],
        out_specs=[
            pl.BlockSpec((None, NS, None, L, D_MODEL), lambda b, c: (b, 0, c, 0, 0)),
            state4(H_A),
            pl.BlockSpec((NS, H_A, HEAD_DIM), lambda b, c: (b, 0, 0)),
            pl.BlockSpec((NS, 1, 128), lambda b, c: (b, 0, 0)),
            state4(H_B),
            state4(H_C),
            pl.BlockSpec((NS, CONV_W - 1, 3 * D_C), lambda b, c: (b, 0, 0)),
        ],
        out_shape=[
            jax.ShapeDtypeStruct((batch // NS, NS, nchunk, L, D_MODEL), bf16),
            jax.ShapeDtypeStruct((batch, H_A, HEAD_DIM, HEAD_DIM), f32),
            jax.ShapeDtypeStruct((batch, H_A, HEAD_DIM), f32),
            jax.ShapeDtypeStruct((batch, 1, 128), f32),
            jax.ShapeDtypeStruct((batch, H_B, HEAD_DIM, HEAD_DIM), f32),
            jax.ShapeDtypeStruct((batch, H_C, HEAD_DIM, HEAD_DIM), f32),
            jax.ShapeDtypeStruct((batch, CONV_W - 1, 3 * D_C), f32),
        ],
        scratch_shapes=[pltpu.VMEM((NS, L + 8, 3 * D_C), f32)],
        compiler_params=_cparams(("parallel", "arbitrary")),
        name="prompt_mixers",
    )(*([proj] * NS + [gates] * NS), cos, sin, rd, rq, rk, gp, cw, na, nb, nc)
    return [outs[0].reshape(batch * seq, D_MODEL)] + list(outs[1:])


def _sample_mixer_kernel(T, BB, layer_first, *refs):
    (proj_ref, gt_ref, cos_ref, sin_ref, rt_ref, gp_ref, cw_ref, na_ref, nb_ref, nc_ref,
     c0_ref, nrep_ref, mrep_ref, r0_ref, g0_ref, conv0_ref) = refs[:16]
    rest = refs[16 + (0 if layer_first else 6):]
    (y_ref, c_ref, nout_ref, mout_ref, r_ref, g_ref, conv_ref, cv_ref, sa_ref, ext_ref) = rest
    R = T * BB
    row = lax.broadcasted_iota(jnp.int32, (R, 128), 0)
    lane = lax.broadcasted_iota(jnp.int32, (R, 128), 1)
    t = lax.rem(row, T)
    tcol = t[:, 0:1]

    def shift(x, s):
        return x if s == 0 else pltpu.roll(x, s, 0)

    def seg_scan(x, op, fill):
        s = 1
        while s < T:
            x = op(x, jnp.where(t >= s, shift(x, s), fill))
            s *= 2
        return x

    def last_rep(x):
        x_last = jnp.where(t == T - 1, x, 0.0)
        out = x_last
        for s in range(1, T):
            out = out + pltpu.roll(x_last, R - s, 0)
        return out

    capped, log_f, decay, beta = _gate_transform(gt_ref[...], gp_ref[...])

    head_lane = lane < H_A
    ig = jnp.where(head_lane, capped, 0.0)
    lf = jnp.where(head_lane, pltpu.roll(log_f, 128 - LANE_F, 1), 0.0)
    mp = mrep_ref[...]
    F = seg_scan(lf, jnp.add, 0.0)
    m = F + jnp.maximum(mp, seg_scan(ig - F, jnp.maximum, NEG_BIG))
    m_new = last_rep(m)
    f_last = last_rep(F)
    inter = jnp.exp(F + mp - m)
    inv_floor = jnp.exp(-m)
    wl = jnp.exp(ig + f_last - F - m_new)
    dec = jnp.exp(f_last + mp - m_new)
    mout_ref[...] = m_new
    pw = [jnp.where(t >= s, jnp.exp(F - shift(F, s) + shift(ig, s) - m), 0.0) for s in range(T)]

    def seq(x, b):
        return x[b * T:(b + 1) * T]

    slot_a, slot_b, slot_w, slot_q = 0, H_A, H_A + H_B, H_A + H_B + H_C

    qa = [proj_ref[:, COL_A + h * HEAD_DIM:COL_A + (h + 1) * HEAD_DIM] for h in range(H_A)]
    for h in range(H_A):
        for b in range(BB):
            sa_ref[slot_a + h, b * T:(b + 1) * T, :] = _dot(seq(qa[h], b), c0_ref[b, h])
    cos = cos_ref[...]
    sin = sin_ref[...]
    lg = _retention_log_gamma()
    qr = [_rotary(proj_ref[:, COL_B + h * HEAD_DIM:COL_B + (h + 1) * HEAD_DIM], cos, sin) for h in range(H_B)]
    kr = [_rotary(proj_ref[:, COL_B + D_B + h * HEAD_DIM:COL_B + D_B + (h + 1) * HEAD_DIM], cos, sin) * QK_SCALE
          for h in range(H_B)]
    for h in range(H_B):
        for b in range(BB):
            sa_ref[slot_b + h, b * T:(b + 1) * T, :] = _dot(seq(qr[h], b), r0_ref[b, h])

    ka = [proj_ref[:, COL_A + D_A + h * HEAD_DIM:COL_A + D_A + (h + 1) * HEAD_DIM] * QK_SCALE for h in range(H_A)]
    va = [proj_ref[:, COL_A + 2 * D_A + h * HEAD_DIM:COL_A + 2 * D_A + (h + 1) * HEAD_DIM] for h in range(H_A)]
    n_old = [nrep_ref[:, h * HEAD_DIM:(h + 1) * HEAD_DIM] for h in range(H_A)]
    kw = [ka[h] * wl[:, h:h + 1] for h in range(H_A)]
    for h in range(H_A):
        for b in range(BB):
            dec_bh = dec[b * T:b * T + 1, h:h + 1]
            c_ref[b, h] = dec_bh * c0_ref[b, h] + _dot_ta(seq(kw[h], b), seq(va[h], b))
        nout_ref[:, h * HEAD_DIM:(h + 1) * HEAD_DIM] = dec[:, h:h + 1] * n_old[h] + seg_scan(kw[h], jnp.add, 0.0)
    vb = [proj_ref[:, COL_B + 2 * D_B + h * HEAD_DIM:COL_B + 2 * D_B + (h + 1) * HEAD_DIM] for h in range(H_B)]
    for h in range(H_B):
        kd = kr[h] * rt_ref[:, H_B + h:H_B + h + 1]
        for b in range(BB):
            r_ref[b, h] = float(np.exp(T * lg[h])) * r0_ref[b, h] + _dot_ta(seq(kd, b), seq(vb[h], b))

    for b in range(BB):
        ext_ref[b, 0:CONV_W - 1, :] = conv0_ref[b]
        ext_ref[b, CONV_W - 1:CONV_W - 1 + T, :] = proj_ref[b * T:(b + 1) * T, COL_C:COL_C + 3 * D_C]
        acc = cw_ref[0:1, :] * ext_ref[b, 0:T, :]
        for w in range(1, CONV_W):
            acc = acc + cw_ref[w:w + 1, :] * ext_ref[b, w:w + T, :]
        cv_ref[b * T:(b + 1) * T, :] = _silu(acc)
        conv_ref[b] = ext_ref[b, T:T + CONV_W - 1, :]

    G = seg_scan(decay, jnp.add, 0.0)
    gam = jnp.exp(G)
    g_last = last_rep(G)
    k_decay = jnp.exp(g_last - G)
    s_decay = jnp.exp(g_last)
    dshift = [None] + [jnp.where(t >= s, jnp.exp(G - shift(G, s)), 0.0) for s in range(1, T)]

    heads = range(H_C)
    dcol = [LANE_DECAY + h for h in heads]
    beta_c = [beta[:, LANE_BETA + h:LANE_BETA + h + 1] for h in heads]
    qg = [_l2norm(cv_ref[:, h * HEAD_DIM:(h + 1) * HEAD_DIM]) * QK_SCALE for h in heads]
    kg = [_l2norm(cv_ref[:, D_C + h * HEAD_DIM:D_C + (h + 1) * HEAD_DIM]) for h in heads]
    vg = [cv_ref[:, 2 * D_C + h * HEAD_DIM:2 * D_C + (h + 1) * HEAD_DIM] for h in heads]
    a_sub = [[None] + [beta_c[h] * dshift[s][:, dcol[h]:dcol[h] + 1] * _lane_sum(kg[h] * shift(kg[h], s))
                       for s in range(1, T)] for h in heads]
    rhs_w = [(beta_c[h] * gam[:, dcol[h]:dcol[h] + 1]) * kg[h] for h in heads]
    rhs_u = [beta_c[h] * vg[h] for h in heads]
    w_sol, u_sol = list(rhs_w), list(rhs_u)
    for i in range(1, T):
        for h in heads:
            upd_w = a_sub[h][1] * shift(w_sol[h], 1)
            upd_u = a_sub[h][1] * shift(u_sol[h], 1)
            for s in range(2, i + 1):
                upd_w = upd_w + a_sub[h][s] * shift(w_sol[h], s)
                upd_u = upd_u + a_sub[h][s] * shift(u_sol[h], s)
            w_sol[h] = jnp.where(t == i, rhs_w[h] - upd_w, w_sol[h])
            u_sol[h] = jnp.where(t == i, rhs_u[h] - upd_u, u_sol[h])
    for h in heads:
        for b in range(BB):
            both = _dot(jnp.concatenate([seq(w_sol[h], b), seq(qg[h], b)], axis=0), g0_ref[b, h])
            sa_ref[slot_w + h, b * T:(b + 1) * T, :] = both[:T]
            sa_ref[slot_q + h, b * T:(b + 1) * T, :] = both[T:]

    ug = [u_sol[h] - sa_ref[slot_w + h] for h in heads]
    for h in heads:
        kd = kg[h] * k_decay[:, dcol[h]:dcol[h] + 1]
        for b in range(BB):
            g_ref[b, h] = (s_decay[b * T:b * T + 1, dcol[h]:dcol[h] + 1] * g0_ref[b, h]
                           + _dot_ta(seq(kd, b), seq(ug[h], b)))

    sc_a = [[_lane_sum(qa[h] * shift(ka[h], s)) * pw[s][:, h:h + 1] for s in range(T)] for h in range(H_A)]
    qn = [_lane_sum(qa[h] * n_old[h]) for h in range(H_A)]
    sc_b = [[jnp.where(tcol >= s, _lane_sum(qr[h] * shift(kr[h], s)) * float(np.exp(s * lg[h])), 0.0)
             for s in range(T)] for h in range(H_B)]
    sc_c = [[_lane_sum(qg[h] * shift(kg[h], s)) * (1.0 if s == 0 else dshift[s][:, dcol[h]:dcol[h] + 1])
             for s in range(T)] for h in heads]
    hid_a, hid_b, hid_c = [], [], []
    for h in range(H_A):
        inter_h = inter[:, h:h + 1]
        num = inter_h * sa_ref[slot_a + h]
        den = inter_h * qn[h]
        for s in range(T):
            num = num + sc_a[h][s] * shift(va[h], s)
            den = den + sc_a[h][s]
        hid_a.append(num / jnp.maximum(jnp.abs(den), inv_floor[:, h:h + 1]))
    for h in range(H_B):
        o = rt_ref[:, h:h + 1] * sa_ref[slot_b + h]
        for s in range(T):
            o = o + sc_b[h][s] * shift(vb[h], s)
        hid_b.append(o)
    for h in heads:
        o = gam[:, dcol[h]:dcol[h] + 1] * sa_ref[slot_q + h]
        for s in range(T):
            o = o + sc_c[h][s] * shift(ug[h], s)
        hid_c.append(o)
    mean_b = [jnp.mean(x, axis=-1, keepdims=True) for x in hid_b]
    hid_b = [x - mu for x, mu in zip(hid_b, mean_b)]
    hidden = hid_a + hid_b + hid_c
    inv_rms = [lax.rsqrt(jnp.mean(x * x, axis=-1, keepdims=True) + NORM_EPS) for x in hidden]
    for h in range(H_A):
        og = proj_ref[:, COL_A + 3 * D_A + h * HEAD_DIM:COL_A + 3 * D_A + (h + 1) * HEAD_DIM]
        ya = jax.nn.sigmoid(og) * (hidden[h] * inv_rms[h] * na_ref[h:h + 1, :])
        y_ref[:, h * HEAD_DIM:(h + 1) * HEAD_DIM] = ya.astype(y_ref.dtype)
    for h in range(H_B):
        gb = proj_ref[:, COL_B + 3 * D_B + h * HEAD_DIM:COL_B + 3 * D_B + (h + 1) * HEAD_DIM]
        yb = _silu(gb) * (hidden[H_A + h] * inv_rms[H_A + h] * nb_ref[h:h + 1, :])
        y_ref[:, D_A + h * HEAD_DIM:D_A + (h + 1) * HEAD_DIM] = yb.astype(y_ref.dtype)
    for h in heads:
        gz = proj_ref[:, COL_C + 3 * D_C + h * HEAD_DIM:COL_C + 3 * D_C + (h + 1) * HEAD_DIM]
        yc = _silu(gz) * (hidden[H_A + H_B + h] * inv_rms[H_A + H_B + h] * nc_ref[...])
        y_ref[:, D_A + D_B + h * HEAD_DIM:D_A + D_B + (h + 1) * HEAD_DIM] = yc.astype(y_ref.dtype)


def _sample_tables(T, BB):
    lg = _retention_log_gamma()
    tt = np.arange(T, dtype=np.float64)
    rt = np.zeros((T, 128), np.float64)
    rt[:, 0:H_B] = np.exp((tt[:, None] + 1.0) * lg[None, :])
    rt[:, H_B:2 * H_B] = np.exp((T - 1.0 - tt[:, None]) * lg[None, :])
    cos, sin = _rope_tables(PAST_LEN + np.arange(T))
    rep = lambda a: jnp.asarray(np.tile(a, (BB, 1)), f32)
    return rep(cos), rep(sin), rep(rt)


def _sample_mixers(layer, proj, gates, tables, gp, cw, na, nb, nc, states, prev_out, batch, T, row0, BB):
    c0, nrep, mrep, r0, g0, conv0 = states
    R = T * BB
    nblk = batch // BB
    blk0 = row0 // R
    cos, sin, rt = tables
    full = lambda shape: pl.BlockSpec(shape, lambda i: (0,) * len(shape))
    st4 = lambda heads: pl.BlockSpec((None, BB, heads, HEAD_DIM, HEAD_DIM), lambda i: (layer, i, 0, 0, 0))
    rows = lambda width: pl.BlockSpec((None, R, width), lambda i: (layer, i, 0))
    convspec = pl.BlockSpec((None, BB, CONV_W - 1, 3 * D_C), lambda i: (layer, i, 0, 0))
    anyspec = pl.BlockSpec(memory_space=pl.ANY)
    in_specs = [
        pl.BlockSpec((R, N_PROJ), lambda i: (blk0 + i, 0)),
        pl.BlockSpec((R, 128), lambda i: (blk0 + i, 0)),
        full((R, HEAD_DIM)), full((R, HEAD_DIM)), full((R, 128)),
        full((8, 128)), full((CONV_W, 3 * D_C)),
        full((H_A, HEAD_DIM)), full((H_B, HEAD_DIM)), full((1, HEAD_DIM)),
        st4(H_A), rows(D_A), rows(128), st4(H_B), st4(H_C), convspec,
    ]
    args = [proj, gates, cos, sin, rt, gp, cw, na, nb, nc, c0, nrep, mrep, r0, g0, conv0]
    out_shape = [
        jax.ShapeDtypeStruct((batch * T, D_MODEL), bf16),
        jax.ShapeDtypeStruct(c0.shape, f32),
        jax.ShapeDtypeStruct(nrep.shape, f32),
        jax.ShapeDtypeStruct(mrep.shape, f32),
        jax.ShapeDtypeStruct(r0.shape, f32),
        jax.ShapeDtypeStruct(g0.shape, f32),
        jax.ShapeDtypeStruct(conv0.shape, f32),
    ]
    out_specs = [pl.BlockSpec((R, D_MODEL), lambda i: (i, 0)),
                 st4(H_A), rows(D_A), rows(128), st4(H_B), st4(H_C), convspec]
    first = prev_out is None
    aliases = {}
    if not first:
        in_specs += [anyspec] * 6
        args += list(prev_out)
        aliases = {16 + j: 1 + j for j in range(6)}
    return pl.pallas_call(
        functools.partial(_sample_mixer_kernel, T, BB, first),
        grid=(nblk,),
        in_specs=in_specs,
        out_specs=out_specs,
        out_shape=out_shape,
        input_output_aliases=aliases,
        scratch_shapes=[pltpu.VMEM((R, 3 * D_C), f32), pltpu.VMEM((H_A + H_B + 2 * H_C, R, HEAD_DIM), f32),
                        pltpu.VMEM((BB, 8, 3 * D_C), f32)],
        compiler_params=_cparams(("parallel",)),
        name="sample_mixers",
    )(*args)


def _repack_kernel(win_ref, main_ref, gate_ref):
    _repack_slab(win_ref, main_ref, gate_ref)


def _repack_w_in(w_in_t, layer):
    _, n_in, dm = w_in_t.shape
    return pl.pallas_call(
        _repack_kernel,
        grid=(dm // REPACK_LANES,),
        in_specs=[pl.BlockSpec((None, n_in, REPACK_LANES), lambda i: (layer, 0, i))],
        out_specs=[pl.BlockSpec((N_PROJ, REPACK_LANES), lambda i: (0, i)),
                   pl.BlockSpec((128, REPACK_LANES), lambda i: (0, i))],
        out_shape=[jax.ShapeDtypeStruct((N_PROJ, dm), bf16), jax.ShapeDtypeStruct((128, dm), bf16)],
        compiler_params=_cparams(("parallel",)),
        name="repack_w_in",
    )(w_in_t)


def _gate_params(gate_bias, dt_bias, a_log):
    depth = gate_bias.shape[0]
    gp = jnp.zeros((depth, 8, 128), f32)
    gp = gp.at[:, 0, LANE_I:LANE_I + 2 * H_A].set(gate_bias)
    gp = gp.at[:, 0, LANE_DECAY:LANE_DECAY + H_C].set(dt_bias)
    gp = gp.at[:, 1, LANE_DECAY:LANE_DECAY + H_C].set(a_log)
    return gp


def kernel(x_prompt, x_sample, state_mlstm_C, state_mlstm_n, state_mlstm_m, state_ret_S, state_gdn_S,
           state_gdn_conv, norm_mix_pre, norm_mix_post, norm_mlp_pre, norm_mlp_post, w_in, mlstm_gate_bias,
           gdn_conv_w, gdn_A_log, gdn_dt_bias, norm_mlstm, norm_ret, norm_gdn, w_out, w_up, w_down):
    bp, tp, d = x_prompt.shape
    bs, ts, _ = x_sample.shape
    depth = w_in.shape[0]
    rows_p = bp * tp
    rows_s = bs * ts
    rows = rows_p + rows_s

    x = jnp.concatenate([x_prompt.reshape(rows_p, d), x_sample.reshape(rows_s, d)], axis=0)
    w_in_t = jnp.swapaxes(w_in, 1, 2)
    w_in_p, w_gate_p = _repack_w_in(w_in_t, 0)
    gp = _gate_params(mlstm_gate_bias, gdn_dt_bias, gdn_A_log)
    na = norm_mlstm.reshape(depth, H_A, HEAD_DIM)
    nb = norm_ret.reshape(depth, H_B, HEAD_DIM)
    nc = norm_gdn.reshape(depth, 1, HEAD_DIM)

    nrep = jnp.repeat(state_mlstm_n.reshape(depth, bs, D_A), ts, axis=1)
    mrep = jnp.pad(jnp.repeat(state_mlstm_m, ts, axis=1), ((0, 0), (0, 0), (0, 128 - H_A)))
    s_states = (state_mlstm_C, nrep, mrep, state_ret_S, state_gdn_S, state_gdn_conv)

    p_tables = _prompt_tables(tp, PROMPT_CHUNK)
    s_tables = _sample_tables(ts, SAMPLE_BLOCK)

    tm = rows // 8
    h = _rmsnorm_rows(x, norm_mix_pre[0], tm // 2)
    p_states = []
    s_out = None
    for l in range(depth):
        proj, w_out_b = _matmul_in(h, w_in_p, w_out, l, tm, 2048)
        gates = _matmul_gates(h, w_gate_p, tm)
        outs = _prompt_mixers(proj, gates, p_tables, gp[l], gdn_conv_w[l], na[l], nb[l], nc[l], bp, tp,
                              PROMPT_CHUNK, PROMPT_SEQS)
        p_states.append(outs[1:])
        res = _sample_mixers(l, proj, gates, s_tables, gp[l], gdn_conv_w[l], na[l], nb[l], nc[l],
                             s_states, s_out, bs, ts, rows_p, SAMPLE_BLOCK)
        y_s, s_out = res[0], res[1:]
        x, h = _matmul_out(outs[0], y_s, w_out_b, x, norm_mix_post[l], norm_mlp_pre[l], rows_s)
        up = _matmul_up(h, w_up, w_down, w_in_t, l, tm, 1024)
        u, w_down_b = up[0], up[1]
        if l + 1 < depth:
            w_in_p, w_gate_p = up[2], up[3]
        x, h = _matmul_down(u, w_down_b, x, norm_mlp_post[l], norm_mix_pre[(l + 1) % depth], tm, 1024)

    stk = lambda j: jnp.stack([s[j] for s in p_states], axis=0)
    sc, sn, sm, sr, sg, sconv = s_out
    return (
        x[:rows_p].reshape(bp, tp, d), x[rows_p:].reshape(bs, ts, d),
        stk(0), stk(1), stk(2)[:, :, 0, :H_A], stk(3), stk(4), stk(5),
        sc, sn[:, ts - 1::ts, :].reshape(depth, bs, H_A, HEAD_DIM), sm[:, ts - 1::ts, :H_A], sr, sg,
        sconv,
    )
```

```python
import functools
import math
import types

import numpy as np
import jax
import jax.numpy as jnp
from jax import lax
from jax.experimental import pallas as pl
from jax.experimental.pallas import tpu as pltpu

f32 = jnp.float32
bf16 = jnp.bfloat16

D_MODEL = 2048
HEAD_DIM = 128
H_A, H_B, H_C = 4, 4, 8
D_A, D_B, D_C = H_A * HEAD_DIM, H_B * HEAD_DIM, H_C * HEAD_DIM
D_FF = 4 * D_MODEL
CONV_W = 4
PAST_LEN = 16384
ROPE_BASE = 10000.0
GATE_SOFTCAP = 15.0
NORM_EPS = 1e-6
QK_SCALE = HEAD_DIM ** -0.5

COL_A = 0
COL_B = 4 * D_A
COL_C = COL_B + 4 * D_B
N_PROJ = COL_C + 4 * D_C
LANE_I, LANE_F, LANE_DECAY, LANE_BETA = 0, H_A, 2 * H_A, 2 * H_A + H_C

PROMPT_CHUNK = 64
PROMPT_SEQS = 2
LEAD_FILLERS = 4
INV_BLOCK = 8
SAMPLE_BLOCK = 8
NEG_BIG = -1e30
VMEM_LIMIT = 60 * 1024 * 1024


def _cparams(sem):
    return pltpu.CompilerParams(dimension_semantics=sem, vmem_limit_bytes=VMEM_LIMIT)


def _rms(x, g):
    return x * lax.rsqrt(jnp.mean(x * x, axis=-1, keepdims=True) + NORM_EPS) * g


def _rmsnorm_kernel(x_ref, g_ref, o_ref):
    o_ref[...] = _rms(x_ref[...], g_ref[...]).astype(o_ref.dtype)


def _rmsnorm_rows(x, g, tm):
    m, d = x.shape
    return pl.pallas_call(
        _rmsnorm_kernel,
        grid=(m // tm,),
        in_specs=[pl.BlockSpec((tm, d), lambda i: (i, 0)), pl.BlockSpec((1, d), lambda i: (0, 0))],
        out_specs=pl.BlockSpec((tm, d), lambda i: (i, 0)),
        out_shape=jax.ShapeDtypeStruct((m, d), bf16),
        compiler_params=_cparams(("parallel",)),
        name="rmsnorm_in",
    )(x, g.reshape(1, d))


def _dot_nt(a, w):
    return lax.dot_general(a, w, (((1,), (1,)), ((), ())), preferred_element_type=f32)


def _mm_gates_kernel(a_ref, w_ref, o_ref):
    o_ref[...] = _dot_nt(a_ref[...], w_ref[...])


def _matmul_gates(h, w_t, tm):
    m, k = h.shape
    n = w_t.shape[0]
    return pl.pallas_call(
        _mm_gates_kernel,
        grid=(m // tm,),
        in_specs=[pl.BlockSpec((tm, k), lambda i: (i, 0)), pl.BlockSpec((n, k), lambda i: (0, 0))],
        out_specs=pl.BlockSpec((tm, n), lambda i: (i, 0)),
        out_shape=jax.ShapeDtypeStruct((m, n), f32),
        compiler_params=_cparams(("parallel",)),
        name="proj_gates",
    )(h, w_t)


def _mm_in_kernel(a_ref, w_ref, wo_ref, o_ref, wob_ref):
    o_ref[...] = _dot_nt(a_ref[...], w_ref[...])
    wob_ref[...] = wo_ref[...].astype(bf16)


def _matmul_in(h, w_t, w_out, layer, tm, tn):
    m, k = h.shape
    n = w_t.shape[0]
    nj, ni = n // tn, m // tm
    d = w_out.shape[1]
    slab = d // (nj * ni)
    step = lambda j, i: j * ni + i
    return pl.pallas_call(
        _mm_in_kernel,
        grid=(nj, ni),
        in_specs=[pl.BlockSpec((tm, k), lambda j, i: (i, 0)), pl.BlockSpec((tn, k), lambda j, i: (j, 0)),
                  pl.BlockSpec((None, slab, d), lambda j, i: (layer, step(j, i), 0))],
        out_specs=[pl.BlockSpec((tm, tn), lambda j, i: (i, j)),
                   pl.BlockSpec((slab, d), lambda j, i: (step(j, i), 0))],
        out_shape=[jax.ShapeDtypeStruct((m, n), f32), jax.ShapeDtypeStruct((d, d), bf16)],
        compiler_params=_cparams(("parallel", "arbitrary")),
        name="proj_in",
    )(h, w_t, w_out)


REPACK_LANES = 128


def _repack_slab(win_ref, main_ref, gate_ref):
    n_a = 4 * D_A
    n_ag = n_a + 2 * H_A
    n_bc = 4 * D_B + 4 * D_C
    main_ref[0:n_a, :] = win_ref[0:n_a, :].astype(bf16)
    main_ref[n_a:n_a + n_bc, :] = win_ref[n_ag:n_ag + n_bc, :].astype(bf16)
    gate_ref[...] = jnp.zeros_like(gate_ref)
    gate_ref[LANE_I:LANE_I + 2 * H_A, :] = win_ref[n_a:n_ag, :].astype(bf16)
    gate_ref[LANE_DECAY:LANE_DECAY + 2 * H_C, :] = win_ref[n_ag + n_bc:n_ag + n_bc + 2 * H_C, :].astype(bf16)


def _repack_part(win_ref, main_ref, gate_ref, part, parts):
    n_a = 4 * D_A
    n_ag = n_a + 2 * H_A
    n_bc = 4 * D_B + 4 * D_C
    run = N_PROJ // parts
    dst = pl.multiple_of(part * run, 16)
    src = pl.multiple_of(dst + jnp.where(dst >= n_a, 2 * H_A, 0), 8)
    main_ref[pl.ds(dst, run), :] = win_ref[pl.ds(src, run), :].astype(bf16)
    gate_ref[...] = jnp.zeros_like(gate_ref)
    gate_ref[LANE_I:LANE_I + 2 * H_A, :] = win_ref[n_a:n_ag, :].astype(bf16)
    gate_ref[LANE_DECAY:LANE_DECAY + 2 * H_C, :] = win_ref[n_ag + n_bc:n_ag + n_bc + 2 * H_C, :].astype(bf16)


def _mm_up_kernel(repack_every, a_ref, w_ref, wd_ref, *rest):
    if repack_every:
        win_ref, o_ref, wdb_ref, main_ref, gate_ref, wb_ref = rest
        step = pl.program_id(0) * pl.num_programs(1) + pl.program_id(1)
        _repack_part(win_ref, main_ref, gate_ref, step % repack_every, repack_every)
    else:
        o_ref, wdb_ref, wb_ref = rest

    @pl.when(pl.program_id(1) == 0)
    def _():
        wb_ref[...] = w_ref[...].astype(bf16)

    z = jnp.maximum(jnp.dot(a_ref[...], wb_ref[...], preferred_element_type=f32), 0.0)
    o_ref[...] = (z * z).astype(o_ref.dtype)
    wdb_ref[...] = wd_ref[...].astype(bf16)


def _matmul_up(h, w, w_down, w_in_t, layer, tm, tn):
    m, k = h.shape
    n = w.shape[2]
    nj, ni = n // tn, m // tm
    steps = nj * ni
    step = lambda j, i: j * ni + i
    with_repack = layer + 1 < w_in_t.shape[0]
    dslab = w_down.shape[1] // steps
    d = w_down.shape[2]
    in_specs = [pl.BlockSpec((tm, k), lambda j, i: (i, 0)), pl.BlockSpec((None, k, tn), lambda j, i: (layer, 0, j)),
                pl.BlockSpec((None, dslab, d), lambda j, i: (layer, step(j, i), 0))]
    out_specs = [pl.BlockSpec((tm, tn), lambda j, i: (i, j)), pl.BlockSpec((dslab, d), lambda j, i: (step(j, i), 0))]
    out_shape = [jax.ShapeDtypeStruct((m, n), bf16), jax.ShapeDtypeStruct(w_down.shape[1:], bf16)]
    args = [h, w, w_down]
    every = 0
    if with_repack:
        n_in, dm = w_in_t.shape[1:]
        every = steps // (dm // REPACK_LANES)
        slab = lambda j, i: step(j, i) // every
        in_specs.append(pl.BlockSpec((None, n_in, REPACK_LANES), lambda j, i: (layer + 1, 0, slab(j, i))))
        out_specs += [pl.BlockSpec((N_PROJ, REPACK_LANES), lambda j, i: (0, slab(j, i))),
                      pl.BlockSpec((128, REPACK_LANES), lambda j, i: (0, slab(j, i)))]
        out_shape += [jax.ShapeDtypeStruct((N_PROJ, dm), bf16), jax.ShapeDtypeStruct((128, dm), bf16)]
        args.append(w_in_t)
    return pl.pallas_call(
        functools.partial(_mm_up_kernel, every),
        grid=(nj, ni),
        in_specs=in_specs,
        out_specs=out_specs,
        out_shape=out_shape,
        scratch_shapes=[pltpu.VMEM((k, tn), bf16)],
        compiler_params=_cparams(("parallel", "arbitrary")),
        name="mlp_up",
    )(*args)


def _residual_epilogue(z, x_ref, gpost_ref, gnext_ref, xo_ref, ho_ref, rows=slice(None)):
    x_new = x_ref[rows, :] + _rms(z, gpost_ref[...])
    xo_ref[rows, :] = x_new
    ho_ref[rows, :] = _rms(x_new, gnext_ref[...]).astype(ho_ref.dtype)


OUT_ROW_PARTS = 4


def _mm_out_kernel(n_first, a_ref, b_ref, w_ref, x_ref, gpost_ref, gnext_ref, xo_ref, ho_ref):
    a = jnp.where(pl.program_id(0) < n_first, a_ref[...], b_ref[...])
    part = a.shape[0] // OUT_ROW_PARTS
    for r in range(OUT_ROW_PARTS):
        rows = slice(r * part, (r + 1) * part)
        z = jnp.dot(a[rows], w_ref[...], preferred_element_type=f32)
        _residual_epilogue(z, x_ref, gpost_ref, gnext_ref, xo_ref, ho_ref, rows)


def _matmul_out(y_p, y_s, w, x, g_post, g_next, tm):
    k = y_p.shape[1]
    n_p, n_s = y_p.shape[0] // tm, y_s.shape[0] // tm
    m = (n_p + n_s) * tm
    d = w.shape[1]
    row = lambda i: (i, 0)
    fixed = lambda i: (0, 0)
    return pl.pallas_call(
        functools.partial(_mm_out_kernel, n_p),
        grid=(n_p + n_s,),
        in_specs=[pl.BlockSpec((tm, k), lambda i: (jnp.minimum(i, n_p - 1), 0)),
                  pl.BlockSpec((tm, k), lambda i: (jnp.maximum(i - n_p, 0), 0)),
                  pl.BlockSpec((k, d), fixed), pl.BlockSpec((tm, d), row),
                  pl.BlockSpec((1, d), fixed), pl.BlockSpec((1, d), fixed)],
        out_specs=[pl.BlockSpec((tm, d), row), pl.BlockSpec((tm, d), row)],
        out_shape=[jax.ShapeDtypeStruct((m, d), f32), jax.ShapeDtypeStruct((m, d), bf16)],
        compiler_params=_cparams(("parallel",)),
        name="proj_out",
    )(y_p, y_s, w, x, g_post.reshape(1, d), g_next.reshape(1, d))


def _mm_down_kernel(a_ref, w_ref, x_ref, gpost_ref, gnext_ref, xo_ref, ho_ref):
    kk = pl.program_id(1)

    @pl.when(kk == 0)
    def _():
        xo_ref[...] = jnp.zeros_like(xo_ref)

    xo_ref[...] += jnp.dot(a_ref[...], w_ref[...], preferred_element_type=f32)

    @pl.when(kk == pl.num_programs(1) - 1)
    def _():
        _residual_epilogue(xo_ref[...], x_ref, gpost_ref, gnext_ref, xo_ref, ho_ref)


def _matmul_down(u, w, x, g_post, g_next, tm, tk):
    m, k = u.shape
    d = w.shape[1]
    row = lambda i, kk: (i, 0)
    fixed = lambda i, kk: (0, 0)
    return pl.pallas_call(
        _mm_down_kernel,
        grid=(m // tm, k // tk),
        in_specs=[pl.BlockSpec((tm, tk), lambda i, kk: (i, kk)),
                  pl.BlockSpec((tk, d), lambda i, kk: (kk, 0)),
                  pl.BlockSpec((tm, d), row), pl.BlockSpec((1, d), fixed), pl.BlockSpec((1, d), fixed)],
        out_specs=[pl.BlockSpec((tm, d), row), pl.BlockSpec((tm, d), row)],
        out_shape=[jax.ShapeDtypeStruct((m, d), f32), jax.ShapeDtypeStruct((m, d), bf16)],
        compiler_params=_cparams(("parallel", "arbitrary")),
        name="mlp_down",
    )(u, w, x, g_post.reshape(1, d), g_next.reshape(1, d))


def _dot(a, b):
    return jnp.dot(a, b, preferred_element_type=f32)


def _dot_tb(a, b):
    return lax.dot_general(a, b, (((1,), (1,)), ((), ())), preferred_element_type=f32)


def _dot_ta(a, b):
    return lax.dot_general(a, b, (((0,), (0,)), ((), ())), preferred_element_type=f32)


def _lane_sum(x):
    return jnp.sum(x, axis=-1, keepdims=True)


def _softplus(z):
    return jnp.maximum(z, 0.0) + jnp.log1p(jnp.exp(-jnp.abs(z)))


def _gate_transform(gt, gp):
    z = gt + gp[0:1, :]
    capped = GATE_SOFTCAP * jnp.tanh(z / GATE_SOFTCAP)
    log_f = -_softplus(-capped)
    decay = -jnp.exp(gp[1:2, :]) * _softplus(z)
    beta = jax.nn.sigmoid(gt)
    return capped, log_f, decay, beta


def _head_norm(x, gain, center):
    if center:
        x = x - jnp.mean(x, axis=-1, keepdims=True)
    return x * lax.rsqrt(jnp.mean(x * x, axis=-1, keepdims=True) + NORM_EPS) * gain


def _l2norm(x):
    return x * lax.rsqrt(_lane_sum(x * x) + NORM_EPS)


def _silu(x):
    return x * jax.nn.sigmoid(x)


def _rotary(x, cos, sin_signed):
    return x * cos + pltpu.roll(x, HEAD_DIM // 2, 1) * sin_signed


def _retention_log_gamma():
    return np.log1p(-np.exp2(-5.0 - np.arange(H_B, dtype=np.float64)))


def _rope_tables(pos):
    half = HEAD_DIM // 2
    inv = ROPE_BASE ** (-np.arange(half, dtype=np.float64) / half)
    ang = np.asarray(pos, dtype=np.float64)[:, None] * inv[None, :]
    cos = np.concatenate([np.cos(ang), np.cos(ang)], axis=-1)
    sin = np.concatenate([-np.sin(ang), np.sin(ang)], axis=-1)
    return cos.astype(np.float32), sin.astype(np.float32)


def _scan_rows(x, row, length, op, fill):
    s = 1
    while s < length:
        x = op(x, jnp.where(row >= s, pltpu.roll(x, s, 0), fill))
        s *= 2
    return x


def _rows_to_lanes(x, length):
    if length < 128:
        x = jnp.concatenate([x, jnp.zeros((128 - length, 128), x.dtype)], axis=0)
    return x.T[:, :length]


def _prompt_mixer_kernel(L, NS, *refs):
    proj_refs, gt_refs = refs[:NS], refs[NS:2 * NS]
    (cos_ref, sin_ref, rd_ref, rq_ref, rk_ref, gp_ref, cw_ref, na_ref, nb_ref, nc_ref,
     y_ref, c_ref, n_ref, m_ref, r_ref, g_ref, conv_ref, ext_ref) = refs[2 * NS:]
    c = pl.program_id(1)

    @pl.when(c == 0)
    def _():
        c_ref[...] = jnp.zeros_like(c_ref)
        n_ref[...] = jnp.zeros_like(n_ref)
        m_ref[...] = jnp.zeros_like(m_ref)
        r_ref[...] = jnp.zeros_like(r_ref)
        g_ref[...] = jnp.zeros_like(g_ref)
        ext_ref[:, 0:8, :] = jnp.zeros((NS, 8, 3 * D_C), f32)

    row = lax.broadcasted_iota(jnp.int32, (L, 128), 0)
    lane = lax.broadcasted_iota(jnp.int32, (L, 128), 1)
    ri = lax.broadcasted_iota(jnp.int32, (L, L), 0)
    ci = lax.broadcasted_iota(jnp.int32, (L, L), 1)
    incl = ri >= ci
    strict = ri > ci
    head_lane = lane < H_A
    seqs = range(NS)

    def gate_block(s):
        g = types.SimpleNamespace()
        capped, log_f, decay, g.beta = _gate_transform(gt_refs[s][...], gp_ref[...])
        ig = jnp.where(head_lane, capped, 0.0)
        lf = jnp.where(head_lane, pltpu.roll(log_f, 128 - LANE_F, 1), 0.0)
        mp = m_ref[s]
        F = _scan_rows(lf, row, L, jnp.add, 0.0)
        m = F + jnp.maximum(mp, _scan_rows(ig - F, row, L, jnp.maximum, NEG_BIG))
        m_new = m[L - 1:L, :]
        f_last = F[L - 1:L, :]
        g.a_rows = F - m
        g.b_lanes = _rows_to_lanes(ig - F, L)
        g.inter = jnp.exp(F + mp - m)
        g.inv_floor = jnp.exp(-m)
        g.wl = jnp.exp(ig + f_last - F - m_new)
        g.dec = jnp.exp(f_last + mp - m_new)
        m_ref[s] = m_new
        g.G = _scan_rows(decay, row, L, jnp.add, 0.0)
        g.g_lanes = _rows_to_lanes(g.G, L)
        g.gam = jnp.exp(g.G)
        g_last = g.G[L - 1:L, :]
        g.k_decay = jnp.exp(g_last - g.G)
        g.s_decay = jnp.exp(g_last)
        return g

    gs = [gate_block(s) for s in seqs]

    def mlstm_head(s, h):
        proj_ref, g = proj_refs[s], gs[s]
        q = proj_ref[:, COL_A + h * HEAD_DIM:COL_A + (h + 1) * HEAD_DIM]
        k = proj_ref[:, COL_A + D_A + h * HEAD_DIM:COL_A + D_A + (h + 1) * HEAD_DIM] * QK_SCALE
        v = proj_ref[:, COL_A + 2 * D_A + h * HEAD_DIM:COL_A + 2 * D_A + (h + 1) * HEAD_DIM]
        og = proj_ref[:, COL_A + 3 * D_A + h * HEAD_DIM:COL_A + 3 * D_A + (h + 1) * HEAD_DIM]
        logw = g.a_rows[:, h:h + 1] + g.b_lanes[h:h + 1, :]
        sc = _dot_tb(q, k) * jnp.exp(jnp.where(incl, logw, NEG_BIG))
        c_old = c_ref[s, h]
        n_old = n_ref[s, h:h + 1, :]
        inter_h = g.inter[:, h:h + 1]
        num = inter_h * _dot(q, c_old) + _dot(sc, v)
        den = inter_h * _lane_sum(q * n_old) + _lane_sum(sc)
        hh = num / jnp.maximum(jnp.abs(den), g.inv_floor[:, h:h + 1])
        kw = k * g.wl[:, h:h + 1]
        dec_h = g.dec[:, h:h + 1]
        c_ref[s, h] = dec_h * c_old + _dot_ta(kw, v)
        n_ref[s, h:h + 1, :] = dec_h * n_old + jnp.sum(kw, axis=0, keepdims=True)
        ya = jax.nn.sigmoid(og) * _head_norm(hh, na_ref[h:h + 1, :], False)
        y_ref[s, :, h * HEAD_DIM:(h + 1) * HEAD_DIM] = ya.astype(y_ref.dtype)

    lg = _retention_log_gamma()

    def retention_head(s, h):
        proj_ref = proj_refs[s]
        cos = cos_ref[...]
        sin = sin_ref[...]
        q = proj_ref[:, COL_B + h * HEAD_DIM:COL_B + (h + 1) * HEAD_DIM]
        k = proj_ref[:, COL_B + D_B + h * HEAD_DIM:COL_B + D_B + (h + 1) * HEAD_DIM]
        v = proj_ref[:, COL_B + 2 * D_B + h * HEAD_DIM:COL_B + 2 * D_B + (h + 1) * HEAD_DIM]
        gb = proj_ref[:, COL_B + 3 * D_B + h * HEAD_DIM:COL_B + 3 * D_B + (h + 1) * HEAD_DIM]
        qr = _rotary(q, cos, sin)
        kr = _rotary(k, cos, sin) * QK_SCALE
        s_old = r_ref[s, h]
        inner = _dot_tb(qr, kr) * rd_ref[h]
        o = _dot(inner, v) + rq_ref[h] * _dot(qr, s_old)
        r_ref[s, h] = float(np.exp(L * lg[h])) * s_old + _dot_ta(kr * rk_ref[h], v)
        yb = _silu(gb) * _head_norm(o, nb_ref[h:h + 1, :], True)
        y_ref[s, :, D_A + h * HEAD_DIM:D_A + (h + 1) * HEAD_DIM] = yb.astype(y_ref.dtype)

    fillers = [functools.partial(mlstm_head, s, h) for h in range(H_A) for s in seqs]
    fillers += [functools.partial(retention_head, s, h) for h in range(H_B) for s in seqs]
    def emit_fillers(count):
        for _ in range(min(count, len(fillers))):
            fillers.pop(0)()

    emit_fillers(LEAD_FILLERS)
    n_stages = max(int(math.log2(min(INV_BLOCK, L))) - 1, 0) + int(math.log2(L // min(INV_BLOCK, L)))
    per_stage = -(-len(fillers) // max(n_stages, 1))

    for s in seqs:
        ext_ref[s, 8:8 + L, :] = proj_refs[s][:, COL_C:COL_C + 3 * D_C]

    def conv_block(s, col):
        acc = cw_ref[CONV_W - 1:CONV_W, col:col + HEAD_DIM] * ext_ref[s, 8:8 + L, col:col + HEAD_DIM]
        for w in range(CONV_W - 1):
            off = 8 - (CONV_W - 1) + w
            acc = acc + cw_ref[w:w + 1, col:col + HEAD_DIM] * ext_ref[s, off:off + L, col:col + HEAD_DIM]
        return _silu(acc)

    pairs = [(s, h) for h in range(H_C) for s in seqs]
    heads = range(len(pairs))
    dcol = [LANE_DECAY + h for _, h in pairs]
    gam = [gs[s].gam for s, _ in pairs]
    beta_c = [gs[s].beta[:, LANE_BETA + h:LANE_BETA + h + 1] for s, h in pairs]
    q = [_l2norm(conv_block(s, h * HEAD_DIM)) * QK_SCALE for s, h in pairs]
    k = [_l2norm(conv_block(s, D_C + h * HEAD_DIM)) for s, h in pairs]
    v = [conv_block(s, 2 * D_C + h * HEAD_DIM) for s, h in pairs]
    dec_in = [jnp.exp(jnp.where(incl, gs[s].G[:, LANE_DECAY + h:LANE_DECAY + h + 1]
                                - gs[s].g_lanes[LANE_DECAY + h:LANE_DECAY + h + 1, :], NEG_BIG)) for s, h in pairs]
    a_mat = [jnp.where(strict, dec_in[h], 0.0) * beta_c[h] * _dot_tb(k[h], k[h]) for h in heads]
    qk = [_dot_tb(q[h], k[h]) * dec_in[h] for h in heads]
    blk = min(INV_BLOCK, L)
    same = (ri // blk) == (ci // blk)
    pw = [jnp.where(same, -a, 0.0) for a in a_mat]
    e_mat = list(pw)
    span = 1
    while 2 * span < blk:
        pw = [_dot(p, p) for p in pw]
        e_mat = [e_mat[h] + pw[h] + _dot(e_mat[h], pw[h]) for h in heads]
        emit_fillers(per_stage)
        span *= 2
    while blk < L:
        wider = (ri // (2 * blk)) == (ci // (2 * blk))
        a_off = [jnp.where(wider & jnp.logical_not(same), a, 0.0) for a in a_mat]
        low = [a_off[h] + _dot(e_mat[h], a_off[h]) for h in heads]
        e_mat = [e_mat[h] - low[h] - _dot(low[h], e_mat[h]) for h in heads]
        emit_fillers(per_stage)
        same = wider
        blk *= 2
    rhs = [jnp.concatenate([(beta_c[h] * gam[h][:, dcol[h]:dcol[h] + 1]) * k[h], beta_c[h] * v[h]], axis=1)
           for h in heads]
    sol = [rhs[h] + _dot(e_mat[h], rhs[h]) for h in heads]
    emit_fillers(len(fillers))
    s_old = [g_ref[s, hd] for s, hd in pairs]
    both = [_dot(jnp.concatenate([sol[h][:, :HEAD_DIM], q[h]], axis=0), s_old[h]) for h in heads]
    u = [sol[h][:, HEAD_DIM:] - both[h][:L] for h in heads]
    o = [gam[h][:, dcol[h]:dcol[h] + 1] * both[h][L:] + _dot(qk[h], u[h]) for h in heads]
    for h, (s, hd) in enumerate(pairs):
        g_ref[s, hd] = (gs[s].s_decay[:, dcol[h]:dcol[h] + 1] * s_old[h]
                        + _dot_ta(k[h] * gs[s].k_decay[:, dcol[h]:dcol[h] + 1], u[h]))
    for h, (s, hd) in enumerate(pairs):
        gz = proj_refs[s][:, COL_C + 3 * D_C + hd * HEAD_DIM:COL_C + 3 * D_C + (hd + 1) * HEAD_DIM]
        yc = _silu(gz) * _head_norm(o[h], nc_ref[...], False)
        y_ref[s, :, D_A + D_B + hd * HEAD_DIM:D_A + D_B + (hd + 1) * HEAD_DIM] = yc.astype(y_ref.dtype)

    for s in seqs:
        ext_ref[s, 0:8, :] = ext_ref[s, L:L + 8, :]

    @pl.when(c == pl.num_programs(1) - 1)
    def _():
        for s in seqs:
            conv_ref[s] = ext_ref[s, 8 + L - (CONV_W - 1):8 + L, :]


def _prompt_tables(seq, L):
    lg = _retention_log_gamma()
    i = np.arange(L, dtype=np.float64)
    diff = i[:, None] - i[None, :]
    rd = np.where(diff >= 0, np.exp(np.maximum(diff, 0.0) * lg[:, None, None]), 0.0)
    rq = np.broadcast_to(np.exp((i + 1.0) * lg[:, None])[..., None], (H_B, L, HEAD_DIM))
    rk = np.broadcast_to(np.exp((L - 1.0 - i) * lg[:, None])[..., None], (H_B, L, HEAD_DIM))
    cos, sin = _rope_tables(np.arange(seq))
    return (jnp.asarray(cos), jnp.asarray(sin), jnp.asarray(rd, f32), jnp.asarray(rq, f32), jnp.asarray(rk, f32))


def _prompt_mixers(proj, gates, tables, gp, cw, na, nb, nc, batch, seq, L, NS):
    nchunk = seq // L
    cos, sin, rd, rq, rk = tables
    full = lambda shape: pl.BlockSpec(shape, lambda b, c: (0,) * len(shape))
    state4 = lambda heads: pl.BlockSpec((NS, heads, HEAD_DIM, HEAD_DIM), lambda b, c: (b, 0, 0, 0))
    rows_of = lambda s, width: pl.BlockSpec((L, width), lambda b, c: ((b * NS + s) * nchunk + c, 0))
    outs = pl.pallas_call(
        functools.partial(_prompt_mixer_kernel, L, NS),
        grid=(batch // NS, nchunk),
        in_specs=[rows_of(s, N_PROJ) for s in range(NS)] + [rows_of(s, 128) for s in range(NS)] + [
            pl.BlockSpec((L, HEAD_DIM), lambda b, c: (c, 0)),
            pl.BlockSpec((L, HEAD_DIM), lambda b, c: (c, 0)),
            full((H_B, L, L)), full((H_B, L, HEAD_DIM)), full((H_B, L, HEAD_DIM)),
            full((8, 128)), full((CONV_W, 3 * D_C)),
            full((H_A, HEAD_DIM)), full((H_B, HEAD_DIM)), full((1, HEAD_DIM)),
        ],
        out_specs=[
            pl.BlockSpec((None, NS, None, L, D_MODEL), lambda b, c: (b, 0, c, 0, 0)),
            state4(H_A),
            pl.BlockSpec((NS, H_A, HEAD_DIM), lambda b, c: (b, 0, 0)),
            pl.BlockSpec((NS, 1, 128), lambda b, c: (b, 0, 0)),
            state4(H_B),
            state4(H_C),
            pl.BlockSpec((NS, CONV_W - 1, 3 * D_C), lambda b, c: (b, 0, 0)),
        ],
        out_shape=[
            jax.ShapeDtypeStruct((batch // NS, NS, nchunk, L, D_MODEL), bf16),
            jax.ShapeDtypeStruct((batch, H_A, HEAD_DIM, HEAD_DIM), f32),
            jax.ShapeDtypeStruct((batch, H_A, HEAD_DIM), f32),
            jax.ShapeDtypeStruct((batch, 1, 128), f32),
            jax.ShapeDtypeStruct((batch, H_B, HEAD_DIM, HEAD_DIM), f32),
            jax.ShapeDtypeStruct((batch, H_C, HEAD_DIM, HEAD_DIM), f32),
            jax.ShapeDtypeStruct((batch, CONV_W - 1, 3 * D_C), f32),
        ],
        scratch_shapes=[pltpu.VMEM((NS, L + 8, 3 * D_C), f32)],
        compiler_params=_cparams(("parallel", "arbitrary")),
        name="prompt_mixers",
    )(*([proj] * NS + [gates] * NS), cos, sin, rd, rq, rk, gp, cw, na, nb, nc)
    return [outs[0].reshape(batch * seq, D_MODEL)] + list(outs[1:])


def _sample_mixer_kernel(T, BB, layer_first, *refs):
    (proj_ref, gt_ref, cos_ref, sin_ref, rt_ref, gp_ref, cw_ref, na_ref, nb_ref, nc_ref,
     c0_ref, nrep_ref, mrep_ref, r0_ref, g0_ref, conv0_ref) = refs[:16]
    rest = refs[16 + (0 if layer_first else 6):]
    (y_ref, c_ref, nout_ref, mout_ref, r_ref, g_ref, conv_ref, cv_ref, sa_ref, ext_ref) = rest
    R = T * BB
    row = lax.broadcasted_iota(jnp.int32, (R, 128), 0)
    lane = lax.broadcasted_iota(jnp.int32, (R, 128), 1)
    t = lax.rem(row, T)
    tcol = t[:, 0:1]

    def shift(x, s):
        return x if s == 0 else pltpu.roll(x, s, 0)

    def seg_scan(x, op, fill):
        s = 1
        while s < T:
            x = op(x, jnp.where(t >= s, shift(x, s), fill))
            s *= 2
        return x

    def last_rep(x):
        x_last = jnp.where(t == T - 1, x, 0.0)
        out = x_last
        for s in range(1, T):
            out = out + pltpu.roll(x_last, R - s, 0)
        return out

    capped, log_f, decay, beta = _gate_transform(gt_ref[...], gp_ref[...])

    head_lane = lane < H_A
    ig = jnp.where(head_lane, capped, 0.0)
    lf = jnp.where(head_lane, pltpu.roll(log_f, 128 - LANE_F, 1), 0.0)
    mp = mrep_ref[...]
    F = seg_scan(lf, jnp.add, 0.0)
    m = F + jnp.maximum(mp, seg_scan(ig - F, jnp.maximum, NEG_BIG))
    m_new = last_rep(m)
    f_last = last_rep(F)
    inter = jnp.exp(F + mp - m)
    inv_floor = jnp.exp(-m)
    wl = jnp.exp(ig + f_last - F - m_new)
    dec = jnp.exp(f_last + mp - m_new)
    mout_ref[...] = m_new
    pw = [jnp.where(t >= s, jnp.exp(F - shift(F, s) + shift(ig, s) - m), 0.0) for s in range(T)]

    def seq(x, b):
        return x[b * T:(b + 1) * T]

    slot_a, slot_b, slot_w, slot_q = 0, H_A, H_A + H_B, H_A + H_B + H_C

    qa = [proj_ref[:, COL_A + h * HEAD_DIM:COL_A + (h + 1) * HEAD_DIM] for h in range(H_A)]
    for h in range(H_A):
        for b in range(BB):
            sa_ref[slot_a + h, b * T:(b + 1) * T, :] = _dot(seq(qa[h], b), c0_ref[b, h])
    cos = cos_ref[...]
    sin = sin_ref[...]
    lg = _retention_log_gamma()
    qr = [_rotary(proj_ref[:, COL_B + h * HEAD_DIM:COL_B + (h + 1) * HEAD_DIM], cos, sin) for h in range(H_B)]
    kr = [_rotary(proj_ref[:, COL_B + D_B + h * HEAD_DIM:COL_B + D_B + (h + 1) * HEAD_DIM], cos, sin) * QK_SCALE
          for h in range(H_B)]
    for h in range(H_B):
        for b in range(BB):
            sa_ref[slot_b + h, b * T:(b + 1) * T, :] = _dot(seq(qr[h], b), r0_ref[b, h])

    ka = [proj_ref[:, COL_A + D_A + h * HEAD_DIM:COL_A + D_A + (h + 1) * HEAD_DIM] * QK_SCALE for h in range(H_A)]
    va = [proj_ref[:, COL_A + 2 * D_A + h * HEAD_DIM:COL_A + 2 * D_A + (h + 1) * HEAD_DIM] for h in range(H_A)]
    n_old = [nrep_ref[:, h * HEAD_DIM:(h + 1) * HEAD_DIM] for h in range(H_A)]
    kw = [ka[h] * wl[:, h:h + 1] for h in range(H_A)]
    for h in range(H_A):
        for b in range(BB):
            dec_bh = dec[b * T:b * T + 1, h:h + 1]
            c_ref[b, h] = dec_bh * c0_ref[b, h] + _dot_ta(seq(kw[h], b), seq(va[h], b))
        nout_ref[:, h * HEAD_DIM:(h + 1) * HEAD_DIM] = dec[:, h:h + 1] * n_old[h] + seg_scan(kw[h], jnp.add, 0.0)
    vb = [proj_ref[:, COL_B + 2 * D_B + h * HEAD_DIM:COL_B + 2 * D_B + (h + 1) * HEAD_DIM] for h in range(H_B)]
    for h in range(H_B):
        kd = kr[h] * rt_ref[:, H_B + h:H_B + h + 1]
        for b in range(BB):
            r_ref[b, h] = float(np.exp(T * lg[h])) * r0_ref[b, h] + _dot_ta(seq(kd, b), seq(vb[h], b))

    for b in range(BB):
        ext_ref[b, 0:CONV_W - 1, :] = conv0_ref[b]
        ext_ref[b, CONV_W - 1:CONV_W - 1 + T, :] = proj_ref[b * T:(b + 1) * T, COL_C:COL_C + 3 * D_C]
        acc = cw_ref[0:1, :] * ext_ref[b, 0:T, :]
        for w in range(1, CONV_W):
            acc = acc + cw_ref[w:w + 1, :] * ext_ref[b, w:w + T, :]
        cv_ref[b * T:(b + 1) * T, :] = _silu(acc)
        conv_ref[b] = ext_ref[b, T:T + CONV_W - 1, :]

    G = seg_scan(decay, jnp.add, 0.0)
    gam = jnp.exp(G)
    g_last = last_rep(G)
    k_decay = jnp.exp(g_last - G)
    s_decay = jnp.exp(g_last)
    dshift = [None] + [jnp.where(t >= s, jnp.exp(G - shift(G, s)), 0.0) for s in range(1, T)]

    heads = range(H_C)
    dcol = [LANE_DECAY + h for h in heads]
    beta_c = [beta[:, LANE_BETA + h:LANE_BETA + h + 1] for h in heads]
    qg = [_l2norm(cv_ref[:, h * HEAD_DIM:(h + 1) * HEAD_DIM]) * QK_SCALE for h in heads]
    kg = [_l2norm(cv_ref[:, D_C + h * HEAD_DIM:D_C + (h + 1) * HEAD_DIM]) for h in heads]
    vg = [cv_ref[:, 2 * D_C + h * HEAD_DIM:2 * D_C + (h + 1) * HEAD_DIM] for h in heads]
    a_sub = [[None] + [beta_c[h] * dshift[s][:, dcol[h]:dcol[h] + 1] * _lane_sum(kg[h] * shift(kg[h], s))
                       for s in range(1, T)] for h in heads]
    rhs_w = [(beta_c[h] * gam[:, dcol[h]:dcol[h] + 1]) * kg[h] for h in heads]
    rhs_u = [beta_c[h] * vg[h] for h in heads]
    w_sol, u_sol = list(rhs_w), list(rhs_u)
    for i in range(1, T):
        for h in heads:
            upd_w = a_sub[h][1] * shift(w_sol[h], 1)
            upd_u = a_sub[h][1] * shift(u_sol[h], 1)
            for s in range(2, i + 1):
                upd_w = upd_w + a_sub[h][s] * shift(w_sol[h], s)
                upd_u = upd_u + a_sub[h][s] * shift(u_sol[h], s)
            w_sol[h] = jnp.where(t == i, rhs_w[h] - upd_w, w_sol[h])
            u_sol[h] = jnp.where(t == i, rhs_u[h] - upd_u, u_sol[h])
    for h in heads:
        for b in range(BB):
            both = _dot(jnp.concatenate([seq(w_sol[h], b), seq(qg[h], b)], axis=0), g0_ref[b, h])
            sa_ref[slot_w + h, b * T:(b + 1) * T, :] = both[:T]
            sa_ref[slot_q + h, b * T:(b + 1) * T, :] = both[T:]

    ug = [u_sol[h] - sa_ref[slot_w + h] for h in heads]
    for h in heads:
        kd = kg[h] * k_decay[:, dcol[h]:dcol[h] + 1]
        for b in range(BB):
            g_ref[b, h] = (s_decay[b * T:b * T + 1, dcol[h]:dcol[h] + 1] * g0_ref[b, h]
                           + _dot_ta(seq(kd, b), seq(ug[h], b)))

    sc_a = [[_lane_sum(qa[h] * shift(ka[h], s)) * pw[s][:, h:h + 1] for s in range(T)] for h in range(H_A)]
    qn = [_lane_sum(qa[h] * n_old[h]) for h in range(H_A)]
    sc_b = [[jnp.where(tcol >= s, _lane_sum(qr[h] * shift(kr[h], s)) * float(np.exp(s * lg[h])), 0.0)
             for s in range(T)] for h in range(H_B)]
    sc_c = [[_lane_sum(qg[h] * shift(kg[h], s)) * (1.0 if s == 0 else dshift[s][:, dcol[h]:dcol[h] + 1])
             for s in range(T)] for h in heads]
    hid_a, hid_b, hid_c = [], [], []
    for h in range(H_A):
        inter_h = inter[:, h:h + 1]
        num = inter_h * sa_ref[slot_a + h]
        den = inter_h * qn[h]
        for s in range(T):
            num = num + sc_a[h][s] * shift(va[h], s)
            den = den + sc_a[h][s]
        hid_a.append(num / jnp.maximum(jnp.abs(den), inv_floor[:, h:h + 1]))
    for h in range(H_B):
        o = rt_ref[:, h:h + 1] * sa_ref[slot_b + h]
        for s in range(T):
            o = o + sc_b[h][s] * shift(vb[h], s)
        hid_b.append(o)
    for h in heads:
        o = gam[:, dcol[h]:dcol[h] + 1] * sa_ref[slot_q + h]
        for s in range(T):
            o = o + sc_c[h][s] * shift(ug[h], s)
        hid_c.append(o)
    mean_b = [jnp.mean(x, axis=-1, keepdims=True) for x in hid_b]
    hid_b = [x - mu for x, mu in zip(hid_b, mean_b)]
    hidden = hid_a + hid_b + hid_c
    inv_rms = [lax.rsqrt(jnp.mean(x * x, axis=-1, keepdims=True) + NORM_EPS) for x in hidden]
    for h in range(H_A):
        og = proj_ref[:, COL_A + 3 * D_A + h * HEAD_DIM:COL_A + 3 * D_A + (h + 1) * HEAD_DIM]
        ya = jax.nn.sigmoid(og) * (hidden[h] * inv_rms[h] * na_ref[h:h + 1, :])
        y_ref[:, h * HEAD_DIM:(h + 1) * HEAD_DIM] = ya.astype(y_ref.dtype)
    for h in range(H_B):
        gb = proj_ref[:, COL_B + 3 * D_B + h * HEAD_DIM:COL_B + 3 * D_B + (h + 1) * HEAD_DIM]
        yb = _silu(gb) * (hidden[H_A + h] * inv_rms[H_A + h] * nb_ref[h:h + 1, :])
        y_ref[:, D_A + h * HEAD_DIM:D_A + (h + 1) * HEAD_DIM] = yb.astype(y_ref.dtype)
    for h in heads:
        gz = proj_ref[:, COL_C + 3 * D_C + h * HEAD_DIM:COL_C + 3 * D_C + (h + 1) * HEAD_DIM]
        yc = _silu(gz) * (hidden[H_A + H_B + h] * inv_rms[H_A + H_B + h] * nc_ref[...])
        y_ref[:, D_A + D_B + h * HEAD_DIM:D_A + D_B + (h + 1) * HEAD_DIM] = yc.astype(y_ref.dtype)


def _sample_tables(T, BB):
    lg = _retention_log_gamma()
    tt = np.arange(T, dtype=np.float64)
    rt = np.zeros((T, 128), np.float64)
    rt[:, 0:H_B] = np.exp((tt[:, None] + 1.0) * lg[None, :])
    rt[:, H_B:2 * H_B] = np.exp((T - 1.0 - tt[:, None]) * lg[None, :])
    cos, sin = _rope_tables(PAST_LEN + np.arange(T))
    rep = lambda a: jnp.asarray(np.tile(a, (BB, 1)), f32)
    return rep(cos), rep(sin), rep(rt)


def _sample_mixers(layer, proj, gates, tables, gp, cw, na, nb, nc, states, prev_out, batch, T, row0, BB):
    c0, nrep, mrep, r0, g0, conv0 = states
    R = T * BB
    nblk = batch // BB
    blk0 = row0 // R
    cos, sin, rt = tables
    full = lambda shape: pl.BlockSpec(shape, lambda i: (0,) * len(shape))
    st4 = lambda heads: pl.BlockSpec((None, BB, heads, HEAD_DIM, HEAD_DIM), lambda i: (layer, i, 0, 0, 0))
    rows = lambda width: pl.BlockSpec((None, R, width), lambda i: (layer, i, 0))
    convspec = pl.BlockSpec((None, BB, CONV_W - 1, 3 * D_C), lambda i: (layer, i, 0, 0))
    anyspec = pl.BlockSpec(memory_space=pl.ANY)
    in_specs = [
        pl.BlockSpec((R, N_PROJ), lambda i: (blk0 + i, 0)),
        pl.BlockSpec((R, 128), lambda i: (blk0 + i, 0)),
        full((R, HEAD_DIM)), full((R, HEAD_DIM)), full((R, 128)),
        full((8, 128)), full((CONV_W, 3 * D_C)),
        full((H_A, HEAD_DIM)), full((H_B, HEAD_DIM)), full((1, HEAD_DIM)),
        st4(H_A), rows(D_A), rows(128), st4(H_B), st4(H_C), convspec,
    ]
    args = [proj, gates, cos, sin, rt, gp, cw, na, nb, nc, c0, nrep, mrep, r0, g0, conv0]
    out_shape = [
        jax.ShapeDtypeStruct((batch * T, D_MODEL), bf16),
        jax.ShapeDtypeStruct(c0.shape, f32),
        jax.ShapeDtypeStruct(nrep.shape, f32),
        jax.ShapeDtypeStruct(mrep.shape, f32),
        jax.ShapeDtypeStruct(r0.shape, f32),
        jax.ShapeDtypeStruct(g0.shape, f32),
        jax.ShapeDtypeStruct(conv0.shape, f32),
    ]
    out_specs = [pl.BlockSpec((R, D_MODEL), lambda i: (i, 0)),
                 st4(H_A), rows(D_A), rows(128), st4(H_B), st4(H_C), convspec]
    first = prev_out is None
    aliases = {}
    if not first:
        in_specs += [anyspec] * 6
        args += list(prev_out)
        aliases = {16 + j: 1 + j for j in range(6)}
    return pl.pallas_call(
        functools.partial(_sample_mixer_kernel, T, BB, first),
        grid=(nblk,),
        in_specs=in_specs,
        out_specs=out_specs,
        out_shape=out_shape,
        input_output_aliases=aliases,
        scratch_shapes=[pltpu.VMEM((R, 3 * D_C), f32), pltpu.VMEM((H_A + H_B + 2 * H_C, R, HEAD_DIM), f32),
                        pltpu.VMEM((BB, 8, 3 * D_C), f32)],
        compiler_params=_cparams(("parallel",)),
        name="sample_mixers",
    )(*args)


def _repack_kernel(win_ref, main_ref, gate_ref):
    _repack_slab(win_ref, main_ref, gate_ref)


def _repack_w_in(w_in_t, layer):
    _, n_in, dm = w_in_t.shape
    return pl.pallas_call(
        _repack_kernel,
        grid=(dm // REPACK_LANES,),
        in_specs=[pl.BlockSpec((None, n_in, REPACK_LANES), lambda i: (layer, 0, i))],
        out_specs=[pl.BlockSpec((N_PROJ, REPACK_LANES), lambda i: (0, i)),
                   pl.BlockSpec((128, REPACK_LANES), lambda i: (0, i))],
        out_shape=[jax.ShapeDtypeStruct((N_PROJ, dm), bf16), jax.ShapeDtypeStruct((128, dm), bf16)],
        compiler_params=_cparams(("parallel",)),
        name="repack_w_in",
    )(w_in_t)


def _gate_params(gate_bias, dt_bias, a_log):
    depth = gate_bias.shape[0]
    gp = jnp.zeros((depth, 8, 128), f32)
    gp = gp.at[:, 0, LANE_I:LANE_I + 2 * H_A].set(gate_bias)
    gp = gp.at[:, 0, LANE_DECAY:LANE_DECAY + H_C].set(dt_bias)
    gp = gp.at[:, 1, LANE_DECAY:LANE_DECAY + H_C].set(a_log)
    return gp


def kernel(x_prompt, x_sample, state_mlstm_C, state_mlstm_n, state_mlstm_m, state_ret_S, state_gdn_S,
           state_gdn_conv, norm_mix_pre, norm_mix_post, norm_mlp_pre, norm_mlp_post, w_in, mlstm_gate_bias,
           gdn_conv_w, gdn_A_log, gdn_dt_bias, norm_mlstm, norm_ret, norm_gdn, w_out, w_up, w_down):
    bp, tp, d = x_prompt.shape
    bs, ts, _ = x_sample.shape
    depth = w_in.shape[0]
    rows_p = bp * tp
    rows_s = bs * ts
    rows = rows_p + rows_s

    x = jnp.concatenate([x_prompt.reshape(rows_p, d), x_sample.reshape(rows_s, d)], axis=0)
    w_in_t = jnp.swapaxes(w_in, 1, 2)
    w_in_p, w_gate_p = _repack_w_in(w_in_t, 0)
    gp = _gate_params(mlstm_gate_bias, gdn_dt_bias, gdn_A_log)
    na = norm_mlstm.reshape(depth, H_A, HEAD_DIM)
    nb = norm_ret.reshape(depth, H_B, HEAD_DIM)
    nc = norm_gdn.reshape(depth, 1, HEAD_DIM)

    nrep = jnp.repeat(state_mlstm_n.reshape(depth, bs, D_A), ts, axis=1)
    mrep = jnp.pad(jnp.repeat(state_mlstm_m, ts, axis=1), ((0, 0), (0, 0), (0, 128 - H_A)))
    s_states = (state_mlstm_C, nrep, mrep, state_ret_S, state_gdn_S, state_gdn_conv)

    p_tables = _prompt_tables(tp, PROMPT_CHUNK)
    s_tables = _sample_tables(ts, SAMPLE_BLOCK)

    tm = rows // 8
    h = _rmsnorm_rows(x, norm_mix_pre[0], tm // 2)
    p_states = []
    s_out = None
    for l in range(depth):
        proj, w_out_b = _matmul_in(h, w_in_p, w_out, l, tm, 2048)
        gates = _matmul_gates(h, w_gate_p, tm)
        outs = _prompt_mixers(proj, gates, p_tables, gp[l], gdn_conv_w[l], na[l], nb[l], nc[l], bp, tp,
                              PROMPT_CHUNK, PROMPT_SEQS)
        p_states.append(outs[1:])
        res = _sample_mixers(l, proj, gates, s_tables, gp[l], gdn_conv_w[l], na[l], nb[l], nc[l],
                             s_states, s_out, bs, ts, rows_p, SAMPLE_BLOCK)
        y_s, s_out = res[0], res[1:]
        x, h = _matmul_out(outs[0], y_s, w_out_b, x, norm_mix_post[l], norm_mlp_pre[l], rows_s)
        up = _matmul_up(h, w_up, w_down, w_in_t, l, tm, 1024)
        u, w_down_b = up[0], up[1]
        if l + 1 < depth:
            w_in_p, w_gate_p = up[2], up[3]
        x, h = _matmul_down(u, w_down_b, x, norm_mlp_post[l], norm_mix_pre[(l + 1) % depth], tm, 1024)

    stk = lambda j: jnp.stack([s[j] for s in p_states], axis=0)
    sc, sn, sm, sr, sg, sconv = s_out
    return (
        x[:rows_p].reshape(bp, tp, d), x[rows_p:].reshape(bs, ts, d),
        stk(0), stk(1), stk(2)[:, :, 0, :H_A], stk(3), stk(4), stk(5),
        sc, sn[:, ts - 1::ts, :].reshape(depth, bs, H_A, HEAD_DIM), sm[:, ts - 1::ts, :H_A], sr, sg,
        sconv,
    )
```

```python
import functools
import math
import types

import numpy as np
import jax
import jax.numpy as jnp
from jax import lax
from jax.experimental import pallas as pl
from jax.experimental.pallas import tpu as pltpu

f32 = jnp.float32
bf16 = jnp.bfloat16

D_MODEL = 2048
HEAD_DIM = 128
H_A, H_B, H_C = 4, 4, 8
D_A, D_B, D_C = H_A * HEAD_DIM, H_B * HEAD_DIM, H_C * HEAD_DIM
D_FF = 4 * D_MODEL
CONV_W = 4
PAST_LEN = 16384
ROPE_BASE = 10000.0
GATE_SOFTCAP = 15.0
NORM_EPS = 1e-6
QK_SCALE = HEAD_DIM ** -0.5

COL_A = 0
COL_B = 4 * D_A
COL_C = COL_B + 4 * D_B
N_PROJ = COL_C + 4 * D_C
LANE_I, LANE_F, LANE_DECAY, LANE_BETA = 0, H_A, 2 * H_A, 2 * H_A + H_C

PROMPT_CHUNK = 64
PROMPT_SEQS = 2
LEAD_FILLERS = 4
INV_BLOCK = 8
SAMPLE_BLOCK = 8
NEG_BIG = -1e30
VMEM_LIMIT = 60 * 1024 * 1024


def _cparams(sem):
    return pltpu.CompilerParams(dimension_semantics=sem, vmem_limit_bytes=VMEM_LIMIT)


def _rms(x, g):
    return x * lax.rsqrt(jnp.mean(x * x, axis=-1, keepdims=True) + NORM_EPS) * g


def _rmsnorm_kernel(x_ref, g_ref, o_ref):
    o_ref[...] = _rms(x_ref[...], g_ref[...]).astype(o_ref.dtype)


def _rmsnorm_rows(x, g, tm):
    m, d = x.shape
    return pl.pallas_call(
        _rmsnorm_kernel,
        grid=(m // tm,),
        in_specs=[pl.BlockSpec((tm, d), lambda i: (i, 0)), pl.BlockSpec((1, d), lambda i: (0, 0))],
        out_specs=pl.BlockSpec((tm, d), lambda i: (i, 0)),
        out_shape=jax.ShapeDtypeStruct((m, d), bf16),
        compiler_params=_cparams(("parallel",)),
        name="rmsnorm_in",
    )(x, g.reshape(1, d))


def _dot_tb(a, b):
    return lax.dot_general(a, b, (((1,), (1,)), ((), ())), preferred_element_type=f32)


def _mm_gates_kernel(a_ref, w_ref, o_ref):
    o_ref[...] = _dot_tb(a_ref[...], w_ref[...])


def _matmul_gates(h, w_t, tm):
    m, k = h.shape
    n = w_t.shape[0]
    return pl.pallas_call(
        _mm_gates_kernel,
        grid=(m // tm,),
        in_specs=[pl.BlockSpec((tm, k), lambda i: (i, 0)), pl.BlockSpec((n, k), lambda i: (0, 0))],
        out_specs=pl.BlockSpec((tm, n), lambda i: (i, 0)),
        out_shape=jax.ShapeDtypeStruct((m, n), f32),
        compiler_params=_cparams(("parallel",)),
        name="proj_gates",
    )(h, w_t)


def _mm_in_kernel(a_ref, w_ref, wo_ref, wu_ref, o_ref, wob_ref, wub_ref):
    o_ref[...] = _dot_tb(a_ref[...], w_ref[...])
    wob_ref[...] = wo_ref[...].astype(bf16)
    wub_ref[...] = wu_ref[...].astype(bf16)


def _matmul_in(h, w_t, w_out, w_up, layer, tm, tn):
    m, k = h.shape
    n = w_t.shape[0]
    nj, ni = n // tn, m // tm
    d = w_out.shape[1]
    f = w_up.shape[2]
    slab = d // (nj * ni)
    step = lambda j, i: j * ni + i
    side_in = lambda width: pl.BlockSpec((None, slab, width), lambda j, i: (layer, step(j, i), 0))
    side_out = lambda width: pl.BlockSpec((slab, width), lambda j, i: (step(j, i), 0))
    return pl.pallas_call(
        _mm_in_kernel,
        grid=(nj, ni),
        in_specs=[pl.BlockSpec((tm, k), lambda j, i: (i, 0)), pl.BlockSpec((tn, k), lambda j, i: (j, 0)),
                  side_in(d), side_in(f)],
        out_specs=[pl.BlockSpec((tm, tn), lambda j, i: (i, j)), side_out(d), side_out(f)],
        out_shape=[jax.ShapeDtypeStruct((m, n), f32), jax.ShapeDtypeStruct((d, d), bf16),
                   jax.ShapeDtypeStruct((d, f), bf16)],
        compiler_params=_cparams(("parallel", "arbitrary")),
        name="proj_in",
    )(h, w_t, w_out, w_up)


REPACK_LANES = 128


def _repack_slab(win_ref, main_ref, gate_ref):
    n_a = 4 * D_A
    n_ag = n_a + 2 * H_A
    n_bc = 4 * D_B + 4 * D_C
    main_ref[0:n_a, :] = win_ref[0:n_a, :].astype(bf16)
    main_ref[n_a:n_a + n_bc, :] = win_ref[n_ag:n_ag + n_bc, :].astype(bf16)
    gate_ref[...] = jnp.zeros_like(gate_ref)
    gate_ref[LANE_I:LANE_I + 2 * H_A, :] = win_ref[n_a:n_ag, :].astype(bf16)
    gate_ref[LANE_DECAY:LANE_DECAY + 2 * H_C, :] = win_ref[n_ag + n_bc:n_ag + n_bc + 2 * H_C, :].astype(bf16)


def _repack_part(win_ref, main_ref, gate_ref, part, parts):
    n_a = 4 * D_A
    n_ag = n_a + 2 * H_A
    n_bc = 4 * D_B + 4 * D_C
    runs_per_call = (N_PROJ // n_a) // parts
    for c in range(runs_per_call):
        dst = pl.multiple_of((part * runs_per_call + c) * n_a, 16)
        src = pl.multiple_of(dst + jnp.where(dst >= n_a, 2 * H_A, 0), 8)
        main_ref[pl.ds(dst, n_a), :] = win_ref[pl.ds(src, n_a), :].astype(bf16)
    gate_ref[...] = jnp.zeros_like(gate_ref)
    gate_ref[LANE_I:LANE_I + 2 * H_A, :] = win_ref[n_a:n_ag, :].astype(bf16)
    gate_ref[LANE_DECAY:LANE_DECAY + 2 * H_C, :] = win_ref[n_ag + n_bc:n_ag + n_bc + 2 * H_C, :].astype(bf16)


def _mm_up_kernel(repack_every, a_ref, w_ref, wd_ref, *rest):
    if repack_every:
        win_ref, o_ref, wdb_ref, main_ref, gate_ref = rest
        step = pl.program_id(0) * pl.num_programs(1) + pl.program_id(1)
        _repack_part(win_ref, main_ref, gate_ref, step % repack_every, repack_every)
    else:
        o_ref, wdb_ref = rest
    z = jnp.maximum(jnp.dot(a_ref[...], w_ref[...], preferred_element_type=f32), 0.0)
    o_ref[...] = (z * z).astype(o_ref.dtype)
    wdb_ref[...] = wd_ref[...].astype(bf16)


def _matmul_up(h, w, w_down, w_in_t, layer, tm, tn):
    m, k = h.shape
    n = w.shape[1]
    nj, ni = n // tn, m // tm
    steps = nj * ni
    step = lambda j, i: j * ni + i
    with_repack = layer + 1 < w_in_t.shape[0]
    dslab = w_down.shape[1] // steps
    d = w_down.shape[2]
    in_specs = [pl.BlockSpec((tm, k), lambda j, i: (i, 0)), pl.BlockSpec((k, tn), lambda j, i: (0, j)),
                pl.BlockSpec((None, dslab, d), lambda j, i: (layer, step(j, i), 0))]
    out_specs = [pl.BlockSpec((tm, tn), lambda j, i: (i, j)), pl.BlockSpec((dslab, d), lambda j, i: (step(j, i), 0))]
    out_shape = [jax.ShapeDtypeStruct((m, n), bf16), jax.ShapeDtypeStruct(w_down.shape[1:], bf16)]
    args = [h, w, w_down]
    every = 0
    if with_repack:
        n_in, dm = w_in_t.shape[1:]
        every = steps // (dm // REPACK_LANES)
        slab = lambda j, i: step(j, i) // every
        in_specs.append(pl.BlockSpec((None, n_in, REPACK_LANES), lambda j, i: (layer + 1, 0, slab(j, i))))
        out_specs += [pl.BlockSpec((N_PROJ, REPACK_LANES), lambda j, i: (0, slab(j, i))),
                      pl.BlockSpec((128, REPACK_LANES), lambda j, i: (0, slab(j, i)))]
        out_shape += [jax.ShapeDtypeStruct((N_PROJ, dm), bf16), jax.ShapeDtypeStruct((128, dm), bf16)]
        args.append(w_in_t)
    return pl.pallas_call(
        functools.partial(_mm_up_kernel, every),
        grid=(nj, ni),
        in_specs=in_specs,
        out_specs=out_specs,
        out_shape=out_shape,
        compiler_params=_cparams(("parallel", "arbitrary")),
        name="mlp_up",
    )(*args)


def _residual_epilogue(z, x_ref, gpost_ref, gnext_ref, xo_ref, ho_ref, rows=slice(None)):
    x_new = x_ref[rows, :] + _rms(z, gpost_ref[...])
    xo_ref[rows, :] = x_new
    ho_ref[rows, :] = _rms(x_new, gnext_ref[...]).astype(ho_ref.dtype)


OUT_ROW_PARTS = 4


def _mm_out_kernel(n_first, a_ref, b_ref, w_ref, x_ref, gpost_ref, gnext_ref, xo_ref, ho_ref):
    a = jnp.where(pl.program_id(0) < n_first, a_ref[...], b_ref[...])
    part = a.shape[0] // OUT_ROW_PARTS
    for r in range(OUT_ROW_PARTS):
        rows = slice(r * part, (r + 1) * part)
        z = jnp.dot(a[rows], w_ref[...], preferred_element_type=f32)
        _residual_epilogue(z, x_ref, gpost_ref, gnext_ref, xo_ref, ho_ref, rows)


def _matmul_out(y_p, y_s, w, x, g_post, g_next, tm):
    k = y_p.shape[1]
    n_p, n_s = y_p.shape[0] // tm, y_s.shape[0] // tm
    m = (n_p + n_s) * tm
    d = w.shape[1]
    row = lambda i: (i, 0)
    fixed = lambda i: (0, 0)
    return pl.pallas_call(
        functools.partial(_mm_out_kernel, n_p),
        grid=(n_p + n_s,),
        in_specs=[pl.BlockSpec((tm, k), lambda i: (jnp.minimum(i, n_p - 1), 0)),
                  pl.BlockSpec((tm, k), lambda i: (jnp.maximum(i - n_p, 0), 0)),
                  pl.BlockSpec((k, d), fixed), pl.BlockSpec((tm, d), row),
                  pl.BlockSpec((1, d), fixed), pl.BlockSpec((1, d), fixed)],
        out_specs=[pl.BlockSpec((tm, d), row), pl.BlockSpec((tm, d), row)],
        out_shape=[jax.ShapeDtypeStruct((m, d), f32), jax.ShapeDtypeStruct((m, d), bf16)],
        compiler_params=_cparams(("parallel",)),
        name="proj_out",
    )(y_p, y_s, w, x, g_post.reshape(1, d), g_next.reshape(1, d))


def _mm_down_kernel(a_ref, w_ref, x_ref, gpost_ref, gnext_ref, xo_ref, ho_ref):
    kk = pl.program_id(1)

    @pl.when(kk == 0)
    def _():
        xo_ref[...] = jnp.zeros_like(xo_ref)

    xo_ref[...] += jnp.dot(a_ref[...], w_ref[...], preferred_element_type=f32)

    @pl.when(kk == pl.num_programs(1) - 1)
    def _():
        _residual_epilogue(xo_ref[...], x_ref, gpost_ref, gnext_ref, xo_ref, ho_ref)


def _matmul_down(u, w, x, g_post, g_next, tm, tk):
    m, k = u.shape
    d = w.shape[1]
    row = lambda i, kk: (i, 0)
    fixed = lambda i, kk: (0, 0)
    return pl.pallas_call(
        _mm_down_kernel,
        grid=(m // tm, k // tk),
        in_specs=[pl.BlockSpec((tm, tk), lambda i, kk: (i, kk)),
                  pl.BlockSpec((tk, d), lambda i, kk: (kk, 0)),
                  pl.BlockSpec((tm, d), row), pl.BlockSpec((1, d), fixed), pl.BlockSpec((1, d), fixed)],
        out_specs=[pl.BlockSpec((tm, d), row), pl.BlockSpec((tm, d), row)],
        out_shape=[jax.ShapeDtypeStruct((m, d), f32), jax.ShapeDtypeStruct((m, d), bf16)],
        compiler_params=_cparams(("parallel", "arbitrary")),
        name="mlp_down",
    )(u, w, x, g_post.reshape(1, d), g_next.reshape(1, d))


def _dot(a, b):
    return jnp.dot(a, b, preferred_element_type=f32)


def _dot_ta(a, b):
    return lax.dot_general(a, b, (((0,), (0,)), ((), ())), preferred_element_type=f32)


def _lane_sum(x):
    return jnp.sum(x, axis=-1, keepdims=True)


def _softplus(z):
    return jnp.maximum(z, 0.0) + jnp.log1p(jnp.exp(-jnp.abs(z)))


def _gate_transform(gt, gp):
    z = gt + gp[0:1, :]
    capped = GATE_SOFTCAP * jnp.tanh(z / GATE_SOFTCAP)
    log_f = -_softplus(-capped)
    decay = -jnp.exp(gp[1:2, :]) * _softplus(z)
    beta = jax.nn.sigmoid(gt)
    return capped, log_f, decay, beta


def _head_norm(x, gain, center):
    if center:
        x = x - jnp.mean(x, axis=-1, keepdims=True)
    return x * lax.rsqrt(jnp.mean(x * x, axis=-1, keepdims=True) + NORM_EPS) * gain


def _l2norm(x):
    return x * lax.rsqrt(_lane_sum(x * x) + NORM_EPS)


def _silu(x):
    return x * jax.nn.sigmoid(x)


def _rotary(x, cos, sin_signed):
    return x * cos + pltpu.roll(x, HEAD_DIM // 2, 1) * sin_signed


def _retention_log_gamma():
    return np.log1p(-np.exp2(-5.0 - np.arange(H_B, dtype=np.float64)))


def _rope_tables(pos):
    half = HEAD_DIM // 2
    inv = ROPE_BASE ** (-np.arange(half, dtype=np.float64) / half)
    ang = np.asarray(pos, dtype=np.float64)[:, None] * inv[None, :]
    cos = np.concatenate([np.cos(ang), np.cos(ang)], axis=-1)
    sin = np.concatenate([-np.sin(ang), np.sin(ang)], axis=-1)
    return cos.astype(np.float32), sin.astype(np.float32)


def _scan_rows(x, row, length, op, fill):
    s = 1
    while s < length:
        x = op(x, jnp.where(row >= s, pltpu.roll(x, s, 0), fill))
        s *= 2
    return x


def _rows_to_lanes(x, length):
    if length < 128:
        x = jnp.concatenate([x, jnp.zeros((128 - length, 128), x.dtype)], axis=0)
    return x.T[:, :length]


def _prompt_mixer_kernel(L, NS, *refs):
    proj_refs, gt_refs = refs[:NS], refs[NS:2 * NS]
    (cos_ref, sin_ref, rd_ref, rq_ref, rk_ref, gp_ref, cw_ref, na_ref, nb_ref, nc_ref,
     y_ref, c_ref, n_ref, m_ref, r_ref, g_ref, conv_ref, ext_ref) = refs[2 * NS:]
    c = pl.program_id(1)

    @pl.when(c == 0)
    def _():
        c_ref[...] = jnp.zeros_like(c_ref)
        n_ref[...] = jnp.zeros_like(n_ref)
        m_ref[...] = jnp.zeros_like(m_ref)
        r_ref[...] = jnp.zeros_like(r_ref)
        g_ref[...] = jnp.zeros_like(g_ref)
        ext_ref[:, 0:8, :] = jnp.zeros((NS, 8, 3 * D_C), f32)

    row = lax.broadcasted_iota(jnp.int32, (L, 128), 0)
    lane = lax.broadcasted_iota(jnp.int32, (L, 128), 1)
    ri = lax.broadcasted_iota(jnp.int32, (L, L), 0)
    ci = lax.broadcasted_iota(jnp.int32, (L, L), 1)
    incl = ri >= ci
    strict = ri > ci
    head_lane = lane < H_A
    seqs = range(NS)

    def gate_block(s):
        g = types.SimpleNamespace()
        capped, log_f, decay, g.beta = _gate_transform(gt_refs[s][...], gp_ref[...])
        ig = jnp.where(head_lane, capped, 0.0)
        lf = jnp.where(head_lane, pltpu.roll(log_f, 128 - LANE_F, 1), 0.0)
        mp = m_ref[s]
        F = _scan_rows(lf, row, L, jnp.add, 0.0)
        m = F + jnp.maximum(mp, _scan_rows(ig - F, row, L, jnp.maximum, NEG_BIG))
        m_new = m[L - 1:L, :]
        f_last = F[L - 1:L, :]
        g.a_rows = F - m
        g.inter = jnp.exp(F + mp - m)
        g.inv_floor = jnp.exp(-m)
        g.wl = jnp.exp(ig + f_last - F - m_new)
        g.dec = jnp.exp(f_last + mp - m_new)
        m_ref[s] = m_new
        g.G = _scan_rows(decay, row, L, jnp.add, 0.0)
        g.b_lanes = g.g_lanes = _rows_to_lanes(jnp.where(head_lane, ig - F, g.G), L)
        g.gam = jnp.exp(g.G)
        g_last = g.G[L - 1:L, :]
        g.k_decay = jnp.exp(g_last - g.G)
        g.s_decay = jnp.exp(g_last)
        return g

    gs = [gate_block(s) for s in seqs]

    def mlstm_head(s, h):
        proj_ref, g = proj_refs[s], gs[s]
        q = proj_ref[:, COL_A + h * HEAD_DIM:COL_A + (h + 1) * HEAD_DIM]
        k = proj_ref[:, COL_A + D_A + h * HEAD_DIM:COL_A + D_A + (h + 1) * HEAD_DIM] * QK_SCALE
        v = proj_ref[:, COL_A + 2 * D_A + h * HEAD_DIM:COL_A + 2 * D_A + (h + 1) * HEAD_DIM]
        og = proj_ref[:, COL_A + 3 * D_A + h * HEAD_DIM:COL_A + 3 * D_A + (h + 1) * HEAD_DIM]
        logw = g.a_rows[:, h:h + 1] + g.b_lanes[h:h + 1, :]
        sc = _dot_tb(q, k) * jnp.exp(jnp.where(incl, logw, NEG_BIG))
        c_old = c_ref[s, h]
        n_old = n_ref[s, h:h + 1, :]
        inter_h = g.inter[:, h:h + 1]
        num = inter_h * _dot(q, c_old) + _dot(sc, v)
        den = inter_h * _lane_sum(q * n_old) + _lane_sum(sc)
        hh = num / jnp.maximum(jnp.abs(den), g.inv_floor[:, h:h + 1])
        kw = k * g.wl[:, h:h + 1]
        dec_h = g.dec[:, h:h + 1]
        c_ref[s, h] = dec_h * c_old + _dot_ta(kw, v)
        n_ref[s, h:h + 1, :] = dec_h * n_old + jnp.sum(kw, axis=0, keepdims=True)
        ya = jax.nn.sigmoid(og) * _head_norm(hh, na_ref[h:h + 1, :], False)
        y_ref[s, :, h * HEAD_DIM:(h + 1) * HEAD_DIM] = ya.astype(y_ref.dtype)

    lg = _retention_log_gamma()

    def retention_head(s, h):
        proj_ref = proj_refs[s]
        cos = cos_ref[...]
        sin = sin_ref[...]
        q = proj_ref[:, COL_B + h * HEAD_DIM:COL_B + (h + 1) * HEAD_DIM]
        k = proj_ref[:, COL_B + D_B + h * HEAD_DIM:COL_B + D_B + (h + 1) * HEAD_DIM]
        v = proj_ref[:, COL_B + 2 * D_B + h * HEAD_DIM:COL_B + 2 * D_B + (h + 1) * HEAD_DIM]
        gb = proj_ref[:, COL_B + 3 * D_B + h * HEAD_DIM:COL_B + 3 * D_B + (h + 1) * HEAD_DIM]
        qr = _rotary(q, cos, sin)
        kr = _rotary(k, cos, sin) * QK_SCALE
        s_old = r_ref[s, h]
        inner = _dot_tb(qr, kr) * rd_ref[h]
        o = _dot(inner, v) + rq_ref[h] * _dot(qr, s_old)
        r_ref[s, h] = float(np.exp(L * lg[h])) * s_old + _dot_ta(kr * rk_ref[h], v)
        yb = _silu(gb) * _head_norm(o, nb_ref[h:h + 1, :], True)
        y_ref[s, :, D_A + h * HEAD_DIM:D_A + (h + 1) * HEAD_DIM] = yb.astype(y_ref.dtype)

    fillers = [functools.partial(mlstm_head, s, h) for h in range(H_A) for s in seqs]
    fillers += [functools.partial(retention_head, s, h) for h in range(H_B) for s in seqs]
    def emit_fillers(count):
        for _ in range(min(count, len(fillers))):
            fillers.pop(0)()

    emit_fillers(LEAD_FILLERS)
    n_stages = max(int(math.log2(min(INV_BLOCK, L))) - 1, 0) + int(math.log2(L // min(INV_BLOCK, L)))
    per_stage = -(-len(fillers) // max(n_stages, 1))

    for s in seqs:
        ext_ref[s, 8:8 + L, :] = proj_refs[s][:, COL_C:COL_C + 3 * D_C]

    def conv_block(s, col):
        acc = cw_ref[CONV_W - 1:CONV_W, col:col + HEAD_DIM] * ext_ref[s, 8:8 + L, col:col + HEAD_DIM]
        for w in range(CONV_W - 1):
            off = 8 - (CONV_W - 1) + w
            acc = acc + cw_ref[w:w + 1, col:col + HEAD_DIM] * ext_ref[s, off:off + L, col:col + HEAD_DIM]
        return _silu(acc)

    pairs = [(s, h) for h in range(H_C) for s in seqs]
    heads = range(len(pairs))
    dcol = [LANE_DECAY + h for _, h in pairs]
    gam = [gs[s].gam for s, _ in pairs]
    beta_c = [gs[s].beta[:, LANE_BETA + h:LANE_BETA + h + 1] for s, h in pairs]
    q = [_l2norm(conv_block(s, h * HEAD_DIM)) * QK_SCALE for s, h in pairs]
    k = [_l2norm(conv_block(s, D_C + h * HEAD_DIM)) for s, h in pairs]
    v = [conv_block(s, 2 * D_C + h * HEAD_DIM) for s, h in pairs]
    dec_in = [jnp.exp(jnp.where(incl, gs[s].G[:, LANE_DECAY + h:LANE_DECAY + h + 1]
                                - gs[s].g_lanes[LANE_DECAY + h:LANE_DECAY + h + 1, :], NEG_BIG)) for s, h in pairs]
    a_mat = [jnp.where(strict, dec_in[h], 0.0) * beta_c[h] * _dot_tb(k[h], k[h]) for h in heads]
    qk = [_dot_tb(q[h], k[h]) * dec_in[h] for h in heads]
    blk = min(INV_BLOCK, L)
    same = (ri // blk) == (ci // blk)
    pw = [jnp.where(same, -a, 0.0) for a in a_mat]
    e_mat = list(pw)
    span = 1
    while 2 * span < blk:
        pw = [_dot(p, p) for p in pw]
        e_mat = [e_mat[h] + pw[h] + _dot(e_mat[h], pw[h]) for h in heads]
        emit_fillers(per_stage)
        span *= 2
    while blk < L:
        wider = (ri // (2 * blk)) == (ci // (2 * blk))
        a_off = [jnp.where(wider & jnp.logical_not(same), a, 0.0) for a in a_mat]
        low = [a_off[h] + _dot(e_mat[h], a_off[h]) for h in heads]
        e_mat = [e_mat[h] - low[h] - _dot(low[h], e_mat[h]) for h in heads]
        emit_fillers(per_stage)
        same = wider
        blk *= 2
    rhs = [jnp.concatenate([(beta_c[h] * gam[h][:, dcol[h]:dcol[h] + 1]) * k[h], beta_c[h] * v[h]], axis=1)
           for h in heads]
    sol = [rhs[h] + _dot(e_mat[h], rhs[h]) for h in heads]
    emit_fillers(len(fillers))
    s_old = [g_ref[s, hd] for s, hd in pairs]
    both = [_dot(jnp.concatenate([sol[h][:, :HEAD_DIM], q[h]], axis=0), s_old[h]) for h in heads]
    u = [sol[h][:, HEAD_DIM:] - both[h][:L] for h in heads]
    o = [gam[h][:, dcol[h]:dcol[h] + 1] * both[h][L:] + _dot(qk[h], u[h]) for h in heads]
    for h, (s, hd) in enumerate(pairs):
        g_ref[s, hd] = (gs[s].s_decay[:, dcol[h]:dcol[h] + 1] * s_old[h]
                        + _dot_ta(k[h] * gs[s].k_decay[:, dcol[h]:dcol[h] + 1], u[h]))
    for h, (s, hd) in enumerate(pairs):
        gz = proj_refs[s][:, COL_C + 3 * D_C + hd * HEAD_DIM:COL_C + 3 * D_C + (hd + 1) * HEAD_DIM]
        yc = _silu(gz) * _head_norm(o[h], nc_ref[...], False)
        y_ref[s, :, D_A + D_B + hd * HEAD_DIM:D_A + D_B + (hd + 1) * HEAD_DIM] = yc.astype(y_ref.dtype)

    for s in seqs:
        ext_ref[s, 0:8, :] = ext_ref[s, L:L + 8, :]

    @pl.when(c == pl.num_programs(1) - 1)
    def _():
        for s in seqs:
            conv_ref[s] = ext_ref[s, 8 + L - (CONV_W - 1):8 + L, :]


def _prompt_tables(seq, L):
    lg = _retention_log_gamma()
    i = np.arange(L, dtype=np.float64)
    diff = i[:, None] - i[None, :]
    rd = np.where(diff >= 0, np.exp(np.maximum(diff, 0.0) * lg[:, None, None]), 0.0)
    rq = np.broadcast_to(np.exp((i + 1.0) * lg[:, None])[..., None], (H_B, L, HEAD_DIM))
    rk = np.broadcast_to(np.exp((L - 1.0 - i) * lg[:, None])[..., None], (H_B, L, HEAD_DIM))
    cos, sin = _rope_tables(np.arange(seq))
    return (jnp.asarray(cos), jnp.asarray(sin), jnp.asarray(rd, f32), jnp.asarray(rq, f32), jnp.asarray(rk, f32))


def _prompt_mixers(proj, gates, tables, gp, cw, na, nb, nc, batch, seq, L, NS):
    nchunk = seq // L
    cos, sin, rd, rq, rk = tables
    full = lambda shape: pl.BlockSpec(shape, lambda b, c: (0,) * len(shape))
    state4 = lambda heads: pl.BlockSpec((NS, heads, HEAD_DIM, HEAD_DIM), lambda b, c: (b, 0, 0, 0))
    rows_of = lambda s, width: pl.BlockSpec((L, width), lambda b, c: ((b * NS + s) * nchunk + c, 0))
    outs = pl.pallas_call(
        functools.partial(_prompt_mixer_kernel, L, NS),
        grid=(batch // NS, nchunk),
        in_specs=[rows_of(s, N_PROJ) for s in range(NS)] + [rows_of(s, 128) for s in range(NS)] + [
            pl.BlockSpec((L, HEAD_DIM), lambda b, c: (c, 0)),
            pl.BlockSpec((L, HEAD_DIM), lambda b, c: (c, 0)),
            full((H_B, L, L)), full((H_B, L, HEAD_DIM)), full((H_B, L, HEAD_DIM)),
            full((8, 128)), full((CONV_W, 3 * D_C)),
            full((H_A, HEAD_DIM)), full((H_B, HEAD_DIM)), full((1, HEAD_DIM)),
        ],
        out_specs=[
            pl.BlockSpec((None, NS, None, L, D_MODEL), lambda b, c: (b, 0, c, 0, 0)),
            state4(H_A),
            pl.BlockSpec((NS, H_A, HEAD_DIM), lambda b, c: (b, 0, 0)),
            pl.BlockSpec((NS, 1, 128), lambda b, c: (b, 0, 0)),
            state4(H_B),
            state4(H_C),
            pl.BlockSpec((NS, CONV_W - 1, 3 * D_C), lambda b, c: (b, 0, 0)),
        ],
        out_shape=[
            jax.ShapeDtypeStruct((batch // NS, NS, nchunk, L, D_MODEL), bf16),
            jax.ShapeDtypeStruct((batch, H_A, HEAD_DIM, HEAD_DIM), f32),
            jax.ShapeDtypeStruct((batch, H_A, HEAD_DIM), f32),
            jax.ShapeDtypeStruct((batch, 1, 128), f32),
            jax.ShapeDtypeStruct((batch, H_B, HEAD_DIM, HEAD_DIM), f32),
            jax.ShapeDtypeStruct((batch, H_C, HEAD_DIM, HEAD_DIM), f32),
            jax.ShapeDtypeStruct((batch, CONV_W - 1, 3 * D_C), f32),
        ],
        scratch_shapes=[pltpu.VMEM((NS, L + 8, 3 * D_C), f32)],
        compiler_params=_cparams(("parallel", "arbitrary")),
        name="prompt_mixers",
    )(*([proj] * NS + [gates] * NS), cos, sin, rd, rq, rk, gp, cw, na, nb, nc)
    return [outs[0].reshape(batch * seq, D_MODEL)] + list(outs[1:])


def _sample_mixer_kernel(T, BB, layer_first, *refs):
    (proj_ref, gt_ref, cos_ref, sin_ref, rt_ref, gp_ref, cw_ref, na_ref, nb_ref, nc_ref,
     c0_ref, nrep_ref, mrep_ref, r0_ref, g0_ref, conv0_ref) = refs[:16]
    rest = refs[16 + (0 if layer_first else 6):]
    (y_ref, c_ref, nout_ref, mout_ref, r_ref, g_ref, conv_ref, cv_ref, sa_ref, ext_ref) = rest
    R = T * BB
    row = lax.broadcasted_iota(jnp.int32, (R, 128), 0)
    lane = lax.broadcasted_iota(jnp.int32, (R, 128), 1)
    t = lax.rem(row, T)
    tcol = t[:, 0:1]

    def shift(x, s):
        return x if s == 0 else pltpu.roll(x, s, 0)

    def seg_scan(x, op, fill):
        s = 1
        while s < T:
            x = op(x, jnp.where(t >= s, shift(x, s), fill))
            s *= 2
        return x

    def last_rep(x):
        x_last = jnp.where(t == T - 1, x, 0.0)
        out = x_last
        for s in range(1, T):
            out = out + pltpu.roll(x_last, R - s, 0)
        return out

    capped, log_f, decay, beta = _gate_transform(gt_ref[...], gp_ref[...])

    head_lane = lane < H_A
    ig = jnp.where(head_lane, capped, 0.0)
    lf = jnp.where(head_lane, pltpu.roll(log_f, 128 - LANE_F, 1), 0.0)
    mp = mrep_ref[...]
    F = seg_scan(lf, jnp.add, 0.0)
    m = F + jnp.maximum(mp, seg_scan(ig - F, jnp.maximum, NEG_BIG))
    m_new = last_rep(m)
    f_last = last_rep(F)
    inter = jnp.exp(F + mp - m)
    inv_floor = jnp.exp(-m)
    wl = jnp.exp(ig + f_last - F - m_new)
    dec = jnp.exp(f_last + mp - m_new)
    mout_ref[...] = m_new
    pw = [jnp.where(t >= s, jnp.exp(F - shift(F, s) + shift(ig, s) - m), 0.0) for s in range(T)]

    def seq(x, b):
        return x[b * T:(b + 1) * T]

    slot_a, slot_b, slot_w, slot_q = 0, H_A, H_A + H_B, H_A + H_B + H_C

    qa = [proj_ref[:, COL_A + h * HEAD_DIM:COL_A + (h + 1) * HEAD_DIM] for h in range(H_A)]
    for h in range(H_A):
        for b in range(BB):
            sa_ref[slot_a + h, b * T:(b + 1) * T, :] = _dot(seq(qa[h], b), c0_ref[b, h])
    cos = cos_ref[...]
    sin = sin_ref[...]
    lg = _retention_log_gamma()
    qr = [_rotary(proj_ref[:, COL_B + h * HEAD_DIM:COL_B + (h + 1) * HEAD_DIM], cos, sin) for h in range(H_B)]
    kr = [_rotary(proj_ref[:, COL_B + D_B + h * HEAD_DIM:COL_B + D_B + (h + 1) * HEAD_DIM], cos, sin) * QK_SCALE
          for h in range(H_B)]
    for h in range(H_B):
        for b in range(BB):
            sa_ref[slot_b + h, b * T:(b + 1) * T, :] = _dot(seq(qr[h], b), r0_ref[b, h])

    ka = [proj_ref[:, COL_A + D_A + h * HEAD_DIM:COL_A + D_A + (h + 1) * HEAD_DIM] * QK_SCALE for h in range(H_A)]
    va = [proj_ref[:, COL_A + 2 * D_A + h * HEAD_DIM:COL_A + 2 * D_A + (h + 1) * HEAD_DIM] for h in range(H_A)]
    n_old = [nrep_ref[:, h * HEAD_DIM:(h + 1) * HEAD_DIM] for h in range(H_A)]
    kw = [ka[h] * wl[:, h:h + 1] for h in range(H_A)]
    for h in range(H_A):
        for b in range(BB):
            dec_bh = dec[b * T:b * T + 1, h:h + 1]
            c_ref[b, h] = dec_bh * c0_ref[b, h] + _dot_ta(seq(kw[h], b), seq(va[h], b))
        nout_ref[:, h * HEAD_DIM:(h + 1) * HEAD_DIM] = dec[:, h:h + 1] * n_old[h] + seg_scan(kw[h], jnp.add, 0.0)
    vb = [proj_ref[:, COL_B + 2 * D_B + h * HEAD_DIM:COL_B + 2 * D_B + (h + 1) * HEAD_DIM] for h in range(H_B)]
    for h in range(H_B):
        kd = kr[h] * rt_ref[:, H_B + h:H_B + h + 1]
        for b in range(BB):
            r_ref[b, h] = float(np.exp(T * lg[h])) * r0_ref[b, h] + _dot_ta(seq(kd, b), seq(vb[h], b))

    for b in range(BB):
        ext_ref[b, 0:CONV_W - 1, :] = conv0_ref[b]
        ext_ref[b, CONV_W - 1:CONV_W - 1 + T, :] = proj_ref[b * T:(b + 1) * T, COL_C:COL_C + 3 * D_C]
        acc = cw_ref[0:1, :] * ext_ref[b, 0:T, :]
        for w in range(1, CONV_W):
            acc = acc + cw_ref[w:w + 1, :] * ext_ref[b, w:w + T, :]
        cv_ref[b * T:(b + 1) * T, :] = _silu(acc)
        conv_ref[b] = ext_ref[b, T:T + CONV_W - 1, :]

    G = seg_scan(decay, jnp.add, 0.0)
    gam = jnp.exp(G)
    g_last = last_rep(G)
    k_decay = jnp.exp(g_last - G)
    s_decay = jnp.exp(g_last)
    dshift = [None] + [jnp.where(t >= s, jnp.exp(G - shift(G, s)), 0.0) for s in range(1, T)]

    heads = range(H_C)
    dcol = [LANE_DECAY + h for h in heads]
    beta_c = [beta[:, LANE_BETA + h:LANE_BETA + h + 1] for h in heads]
    qg = [_l2norm(cv_ref[:, h * HEAD_DIM:(h + 1) * HEAD_DIM]) * QK_SCALE for h in heads]
    kg = [_l2norm(cv_ref[:, D_C + h * HEAD_DIM:D_C + (h + 1) * HEAD_DIM]) for h in heads]
    vg = [cv_ref[:, 2 * D_C + h * HEAD_DIM:2 * D_C + (h + 1) * HEAD_DIM] for h in heads]
    a_sub = [[None] + [beta_c[h] * dshift[s][:, dcol[h]:dcol[h] + 1] * _lane_sum(kg[h] * shift(kg[h], s))
                       for s in range(1, T)] for h in heads]
    rhs_w = [(beta_c[h] * gam[:, dcol[h]:dcol[h] + 1]) * kg[h] for h in heads]
    rhs_u = [beta_c[h] * vg[h] for h in heads]
    w_sol, u_sol = list(rhs_w), list(rhs_u)
    for i in range(1, T):
        for h in heads:
            upd_w = a_sub[h][1] * shift(w_sol[h], 1)
            upd_u = a_sub[h][1] * shift(u_sol[h], 1)
            for s in range(2, i + 1):
                upd_w = upd_w + a_sub[h][s] * shift(w_sol[h], s)
                upd_u = upd_u + a_sub[h][s] * shift(u_sol[h], s)
            w_sol[h] = jnp.where(t == i, rhs_w[h] - upd_w, w_sol[h])
            u_sol[h] = jnp.where(t == i, rhs_u[h] - upd_u, u_sol[h])
    for h in heads:
        for b in range(BB):
            both = _dot(jnp.concatenate([seq(w_sol[h], b), seq(qg[h], b)], axis=0), g0_ref[b, h])
            sa_ref[slot_w + h, b * T:(b + 1) * T, :] = both[:T]
            sa_ref[slot_q + h, b * T:(b + 1) * T, :] = both[T:]

    ug = [u_sol[h] - sa_ref[slot_w + h] for h in heads]
    for h in heads:
        kd = kg[h] * k_decay[:, dcol[h]:dcol[h] + 1]
        for b in range(BB):
            g_ref[b, h] = (s_decay[b * T:b * T + 1, dcol[h]:dcol[h] + 1] * g0_ref[b, h]
                           + _dot_ta(seq(kd, b), seq(ug[h], b)))

    sc_a = [[_lane_sum(qa[h] * shift(ka[h], s)) * pw[s][:, h:h + 1] for s in range(T)] for h in range(H_A)]
    qn = [_lane_sum(qa[h] * n_old[h]) for h in range(H_A)]
    sc_b = [[jnp.where(tcol >= s, _lane_sum(qr[h] * shift(kr[h], s)) * float(np.exp(s * lg[h])), 0.0)
             for s in range(T)] for h in range(H_B)]
    sc_c = [[_lane_sum(qg[h] * shift(kg[h], s)) * (1.0 if s == 0 else dshift[s][:, dcol[h]:dcol[h] + 1])
             for s in range(T)] for h in heads]
    hid_a, hid_b, hid_c = [], [], []
    for h in range(H_A):
        inter_h = inter[:, h:h + 1]
        num = inter_h * sa_ref[slot_a + h]
        den = inter_h * qn[h]
        for s in range(T):
            num = num + sc_a[h][s] * shift(va[h], s)
            den = den + sc_a[h][s]
        hid_a.append(num / jnp.maximum(jnp.abs(den), inv_floor[:, h:h + 1]))
    for h in range(H_B):
        o = rt_ref[:, h:h + 1] * sa_ref[slot_b + h]
        for s in range(T):
            o = o + sc_b[h][s] * shift(vb[h], s)
        hid_b.append(o)
    for h in heads:
        o = gam[:, dcol[h]:dcol[h] + 1] * sa_ref[slot_q + h]
        for s in range(T):
            o = o + sc_c[h][s] * shift(ug[h], s)
        hid_c.append(o)
    mean_b = [jnp.mean(x, axis=-1, keepdims=True) for x in hid_b]
    hid_b = [x - mu for x, mu in zip(hid_b, mean_b)]
    hidden = hid_a + hid_b + hid_c
    inv_rms = [lax.rsqrt(jnp.mean(x * x, axis=-1, keepdims=True) + NORM_EPS) for x in hidden]
    for h in range(H_A):
        og = proj_ref[:, COL_A + 3 * D_A + h * HEAD_DIM:COL_A + 3 * D_A + (h + 1) * HEAD_DIM]
        ya = jax.nn.sigmoid(og) * (hidden[h] * inv_rms[h] * na_ref[h:h + 1, :])
        y_ref[:, h * HEAD_DIM:(h + 1) * HEAD_DIM] = ya.astype(y_ref.dtype)
    for h in range(H_B):
        gb = proj_ref[:, COL_B + 3 * D_B + h * HEAD_DIM:COL_B + 3 * D_B + (h + 1) * HEAD_DIM]
        yb = _silu(gb) * (hidden[H_A + h] * inv_rms[H_A + h] * nb_ref[h:h + 1, :])
        y_ref[:, D_A + h * HEAD_DIM:D_A + (h + 1) * HEAD_DIM] = yb.astype(y_ref.dtype)
    for h in heads:
        gz = proj_ref[:, COL_C + 3 * D_C + h * HEAD_DIM:COL_C + 3 * D_C + (h + 1) * HEAD_DIM]
        yc = _silu(gz) * (hidden[H_A + H_B + h] * inv_rms[H_A + H_B + h] * nc_ref[...])
        y_ref[:, D_A + D_B + h * HEAD_DIM:D_A + D_B + (h + 1) * HEAD_DIM] = yc.astype(y_ref.dtype)


def _sample_tables(T, BB):
    lg = _retention_log_gamma()
    tt = np.arange(T, dtype=np.float64)
    rt = np.zeros((T, 128), np.float64)
    rt[:, 0:H_B] = np.exp((tt[:, None] + 1.0) * lg[None, :])
    rt[:, H_B:2 * H_B] = np.exp((T - 1.0 - tt[:, None]) * lg[None, :])
    cos, sin = _rope_tables(PAST_LEN + np.arange(T))
    rep = lambda a: jnp.asarray(np.tile(a, (BB, 1)), f32)
    return rep(cos), rep(sin), rep(rt)


def _sample_mixers(layer, proj, gates, tables, gp, cw, na, nb, nc, states, prev_out, batch, T, row0, BB):
    c0, nrep, mrep, r0, g0, conv0 = states
    R = T * BB
    nblk = batch // BB
    blk0 = row0 // R
    cos, sin, rt = tables
    full = lambda shape: pl.BlockSpec(shape, lambda i: (0,) * len(shape))
    st4 = lambda heads: pl.BlockSpec((None, BB, heads, HEAD_DIM, HEAD_DIM), lambda i: (layer, i, 0, 0, 0))
    rows = lambda width: pl.BlockSpec((None, R, width), lambda i: (layer, i, 0))
    convspec = pl.BlockSpec((None, BB, CONV_W - 1, 3 * D_C), lambda i: (layer, i, 0, 0))
    anyspec = pl.BlockSpec(memory_space=pl.ANY)
    in_specs = [
        pl.BlockSpec((R, N_PROJ), lambda i: (blk0 + i, 0)),
        pl.BlockSpec((R, 128), lambda i: (blk0 + i, 0)),
        full((R, HEAD_DIM)), full((R, HEAD_DIM)), full((R, 128)),
        full((8, 128)), full((CONV_W, 3 * D_C)),
        full((H_A, HEAD_DIM)), full((H_B, HEAD_DIM)), full((1, HEAD_DIM)),
        st4(H_A), rows(D_A), rows(128), st4(H_B), st4(H_C), convspec,
    ]
    args = [proj, gates, cos, sin, rt, gp, cw, na, nb, nc, c0, nrep, mrep, r0, g0, conv0]
    out_shape = [
        jax.ShapeDtypeStruct((batch * T, D_MODEL), bf16),
        jax.ShapeDtypeStruct(c0.shape, f32),
        jax.ShapeDtypeStruct(nrep.shape, f32),
        jax.ShapeDtypeStruct(mrep.shape, f32),
        jax.ShapeDtypeStruct(r0.shape, f32),
        jax.ShapeDtypeStruct(g0.shape, f32),
        jax.ShapeDtypeStruct(conv0.shape, f32),
    ]
    out_specs = [pl.BlockSpec((R, D_MODEL), lambda i: (i, 0)),
                 st4(H_A), rows(D_A), rows(128), st4(H_B), st4(H_C), convspec]
    first = prev_out is None
    aliases = {}
    if not first:
        in_specs += [anyspec] * 6
        args += list(prev_out)
        aliases = {16 + j: 1 + j for j in range(6)}
    return pl.pallas_call(
        functools.partial(_sample_mixer_kernel, T, BB, first),
        grid=(nblk,),
        in_specs=in_specs,
        out_specs=out_specs,
        out_shape=out_shape,
        input_output_aliases=aliases,
        scratch_shapes=[pltpu.VMEM((R, 3 * D_C), f32), pltpu.VMEM((H_A + H_B + 2 * H_C, R, HEAD_DIM), f32),
                        pltpu.VMEM((BB, 8, 3 * D_C), f32)],
        compiler_params=_cparams(("parallel",)),
        name="sample_mixers",
    )(*args)


def _repack_kernel(win_ref, main_ref, gate_ref):
    _repack_slab(win_ref, main_ref, gate_ref)


def _repack_w_in(w_in_t, layer):
    _, n_in, dm = w_in_t.shape
    return pl.pallas_call(
        _repack_kernel,
        grid=(dm // REPACK_LANES,),
        in_specs=[pl.BlockSpec((None, n_in, REPACK_LANES), lambda i: (layer, 0, i))],
        out_specs=[pl.BlockSpec((N_PROJ, REPACK_LANES), lambda i: (0, i)),
                   pl.BlockSpec((128, REPACK_LANES), lambda i: (0, i))],
        out_shape=[jax.ShapeDtypeStruct((N_PROJ, dm), bf16), jax.ShapeDtypeStruct((128, dm), bf16)],
        compiler_params=_cparams(("parallel",)),
        name="repack_w_in",
    )(w_in_t)


def _gate_params(gate_bias, dt_bias, a_log):
    depth = gate_bias.shape[0]
    gp = jnp.zeros((depth, 8, 128), f32)
    gp = gp.at[:, 0, LANE_I:LANE_I + 2 * H_A].set(gate_bias)
    gp = gp.at[:, 0, LANE_DECAY:LANE_DECAY + H_C].set(dt_bias)
    gp = gp.at[:, 1, LANE_DECAY:LANE_DECAY + H_C].set(a_log)
    return gp


def kernel(x_prompt, x_sample, state_mlstm_C, state_mlstm_n, state_mlstm_m, state_ret_S, state_gdn_S,
           state_gdn_conv, norm_mix_pre, norm_mix_post, norm_mlp_pre, norm_mlp_post, w_in, mlstm_gate_bias,
           gdn_conv_w, gdn_A_log, gdn_dt_bias, norm_mlstm, norm_ret, norm_gdn, w_out, w_up, w_down):
    bp, tp, d = x_prompt.shape
    bs, ts, _ = x_sample.shape
    depth = w_in.shape[0]
    rows_p = bp * tp
    rows_s = bs * ts
    rows = rows_p + rows_s

    x = jnp.concatenate([x_prompt.reshape(rows_p, d), x_sample.reshape(rows_s, d)], axis=0)
    w_in_t = jnp.swapaxes(w_in, 1, 2)
    w_in_p, w_gate_p = _repack_w_in(w_in_t, 0)
    gp = _gate_params(mlstm_gate_bias, gdn_dt_bias, gdn_A_log)
    na = norm_mlstm.reshape(depth, H_A, HEAD_DIM)
    nb = norm_ret.reshape(depth, H_B, HEAD_DIM)
    nc = norm_gdn.reshape(depth, 1, HEAD_DIM)

    nrep = jnp.repeat(state_mlstm_n.reshape(depth, bs, D_A), ts, axis=1)
    mrep = jnp.pad(jnp.repeat(state_mlstm_m, ts, axis=1), ((0, 0), (0, 0), (0, 128 - H_A)))
    s_states = (state_mlstm_C, nrep, mrep, state_ret_S, state_gdn_S, state_gdn_conv)

    p_tables = _prompt_tables(tp, PROMPT_CHUNK)
    s_tables = _sample_tables(ts, SAMPLE_BLOCK)

    tm = rows // 8
    h = _rmsnorm_rows(x, norm_mix_pre[0], tm // 2)
    p_states = []
    s_out = None
    for l in range(depth):
        proj, w_out_b, w_up_b = _matmul_in(h, w_in_p, w_out, w_up, l, tm, 2048)
        gates = _matmul_gates(h, w_gate_p, tm)
        outs = _prompt_mixers(proj, gates, p_tables, gp[l], gdn_conv_w[l], na[l], nb[l], nc[l], bp, tp,
                              PROMPT_CHUNK, PROMPT_SEQS)
        p_states.append(outs[1:])
        res = _sample_mixers(l, proj, gates, s_tables, gp[l], gdn_conv_w[l], na[l], nb[l], nc[l],
                             s_states, s_out, bs, ts, rows_p, SAMPLE_BLOCK)
        y_s, s_out = res[0], res[1:]
        x, h = _matmul_out(outs[0], y_s, w_out_b, x, norm_mix_post[l], norm_mlp_pre[l], rows_s)
        up = _matmul_up(h, w_up_b, w_down, w_in_t, l, tm, 2048)
        u, w_down_b = up[0], up[1]
        if l + 1 < depth:
            w_in_p, w_gate_p = up[2], up[3]
        x, h = _matmul_down(u, w_down_b, x, norm_mlp_post[l], norm_mix_pre[(l + 1) % depth], tm, 1024)

    stk = lambda j: jnp.stack([s[j] for s in p_states], axis=0)
    sc, sn, sm, sr, sg, sconv = s_out
    return (
        x[:rows_p].reshape(bp, tp, d), x[rows_p:].reshape(bs, ts, d),
        stk(0), stk(1), stk(2)[:, :, 0, :H_A], stk(3), stk(4), stk(5),
        sc, sn[:, ts - 1::ts, :].reshape(depth, bs, H_A, HEAD_DIM), sm[:, ts - 1::ts, :H_A], sr, sg,
        sconv,
    )
```

```python
import functools
import math
import types

import numpy as np
import jax
import jax.numpy as jnp
from jax import lax
from jax.experimental import pallas as pl
from jax.experimental.pallas import tpu as pltpu

f32 = jnp.float32
bf16 = jnp.bfloat16

D_MODEL = 2048
HEAD_DIM = 128
H_A, H_B, H_C = 4, 4, 8
D_A, D_B, D_C = H_A * HEAD_DIM, H_B * HEAD_DIM, H_C * HEAD_DIM
D_FF = 4 * D_MODEL
CONV_W = 4
PAST_LEN = 16384
ROPE_BASE = 10000.0
GATE_SOFTCAP = 15.0
NORM_EPS = 1e-6
QK_SCALE = HEAD_DIM ** -0.5

COL_A = 0
COL_B = 4 * D_A
COL_C = COL_B + 4 * D_B
N_PROJ = COL_C + 4 * D_C
LANE_I, LANE_F, LANE_DECAY, LANE_BETA = 0, H_A, 2 * H_A, 2 * H_A + H_C

PROMPT_CHUNK = 128
PROMPT_SEQS = 2
LEAD_FILLERS = 4
INV_BLOCK = 8
SAMPLE_BLOCK = 8
NEG_BIG = -1e30
VMEM_LIMIT = 60 * 1024 * 1024


def _cparams(sem):
    return pltpu.CompilerParams(dimension_semantics=sem, vmem_limit_bytes=VMEM_LIMIT)


def _rms(x, g):
    return x * lax.rsqrt(jnp.mean(x * x, axis=-1, keepdims=True) + NORM_EPS) * g


def _rmsnorm_kernel(x_ref, g_ref, o_ref):
    o_ref[...] = _rms(x_ref[...], g_ref[...]).astype(o_ref.dtype)


def _rmsnorm_rows(x, g, tm):
    m, d = x.shape
    return pl.pallas_call(
        _rmsnorm_kernel,
        grid=(m // tm,),
        in_specs=[pl.BlockSpec((tm, d), lambda i: (i, 0)), pl.BlockSpec((1, d), lambda i: (0, 0))],
        out_specs=pl.BlockSpec((tm, d), lambda i: (i, 0)),
        out_shape=jax.ShapeDtypeStruct((m, d), bf16),
        compiler_params=_cparams(("parallel",)),
        name="rmsnorm_in",
    )(x, g.reshape(1, d))


def _dot_tb(a, b):
    return lax.dot_general(a, b, (((1,), (1,)), ((), ())), preferred_element_type=f32)


def _mm_gates_kernel(a_ref, w_ref, o_ref):
    o_ref[...] = _dot_tb(a_ref[...], w_ref[...])


def _matmul_gates(h, w_t, tm):
    m, k = h.shape
    n = w_t.shape[0]
    return pl.pallas_call(
        _mm_gates_kernel,
        grid=(m // tm,),
        in_specs=[pl.BlockSpec((tm, k), lambda i: (i, 0)), pl.BlockSpec((n, k), lambda i: (0, 0))],
        out_specs=pl.BlockSpec((tm, n), lambda i: (i, 0)),
        out_shape=jax.ShapeDtypeStruct((m, n), f32),
        compiler_params=_cparams(("parallel",)),
        name="proj_gates",
    )(h, w_t)


def _mm_in_kernel(a_ref, w_ref, wo_ref, wu_ref, o_ref, wob_ref, wub_ref):
    o_ref[...] = _dot_tb(a_ref[...], w_ref[...])
    wob_ref[...] = wo_ref[...].astype(bf16)
    wub_ref[...] = wu_ref[...].astype(bf16)


def _matmul_in(h, w_t, w_out, w_up, layer, tm, tn):
    m, k = h.shape
    n = w_t.shape[0]
    nj, ni = n // tn, m // tm
    d = w_out.shape[1]
    f = w_up.shape[2]
    slab = d // (nj * ni)
    step = lambda j, i: j * ni + i
    side_in = lambda width: pl.BlockSpec((None, slab, width), lambda j, i: (layer, step(j, i), 0))
    side_out = lambda width: pl.BlockSpec((slab, width), lambda j, i: (step(j, i), 0))
    return pl.pallas_call(
        _mm_in_kernel,
        grid=(nj, ni),
        in_specs=[pl.BlockSpec((tm, k), lambda j, i: (i, 0)), pl.BlockSpec((tn, k), lambda j, i: (j, 0)),
                  side_in(d), side_in(f)],
        out_specs=[pl.BlockSpec((tm, tn), lambda j, i: (i, j)), side_out(d), side_out(f)],
        out_shape=[jax.ShapeDtypeStruct((m, n), f32), jax.ShapeDtypeStruct((d, d), bf16),
                   jax.ShapeDtypeStruct((d, f), bf16)],
        compiler_params=_cparams(("parallel", "arbitrary")),
        name="proj_in",
    )(h, w_t, w_out, w_up)


REPACK_LANES = 128


def _repack_slab(win_ref, main_ref, gate_ref):
    n_a = 4 * D_A
    n_ag = n_a + 2 * H_A
    n_bc = 4 * D_B + 4 * D_C
    main_ref[0:n_a, :] = win_ref[0:n_a, :].astype(bf16)
    main_ref[n_a:n_a + n_bc, :] = win_ref[n_ag:n_ag + n_bc, :].astype(bf16)
    gate_ref[...] = jnp.zeros_like(gate_ref)
    gate_ref[LANE_I:LANE_I + 2 * H_A, :] = win_ref[n_a:n_ag, :].astype(bf16)
    gate_ref[LANE_DECAY:LANE_DECAY + 2 * H_C, :] = win_ref[n_ag + n_bc:n_ag + n_bc + 2 * H_C, :].astype(bf16)


def _repack_part(win_ref, main_ref, gate_ref, part, parts):
    n_a = 4 * D_A
    n_ag = n_a + 2 * H_A
    n_bc = 4 * D_B + 4 * D_C
    runs_per_call = (N_PROJ // n_a) // parts
    for c in range(runs_per_call):
        dst = pl.multiple_of((part * runs_per_call + c) * n_a, 16)
        src = pl.multiple_of(dst + jnp.where(dst >= n_a, 2 * H_A, 0), 8)
        main_ref[pl.ds(dst, n_a), :] = win_ref[pl.ds(src, n_a), :].astype(bf16)
    gate_ref[...] = jnp.zeros_like(gate_ref)
    gate_ref[LANE_I:LANE_I + 2 * H_A, :] = win_ref[n_a:n_ag, :].astype(bf16)
    gate_ref[LANE_DECAY:LANE_DECAY + 2 * H_C, :] = win_ref[n_ag + n_bc:n_ag + n_bc + 2 * H_C, :].astype(bf16)


def _mm_up_kernel(repack_every, a_ref, w_ref, wd_ref, *rest):
    if repack_every:
        win_ref, o_ref, wdb_ref, main_ref, gate_ref = rest
        step = pl.program_id(0) * pl.num_programs(1) + pl.program_id(1)
        _repack_part(win_ref, main_ref, gate_ref, step % repack_every, repack_every)
    else:
        o_ref, wdb_ref = rest
    z = jnp.maximum(jnp.dot(a_ref[...], w_ref[...], preferred_element_type=f32), 0.0)
    o_ref[...] = (z * z).astype(o_ref.dtype)
    wdb_ref[...] = wd_ref[...].astype(bf16)


def _matmul_up(h, w, w_down, w_in_t, layer, tm, tn):
    m, k = h.shape
    n = w.shape[1]
    nj, ni = n // tn, m // tm
    steps = nj * ni
    step = lambda j, i: j * ni + i
    with_repack = layer + 1 < w_in_t.shape[0]
    dslab = w_down.shape[1] // steps
    d = w_down.shape[2]
    in_specs = [pl.BlockSpec((tm, k), lambda j, i: (i, 0)), pl.BlockSpec((k, tn), lambda j, i: (0, j)),
                pl.BlockSpec((None, dslab, d), lambda j, i: (layer, step(j, i), 0))]
    out_specs = [pl.BlockSpec((tm, tn), lambda j, i: (i, j)), pl.BlockSpec((dslab, d), lambda j, i: (step(j, i), 0))]
    out_shape = [jax.ShapeDtypeStruct((m, n), bf16), jax.ShapeDtypeStruct(w_down.shape[1:], bf16)]
    args = [h, w, w_down]
    every = 0
    if with_repack:
        n_in, dm = w_in_t.shape[1:]
        every = steps // (dm // REPACK_LANES)
        slab = lambda j, i: step(j, i) // every
        in_specs.append(pl.BlockSpec((None, n_in, REPACK_LANES), lambda j, i: (layer + 1, 0, slab(j, i))))
        out_specs += [pl.BlockSpec((N_PROJ, REPACK_LANES), lambda j, i: (0, slab(j, i))),
                      pl.BlockSpec((128, REPACK_LANES), lambda j, i: (0, slab(j, i)))]
        out_shape += [jax.ShapeDtypeStruct((N_PROJ, dm), bf16), jax.ShapeDtypeStruct((128, dm), bf16)]
        args.append(w_in_t)
    return pl.pallas_call(
        functools.partial(_mm_up_kernel, every),
        grid=(nj, ni),
        in_specs=in_specs,
        out_specs=out_specs,
        out_shape=out_shape,
        compiler_params=_cparams(("parallel", "arbitrary")),
        name="mlp_up",
    )(*args)


def _residual_epilogue(z, x_ref, gpost_ref, gnext_ref, xo_ref, ho_ref, rows=slice(None)):
    x_new = x_ref[rows, :] + _rms(z, gpost_ref[...])
    xo_ref[rows, :] = x_new
    ho_ref[rows, :] = _rms(x_new, gnext_ref[...]).astype(ho_ref.dtype)


OUT_ROW_PARTS = 4


def _mm_out_kernel(n_first, a_ref, b_ref, w_ref, x_ref, gpost_ref, gnext_ref, xo_ref, ho_ref):
    a = jnp.where(pl.program_id(0) < n_first, a_ref[...], b_ref[...])
    part = a.shape[0] // OUT_ROW_PARTS
    for r in range(OUT_ROW_PARTS):
        rows = slice(r * part, (r + 1) * part)
        z = jnp.dot(a[rows], w_ref[...], preferred_element_type=f32)
        _residual_epilogue(z, x_ref, gpost_ref, gnext_ref, xo_ref, ho_ref, rows)


def _matmul_out(y_p, y_s, w, x, g_post, g_next, tm):
    k = y_p.shape[1]
    n_p, n_s = y_p.shape[0] // tm, y_s.shape[0] // tm
    m = (n_p + n_s) * tm
    d = w.shape[1]
    row = lambda i: (i, 0)
    fixed = lambda i: (0, 0)
    return pl.pallas_call(
        functools.partial(_mm_out_kernel, n_p),
        grid=(n_p + n_s,),
        in_specs=[pl.BlockSpec((tm, k), lambda i: (jnp.minimum(i, n_p - 1), 0)),
                  pl.BlockSpec((tm, k), lambda i: (jnp.maximum(i - n_p, 0), 0)),
                  pl.BlockSpec((k, d), fixed), pl.BlockSpec((tm, d), row),
                  pl.BlockSpec((1, d), fixed), pl.BlockSpec((1, d), fixed)],
        out_specs=[pl.BlockSpec((tm, d), row), pl.BlockSpec((tm, d), row)],
        out_shape=[jax.ShapeDtypeStruct((m, d), f32), jax.ShapeDtypeStruct((m, d), bf16)],
        compiler_params=_cparams(("parallel",)),
        name="proj_out",
    )(y_p, y_s, w, x, g_post.reshape(1, d), g_next.reshape(1, d))


def _mm_down_kernel(a_ref, w_ref, x_ref, gpost_ref, gnext_ref, xo_ref, ho_ref):
    kk = pl.program_id(1)

    @pl.when(kk == 0)
    def _():
        xo_ref[...] = jnp.zeros_like(xo_ref)

    xo_ref[...] += jnp.dot(a_ref[...], w_ref[...], preferred_element_type=f32)

    @pl.when(kk == pl.num_programs(1) - 1)
    def _():
        _residual_epilogue(xo_ref[...], x_ref, gpost_ref, gnext_ref, xo_ref, ho_ref)


def _matmul_down(u, w, x, g_post, g_next, tm, tk):
    m, k = u.shape
    d = w.shape[1]
    row = lambda i, kk: (i, 0)
    fixed = lambda i, kk: (0, 0)
    return pl.pallas_call(
        _mm_down_kernel,
        grid=(m // tm, k // tk),
        in_specs=[pl.BlockSpec((tm, tk), lambda i, kk: (i, kk)),
                  pl.BlockSpec((tk, d), lambda i, kk: (kk, 0)),
                  pl.BlockSpec((tm, d), row), pl.BlockSpec((1, d), fixed), pl.BlockSpec((1, d), fixed)],
        out_specs=[pl.BlockSpec((tm, d), row), pl.BlockSpec((tm, d), row)],
        out_shape=[jax.ShapeDtypeStruct((m, d), f32), jax.ShapeDtypeStruct((m, d), bf16)],
        compiler_params=_cparams(("parallel", "arbitrary")),
        name="mlp_down",
    )(u, w, x, g_post.reshape(1, d), g_next.reshape(1, d))


def _dot(a, b):
    return jnp.dot(a, b, preferred_element_type=f32)


def _dot_ta(a, b):
    return lax.dot_general(a, b, (((0,), (0,)), ((), ())), preferred_element_type=f32)


def _lane_sum(x):
    return jnp.sum(x, axis=-1, keepdims=True)


def _softplus(z):
    return jnp.maximum(z, 0.0) + jnp.log1p(jnp.exp(-jnp.abs(z)))


def _gate_transform(gt, gp):
    z = gt + gp[0:1, :]
    capped = GATE_SOFTCAP * jnp.tanh(z / GATE_SOFTCAP)
    log_f = -_softplus(-capped)
    decay = -jnp.exp(gp[1:2, :]) * _softplus(z)
    beta = jax.nn.sigmoid(gt)
    return capped, log_f, decay, beta


def _head_norm(x, gain, center):
    if center:
        x = x - jnp.mean(x, axis=-1, keepdims=True)
    return x * lax.rsqrt(jnp.mean(x * x, axis=-1, keepdims=True) + NORM_EPS) * gain


def _l2norm(x):
    return x * lax.rsqrt(_lane_sum(x * x) + NORM_EPS)


def _silu(x):
    return x * jax.nn.sigmoid(x)


def _rotary(x, cos, sin_signed):
    return x * cos + pltpu.roll(x, HEAD_DIM // 2, 1) * sin_signed


def _retention_log_gamma():
    return np.log1p(-np.exp2(-5.0 - np.arange(H_B, dtype=np.float64)))


def _rope_tables(pos):
    half = HEAD_DIM // 2
    inv = ROPE_BASE ** (-np.arange(half, dtype=np.float64) / half)
    ang = np.asarray(pos, dtype=np.float64)[:, None] * inv[None, :]
    cos = np.concatenate([np.cos(ang), np.cos(ang)], axis=-1)
    sin = np.concatenate([-np.sin(ang), np.sin(ang)], axis=-1)
    return cos.astype(np.float32), sin.astype(np.float32)


def _scan_rows(x, row, length, op, fill):
    s = 1
    while s < length:
        x = op(x, jnp.where(row >= s, pltpu.roll(x, s, 0), fill))
        s *= 2
    return x


def _rows_to_lanes(x, length):
    if length < 128:
        x = jnp.concatenate([x, jnp.zeros((128 - length, 128), x.dtype)], axis=0)
    return x.T[:, :length]


def _prompt_mixer_kernel(L, NS, *refs):
    proj_refs, gt_refs = refs[:NS], refs[NS:2 * NS]
    (cos_ref, sin_ref, rd_ref, rq_ref, rk_ref, gp_ref, cw_ref, na_ref, nb_ref, nc_ref,
     y_ref, c_ref, n_ref, m_ref, r_ref, g_ref, conv_ref, ext_ref) = refs[2 * NS:]
    c = pl.program_id(1)

    @pl.when(c == 0)
    def _():
        c_ref[...] = jnp.zeros_like(c_ref)
        n_ref[...] = jnp.zeros_like(n_ref)
        m_ref[...] = jnp.zeros_like(m_ref)
        r_ref[...] = jnp.zeros_like(r_ref)
        g_ref[...] = jnp.zeros_like(g_ref)
        ext_ref[:, 0:8, :] = jnp.zeros((NS, 8, 3 * D_C), f32)

    row = lax.broadcasted_iota(jnp.int32, (L, 128), 0)
    lane = lax.broadcasted_iota(jnp.int32, (L, 128), 1)
    ri = lax.broadcasted_iota(jnp.int32, (L, L), 0)
    ci = lax.broadcasted_iota(jnp.int32, (L, L), 1)
    incl = ri >= ci
    strict = ri > ci
    head_lane = lane < H_A
    seqs = range(NS)

    def gate_block(s):
        g = types.SimpleNamespace()
        capped, log_f, decay, g.beta = _gate_transform(gt_refs[s][...], gp_ref[...])
        ig = jnp.where(head_lane, capped, 0.0)
        lf = jnp.where(head_lane, pltpu.roll(log_f, 128 - LANE_F, 1), 0.0)
        mp = m_ref[s]
        F = _scan_rows(lf, row, L, jnp.add, 0.0)
        m = F + jnp.maximum(mp, _scan_rows(ig - F, row, L, jnp.maximum, NEG_BIG))
        m_new = m[L - 1:L, :]
        f_last = F[L - 1:L, :]
        g.a_rows = F - m
        g.inter = jnp.exp(F + mp - m)
        g.inv_floor = jnp.exp(-m)
        g.wl = jnp.exp(ig + f_last - F - m_new)
        g.dec = jnp.exp(f_last + mp - m_new)
        m_ref[s] = m_new
        g.G = _scan_rows(decay, row, L, jnp.add, 0.0)
        g.b_lanes = g.g_lanes = _rows_to_lanes(jnp.where(head_lane, ig - F, g.G), L)
        g.gam = jnp.exp(g.G)
        g_last = g.G[L - 1:L, :]
        g.k_decay = jnp.exp(g_last - g.G)
        g.s_decay = jnp.exp(g_last)
        return g

    gs = [gate_block(s) for s in seqs]

    def mlstm_head(s, h):
        proj_ref, g = proj_refs[s], gs[s]
        q = proj_ref[:, COL_A + h * HEAD_DIM:COL_A + (h + 1) * HEAD_DIM]
        k = proj_ref[:, COL_A + D_A + h * HEAD_DIM:COL_A + D_A + (h + 1) * HEAD_DIM] * QK_SCALE
        v = proj_ref[:, COL_A + 2 * D_A + h * HEAD_DIM:COL_A + 2 * D_A + (h + 1) * HEAD_DIM]
        og = proj_ref[:, COL_A + 3 * D_A + h * HEAD_DIM:COL_A + 3 * D_A + (h + 1) * HEAD_DIM]
        logw = g.a_rows[:, h:h + 1] + g.b_lanes[h:h + 1, :]
        sc = _dot_tb(q, k) * jnp.exp(jnp.where(incl, logw, NEG_BIG))
        c_old = c_ref[s, h]
        n_old = n_ref[s, h:h + 1, :]
        inter_h = g.inter[:, h:h + 1]
        num = inter_h * _dot(q, c_old) + _dot(sc, v)
        den = inter_h * _lane_sum(q * n_old) + _lane_sum(sc)
        hh = num / jnp.maximum(jnp.abs(den), g.inv_floor[:, h:h + 1])
        kw = k * g.wl[:, h:h + 1]
        dec_h = g.dec[:, h:h + 1]
        c_ref[s, h] = dec_h * c_old + _dot_ta(kw, v)
        n_ref[s, h:h + 1, :] = dec_h * n_old + jnp.sum(kw, axis=0, keepdims=True)
        ya = jax.nn.sigmoid(og) * _head_norm(hh, na_ref[h:h + 1, :], False)
        y_ref[s, :, h * HEAD_DIM:(h + 1) * HEAD_DIM] = ya.astype(y_ref.dtype)

    lg = _retention_log_gamma()

    def retention_head(s, h):
        proj_ref = proj_refs[s]
        cos = cos_ref[...]
        sin = sin_ref[...]
        q = proj_ref[:, COL_B + h * HEAD_DIM:COL_B + (h + 1) * HEAD_DIM]
        k = proj_ref[:, COL_B + D_B + h * HEAD_DIM:COL_B + D_B + (h + 1) * HEAD_DIM]
        v = proj_ref[:, COL_B + 2 * D_B + h * HEAD_DIM:COL_B + 2 * D_B + (h + 1) * HEAD_DIM]
        gb = proj_ref[:, COL_B + 3 * D_B + h * HEAD_DIM:COL_B + 3 * D_B + (h + 1) * HEAD_DIM]
        qr = _rotary(q, cos, sin)
        kr = _rotary(k, cos, sin) * QK_SCALE
        s_old = r_ref[s, h]
        inner = _dot_tb(qr, kr) * rd_ref[h]
        o = _dot(inner, v) + rq_ref[h] * _dot(qr, s_old)
        r_ref[s, h] = float(np.exp(L * lg[h])) * s_old + _dot_ta(kr * rk_ref[h], v)
        yb = _silu(gb) * _head_norm(o, nb_ref[h:h + 1, :], True)
        y_ref[s, :, D_A + h * HEAD_DIM:D_A + (h + 1) * HEAD_DIM] = yb.astype(y_ref.dtype)

    fillers = [functools.partial(mlstm_head, s, h) for h in range(H_A) for s in seqs]
    fillers += [functools.partial(retention_head, s, h) for h in range(H_B) for s in seqs]
    def emit_fillers(count):
        for _ in range(min(count, len(fillers))):
            fillers.pop(0)()

    emit_fillers(LEAD_FILLERS)
    n_stages = max(int(math.log2(min(INV_BLOCK, L))) - 1, 0) + int(math.log2(L // min(INV_BLOCK, L)))
    per_stage = -(-len(fillers) // max(n_stages, 1))

    for s in seqs:
        ext_ref[s, 8:8 + L, :] = proj_refs[s][:, COL_C:COL_C + 3 * D_C]

    def conv_block(s, col):
        acc = cw_ref[CONV_W - 1:CONV_W, col:col + HEAD_DIM] * ext_ref[s, 8:8 + L, col:col + HEAD_DIM]
        for w in range(CONV_W - 1):
            off = 8 - (CONV_W - 1) + w
            acc = acc + cw_ref[w:w + 1, col:col + HEAD_DIM] * ext_ref[s, off:off + L, col:col + HEAD_DIM]
        return _silu(acc)

    pairs = [(s, h) for h in range(H_C) for s in seqs]
    heads = range(len(pairs))
    dcol = [LANE_DECAY + h for _, h in pairs]
    gam = [gs[s].gam for s, _ in pairs]
    beta_c = [gs[s].beta[:, LANE_BETA + h:LANE_BETA + h + 1] for s, h in pairs]
    q = [_l2norm(conv_block(s, h * HEAD_DIM)) * QK_SCALE for s, h in pairs]
    k = [_l2norm(conv_block(s, D_C + h * HEAD_DIM)) for s, h in pairs]
    v = [conv_block(s, 2 * D_C + h * HEAD_DIM) for s, h in pairs]
    dec_in = [jnp.exp(jnp.where(incl, gs[s].G[:, LANE_DECAY + h:LANE_DECAY + h + 1]
                                - gs[s].g_lanes[LANE_DECAY + h:LANE_DECAY + h + 1, :], NEG_BIG)) for s, h in pairs]
    a_mat = [jnp.where(strict, dec_in[h], 0.0) * beta_c[h] * _dot_tb(k[h], k[h]) for h in heads]
    qk = [_dot_tb(q[h], k[h]) * dec_in[h] for h in heads]
    blk = min(INV_BLOCK, L)
    same = (ri // blk) == (ci // blk)
    pw = [jnp.where(same, -a, 0.0) for a in a_mat]
    e_mat = list(pw)
    span = 1
    while 2 * span < blk:
        pw = [_dot(p, p) for p in pw]
        e_mat = [e_mat[h] + pw[h] + _dot(e_mat[h], pw[h]) for h in heads]
        emit_fillers(per_stage)
        span *= 2
    while blk < L:
        wider = (ri // (2 * blk)) == (ci // (2 * blk))
        a_off = [jnp.where(wider & jnp.logical_not(same), a, 0.0) for a in a_mat]
        low = [a_off[h] + _dot(e_mat[h], a_off[h]) for h in heads]
        e_mat = [e_mat[h] - low[h] - _dot(low[h], e_mat[h]) for h in heads]
        emit_fillers(per_stage)
        same = wider
        blk *= 2
    rhs = [jnp.concatenate([(beta_c[h] * gam[h][:, dcol[h]:dcol[h] + 1]) * k[h], beta_c[h] * v[h]], axis=1)
           for h in heads]
    sol = [rhs[h] + _dot(e_mat[h], rhs[h]) for h in heads]
    emit_fillers(len(fillers))
    s_old = [g_ref[s, hd] for s, hd in pairs]
    both = [_dot(jnp.concatenate([sol[h][:, :HEAD_DIM], q[h]], axis=0), s_old[h]) for h in heads]
    u = [sol[h][:, HEAD_DIM:] - both[h][:L] for h in heads]
    o = [gam[h][:, dcol[h]:dcol[h] + 1] * both[h][L:] + _dot(qk[h], u[h]) for h in heads]
    for h, (s, hd) in enumerate(pairs):
        g_ref[s, hd] = (gs[s].s_decay[:, dcol[h]:dcol[h] + 1] * s_old[h]
                        + _dot_ta(k[h] * gs[s].k_decay[:, dcol[h]:dcol[h] + 1], u[h]))
    for h, (s, hd) in enumerate(pairs):
        gz = proj_refs[s][:, COL_C + 3 * D_C + hd * HEAD_DIM:COL_C + 3 * D_C + (hd + 1) * HEAD_DIM]
        yc = _silu(gz) * _head_norm(o[h], nc_ref[...], False)
        y_ref[s, :, D_A + D_B + hd * HEAD_DIM:D_A + D_B + (hd + 1) * HEAD_DIM] = yc.astype(y_ref.dtype)

    for s in seqs:
        ext_ref[s, 0:8, :] = ext_ref[s, L:L + 8, :]

    @pl.when(c == pl.num_programs(1) - 1)
    def _():
        for s in seqs:
            conv_ref[s] = ext_ref[s, 8 + L - (CONV_W - 1):8 + L, :]


def _prompt_tables(seq, L):
    lg = _retention_log_gamma()
    i = np.arange(L, dtype=np.float64)
    diff = i[:, None] - i[None, :]
    rd = np.where(diff >= 0, np.exp(np.maximum(diff, 0.0) * lg[:, None, None]), 0.0)
    rq = np.broadcast_to(np.exp((i + 1.0) * lg[:, None])[..., None], (H_B, L, HEAD_DIM))
    rk = np.broadcast_to(np.exp((L - 1.0 - i) * lg[:, None])[..., None], (H_B, L, HEAD_DIM))
    cos, sin = _rope_tables(np.arange(seq))
    return (jnp.asarray(cos), jnp.asarray(sin), jnp.asarray(rd, f32), jnp.asarray(rq, f32), jnp.asarray(rk, f32))


def _prompt_mixers(proj, gates, tables, gp, cw, na, nb, nc, batch, seq, L, NS):
    nchunk = seq // L
    cos, sin, rd, rq, rk = tables
    full = lambda shape: pl.BlockSpec(shape, lambda b, c: (0,) * len(shape))
    state4 = lambda heads: pl.BlockSpec((NS, heads, HEAD_DIM, HEAD_DIM), lambda b, c: (b, 0, 0, 0))
    rows_of = lambda s, width: pl.BlockSpec((L, width), lambda b, c: ((b * NS + s) * nchunk + c, 0))
    outs = pl.pallas_call(
        functools.partial(_prompt_mixer_kernel, L, NS),
        grid=(batch // NS, nchunk),
        in_specs=[rows_of(s, N_PROJ) for s in range(NS)] + [rows_of(s, 128) for s in range(NS)] + [
            pl.BlockSpec((L, HEAD_DIM), lambda b, c: (c, 0)),
            pl.BlockSpec((L, HEAD_DIM), lambda b, c: (c, 0)),
            full((H_B, L, L)), full((H_B, L, HEAD_DIM)), full((H_B, L, HEAD_DIM)),
            full((8, 128)), full((CONV_W, 3 * D_C)),
            full((H_A, HEAD_DIM)), full((H_B, HEAD_DIM)), full((1, HEAD_DIM)),
        ],
        out_specs=[
            pl.BlockSpec((None, NS, None, L, D_MODEL), lambda b, c: (b, 0, c, 0, 0)),
            state4(H_A),
            pl.BlockSpec((NS, H_A, HEAD_DIM), lambda b, c: (b, 0, 0)),
            pl.BlockSpec((NS, 1, 128), lambda b, c: (b, 0, 0)),
            state4(H_B),
            state4(H_C),
            pl.BlockSpec((NS, CONV_W - 1, 3 * D_C), lambda b, c: (b, 0, 0)),
        ],
        out_shape=[
            jax.ShapeDtypeStruct((batch // NS, NS, nchunk, L, D_MODEL), bf16),
            jax.ShapeDtypeStruct((batch, H_A, HEAD_DIM, HEAD_DIM), f32),
            jax.ShapeDtypeStruct((batch, H_A, HEAD_DIM), f32),
            jax.ShapeDtypeStruct((batch, 1, 128), f32),
            jax.ShapeDtypeStruct((batch, H_B, HEAD_DIM, HEAD_DIM), f32),
            jax.ShapeDtypeStruct((batch, H_C, HEAD_DIM, HEAD_DIM), f32),
            jax.ShapeDtypeStruct((batch, CONV_W - 1, 3 * D_C), f32),
        ],
        scratch_shapes=[pltpu.VMEM((NS, L + 8, 3 * D_C), f32)],
        compiler_params=_cparams(("parallel", "arbitrary")),
        name="prompt_mixers",
    )(*([proj] * NS + [gates] * NS), cos, sin, rd, rq, rk, gp, cw, na, nb, nc)
    return [outs[0].reshape(batch * seq, D_MODEL)] + list(outs[1:])


def _sample_mixer_kernel(T, BB, layer_first, *refs):
    (proj_ref, gt_ref, cos_ref, sin_ref, rt_ref, gp_ref, cw_ref, na_ref, nb_ref, nc_ref,
     c0_ref, nrep_ref, mrep_ref, r0_ref, g0_ref, conv0_ref) = refs[:16]
    rest = refs[16 + (0 if layer_first else 6):]
    (y_ref, c_ref, nout_ref, mout_ref, r_ref, g_ref, conv_ref, cv_ref, sa_ref, ext_ref) = rest
    R = T * BB
    row = lax.broadcasted_iota(jnp.int32, (R, 128), 0)
    lane = lax.broadcasted_iota(jnp.int32, (R, 128), 1)
    t = lax.rem(row, T)
    tcol = t[:, 0:1]

    def shift(x, s):
        return x if s == 0 else pltpu.roll(x, s, 0)

    def seg_scan(x, op, fill):
        s = 1
        while s < T:
            x = op(x, jnp.where(t >= s, shift(x, s), fill))
            s *= 2
        return x

    def last_rep(x):
        x_last = jnp.where(t == T - 1, x, 0.0)
        out = x_last
        for s in range(1, T):
            out = out + pltpu.roll(x_last, R - s, 0)
        return out

    capped, log_f, decay, beta = _gate_transform(gt_ref[...], gp_ref[...])

    head_lane = lane < H_A
    ig = jnp.where(head_lane, capped, 0.0)
    lf = jnp.where(head_lane, pltpu.roll(log_f, 128 - LANE_F, 1), 0.0)
    mp = mrep_ref[...]
    F = seg_scan(lf, jnp.add, 0.0)
    m = F + jnp.maximum(mp, seg_scan(ig - F, jnp.maximum, NEG_BIG))
    m_new = last_rep(m)
    f_last = last_rep(F)
    inter = jnp.exp(F + mp - m)
    inv_floor = jnp.exp(-m)
    wl = jnp.exp(ig + f_last - F - m_new)
    dec = jnp.exp(f_last + mp - m_new)
    mout_ref[...] = m_new
    pw = [jnp.where(t >= s, jnp.exp(F - shift(F, s) + shift(ig, s) - m), 0.0) for s in range(T)]

    def seq(x, b):
        return x[b * T:(b + 1) * T]

    slot_a, slot_b, slot_w, slot_q = 0, H_A, H_A + H_B, H_A + H_B + H_C

    qa = [proj_ref[:, COL_A + h * HEAD_DIM:COL_A + (h + 1) * HEAD_DIM] for h in range(H_A)]
    for h in range(H_A):
        for b in range(BB):
            sa_ref[slot_a + h, b * T:(b + 1) * T, :] = _dot(seq(qa[h], b), c0_ref[b, h])
    cos = cos_ref[...]
    sin = sin_ref[...]
    lg = _retention_log_gamma()
    qr = [_rotary(proj_ref[:, COL_B + h * HEAD_DIM:COL_B + (h + 1) * HEAD_DIM], cos, sin) for h in range(H_B)]
    kr = [_rotary(proj_ref[:, COL_B + D_B + h * HEAD_DIM:COL_B + D_B + (h + 1) * HEAD_DIM], cos, sin) * QK_SCALE
          for h in range(H_B)]
    for h in range(H_B):
        for b in range(BB):
            sa_ref[slot_b + h, b * T:(b + 1) * T, :] = _dot(seq(qr[h], b), r0_ref[b, h])

    ka = [proj_ref[:, COL_A + D_A + h * HEAD_DIM:COL_A + D_A + (h + 1) * HEAD_DIM] * QK_SCALE for h in range(H_A)]
    va = [proj_ref[:, COL_A + 2 * D_A + h * HEAD_DIM:COL_A + 2 * D_A + (h + 1) * HEAD_DIM] for h in range(H_A)]
    n_old = [nrep_ref[:, h * HEAD_DIM:(h + 1) * HEAD_DIM] for h in range(H_A)]
    kw = [ka[h] * wl[:, h:h + 1] for h in range(H_A)]
    for h in range(H_A):
        for b in range(BB):
            dec_bh = dec[b * T:b * T + 1, h:h + 1]
            c_ref[b, h] = dec_bh * c0_ref[b, h] + _dot_ta(seq(kw[h], b), seq(va[h], b))
        nout_ref[:, h * HEAD_DIM:(h + 1) * HEAD_DIM] = dec[:, h:h + 1] * n_old[h] + seg_scan(kw[h], jnp.add, 0.0)
    vb = [proj_ref[:, COL_B + 2 * D_B + h * HEAD_DIM:COL_B + 2 * D_B + (h + 1) * HEAD_DIM] for h in range(H_B)]
    for h in range(H_B):
        kd = kr[h] * rt_ref[:, H_B + h:H_B + h + 1]
        for b in range(BB):
            r_ref[b, h] = float(np.exp(T * lg[h])) * r0_ref[b, h] + _dot_ta(seq(kd, b), seq(vb[h], b))

    for b in range(BB):
        ext_ref[b, 0:CONV_W - 1, :] = conv0_ref[b]
        ext_ref[b, CONV_W - 1:CONV_W - 1 + T, :] = proj_ref[b * T:(b + 1) * T, COL_C:COL_C + 3 * D_C]
        acc = cw_ref[0:1, :] * ext_ref[b, 0:T, :]
        for w in range(1, CONV_W):
            acc = acc + cw_ref[w:w + 1, :] * ext_ref[b, w:w + T, :]
        cv_ref[b * T:(b + 1) * T, :] = _silu(acc)
        conv_ref[b] = ext_ref[b, T:T + CONV_W - 1, :]

    G = seg_scan(decay, jnp.add, 0.0)
    gam = jnp.exp(G)
    g_last = last_rep(G)
    k_decay = jnp.exp(g_last - G)
    s_decay = jnp.exp(g_last)
    dshift = [None] + [jnp.where(t >= s, jnp.exp(G - shift(G, s)), 0.0) for s in range(1, T)]

    heads = range(H_C)
    dcol = [LANE_DECAY + h for h in heads]
    beta_c = [beta[:, LANE_BETA + h:LANE_BETA + h + 1] for h in heads]
    qg = [_l2norm(cv_ref[:, h * HEAD_DIM:(h + 1) * HEAD_DIM]) * QK_SCALE for h in heads]
    kg = [_l2norm(cv_ref[:, D_C + h * HEAD_DIM:D_C + (h + 1) * HEAD_DIM]) for h in heads]
    vg = [cv_ref[:, 2 * D_C + h * HEAD_DIM:2 * D_C + (h + 1) * HEAD_DIM] for h in heads]
    a_sub = [[None] + [beta_c[h] * dshift[s][:, dcol[h]:dcol[h] + 1] * _lane_sum(kg[h] * shift(kg[h], s))
                       for s in range(1, T)] for h in heads]
    rhs_w = [(beta_c[h] * gam[:, dcol[h]:dcol[h] + 1]) * kg[h] for h in heads]
    rhs_u = [beta_c[h] * vg[h] for h in heads]
    w_sol, u_sol = list(rhs_w), list(rhs_u)
    for i in range(1, T):
        for h in heads:
            upd_w = a_sub[h][1] * shift(w_sol[h], 1)
            upd_u = a_sub[h][1] * shift(u_sol[h], 1)
            for s in range(2, i + 1):
                upd_w = upd_w + a_sub[h][s] * shift(w_sol[h], s)
                upd_u = upd_u + a_sub[h][s] * shift(u_sol[h], s)
            w_sol[h] = jnp.where(t == i, rhs_w[h] - upd_w, w_sol[h])
            u_sol[h] = jnp.where(t == i, rhs_u[h] - upd_u, u_sol[h])
    for h in heads:
        for b in range(BB):
            both = _dot(jnp.concatenate([seq(w_sol[h], b), seq(qg[h], b)], axis=0), g0_ref[b, h])
            sa_ref[slot_w + h, b * T:(b + 1) * T, :] = both[:T]
            sa_ref[slot_q + h, b * T:(b + 1) * T, :] = both[T:]

    ug = [u_sol[h] - sa_ref[slot_w + h] for h in heads]
    for h in heads:
        kd = kg[h] * k_decay[:, dcol[h]:dcol[h] + 1]
        for b in range(BB):
            g_ref[b, h] = (s_decay[b * T:b * T + 1, dcol[h]:dcol[h] + 1] * g0_ref[b, h]
                           + _dot_ta(seq(kd, b), seq(ug[h], b)))

    sc_a = [[_lane_sum(qa[h] * shift(ka[h], s)) * pw[s][:, h:h + 1] for s in range(T)] for h in range(H_A)]
    qn = [_lane_sum(qa[h] * n_old[h]) for h in range(H_A)]
    sc_b = [[jnp.where(tcol >= s, _lane_sum(qr[h] * shift(kr[h], s)) * float(np.exp(s * lg[h])), 0.0)
             for s in range(T)] for h in range(H_B)]
    sc_c = [[_lane_sum(qg[h] * shift(kg[h], s)) * (1.0 if s == 0 else dshift[s][:, dcol[h]:dcol[h] + 1])
             for s in range(T)] for h in heads]
    hid_a, hid_b, hid_c = [], [], []
    for h in range(H_A):
        inter_h = inter[:, h:h + 1]
        num = inter_h * sa_ref[slot_a + h]
        den = inter_h * qn[h]
        for s in range(T):
            num = num + sc_a[h][s] * shift(va[h], s)
            den = den + sc_a[h][s]
        hid_a.append(num / jnp.maximum(jnp.abs(den), inv_floor[:, h:h + 1]))
    for h in range(H_B):
        o = rt_ref[:, h:h + 1] * sa_ref[slot_b + h]
        for s in range(T):
            o = o + sc_b[h][s] * shift(vb[h], s)
        hid_b.append(o)
    for h in heads:
        o = gam[:, dcol[h]:dcol[h] + 1] * sa_ref[slot_q + h]
        for s in range(T):
            o = o + sc_c[h][s] * shift(ug[h], s)
        hid_c.append(o)
    mean_b = [jnp.mean(x, axis=-1, keepdims=True) for x in hid_b]
    hid_b = [x - mu for x, mu in zip(hid_b, mean_b)]
    hidden = hid_a + hid_b + hid_c
    inv_rms = [lax.rsqrt(jnp.mean(x * x, axis=-1, keepdims=True) + NORM_EPS) for x in hidden]
    for h in range(H_A):
        og = proj_ref[:, COL_A + 3 * D_A + h * HEAD_DIM:COL_A + 3 * D_A + (h + 1) * HEAD_DIM]
        ya = jax.nn.sigmoid(og) * (hidden[h] * inv_rms[h] * na_ref[h:h + 1, :])
        y_ref[:, h * HEAD_DIM:(h + 1) * HEAD_DIM] = ya.astype(y_ref.dtype)
    for h in range(H_B):
        gb = proj_ref[:, COL_B + 3 * D_B + h * HEAD_DIM:COL_B + 3 * D_B + (h + 1) * HEAD_DIM]
        yb = _silu(gb) * (hidden[H_A + h] * inv_rms[H_A + h] * nb_ref[h:h + 1, :])
        y_ref[:, D_A + h * HEAD_DIM:D_A + (h + 1) * HEAD_DIM] = yb.astype(y_ref.dtype)
    for h in heads:
        gz = proj_ref[:, COL_C + 3 * D_C + h * HEAD_DIM:COL_C + 3 * D_C + (h + 1) * HEAD_DIM]
        yc = _silu(gz) * (hidden[H_A + H_B + h] * inv_rms[H_A + H_B + h] * nc_ref[...])
        y_ref[:, D_A + D_B + h * HEAD_DIM:D_A + D_B + (h + 1) * HEAD_DIM] = yc.astype(y_ref.dtype)


def _sample_tables(T, BB):
    lg = _retention_log_gamma()
    tt = np.arange(T, dtype=np.float64)
    rt = np.zeros((T, 128), np.float64)
    rt[:, 0:H_B] = np.exp((tt[:, None] + 1.0) * lg[None, :])
    rt[:, H_B:2 * H_B] = np.exp((T - 1.0 - tt[:, None]) * lg[None, :])
    cos, sin = _rope_tables(PAST_LEN + np.arange(T))
    rep = lambda a: jnp.asarray(np.tile(a, (BB, 1)), f32)
    return rep(cos), rep(sin), rep(rt)


def _sample_mixers(layer, proj, gates, tables, gp, cw, na, nb, nc, states, prev_out, batch, T, row0, BB):
    c0, nrep, mrep, r0, g0, conv0 = states
    R = T * BB
    nblk = batch // BB
    blk0 = row0 // R
    cos, sin, rt = tables
    full = lambda shape: pl.BlockSpec(shape, lambda i: (0,) * len(shape))
    st4 = lambda heads: pl.BlockSpec((None, BB, heads, HEAD_DIM, HEAD_DIM), lambda i: (layer, i, 0, 0, 0))
    rows = lambda width: pl.BlockSpec((None, R, width), lambda i: (layer, i, 0))
    convspec = pl.BlockSpec((None, BB, CONV_W - 1, 3 * D_C), lambda i: (layer, i, 0, 0))
    anyspec = pl.BlockSpec(memory_space=pl.ANY)
    in_specs = [
        pl.BlockSpec((R, N_PROJ), lambda i: (blk0 + i, 0)),
        pl.BlockSpec((R, 128), lambda i: (blk0 + i, 0)),
        full((R, HEAD_DIM)), full((R, HEAD_DIM)), full((R, 128)),
        full((8, 128)), full((CONV_W, 3 * D_C)),
        full((H_A, HEAD_DIM)), full((H_B, HEAD_DIM)), full((1, HEAD_DIM)),
        st4(H_A), rows(D_A), rows(128), st4(H_B), st4(H_C), convspec,
    ]
    args = [proj, gates, cos, sin, rt, gp, cw, na, nb, nc, c0, nrep, mrep, r0, g0, conv0]
    out_shape = [
        jax.ShapeDtypeStruct((batch * T, D_MODEL), bf16),
        jax.ShapeDtypeStruct(c0.shape, f32),
        jax.ShapeDtypeStruct(nrep.shape, f32),
        jax.ShapeDtypeStruct(mrep.shape, f32),
        jax.ShapeDtypeStruct(r0.shape, f32),
        jax.ShapeDtypeStruct(g0.shape, f32),
        jax.ShapeDtypeStruct(conv0.shape, f32),
    ]
    out_specs = [pl.BlockSpec((R, D_MODEL), lambda i: (i, 0)),
                 st4(H_A), rows(D_A), rows(128), st4(H_B), st4(H_C), convspec]
    first = prev_out is None
    aliases = {}
    if not first:
        in_specs += [anyspec] * 6
        args += list(prev_out)
        aliases = {16 + j: 1 + j for j in range(6)}
    return pl.pallas_call(
        functools.partial(_sample_mixer_kernel, T, BB, first),
        grid=(nblk,),
        in_specs=in_specs,
        out_specs=out_specs,
        out_shape=out_shape,
        input_output_aliases=aliases,
        scratch_shapes=[pltpu.VMEM((R, 3 * D_C), f32), pltpu.VMEM((H_A + H_B + 2 * H_C, R, HEAD_DIM), f32),
                        pltpu.VMEM((BB, 8, 3 * D_C), f32)],
        compiler_params=_cparams(("parallel",)),
        name="sample_mixers",
    )(*args)


def _repack_kernel(win_ref, main_ref, gate_ref):
    _repack_slab(win_ref, main_ref, gate_ref)


def _repack_w_in(w_in_t, layer):
    _, n_in, dm = w_in_t.shape
    return pl.pallas_call(
        _repack_kernel,
        grid=(dm // REPACK_LANES,),
        in_specs=[pl.BlockSpec((None, n_in, REPACK_LANES), lambda i: (layer, 0, i))],
        out_specs=[pl.BlockSpec((N_PROJ, REPACK_LANES), lambda i: (0, i)),
                   pl.BlockSpec((128, REPACK_LANES), lambda i: (0, i))],
        out_shape=[jax.ShapeDtypeStruct((N_PROJ, dm), bf16), jax.ShapeDtypeStruct((128, dm), bf16)],
        compiler_params=_cparams(("parallel",)),
        name="repack_w_in",
    )(w_in_t)


def _gate_params(gate_bias, dt_bias, a_log):
    depth = gate_bias.shape[0]
    gp = jnp.zeros((depth, 8, 128), f32)
    gp = gp.at[:, 0, LANE_I:LANE_I + 2 * H_A].set(gate_bias)
    gp = gp.at[:, 0, LANE_DECAY:LANE_DECAY + H_C].set(dt_bias)
    gp = gp.at[:, 1, LANE_DECAY:LANE_DECAY + H_C].set(a_log)
    return gp


def kernel(x_prompt, x_sample, state_mlstm_C, state_mlstm_n, state_mlstm_m, state_ret_S, state_gdn_S,
           state_gdn_conv, norm_mix_pre, norm_mix_post, norm_mlp_pre, norm_mlp_post, w_in, mlstm_gate_bias,
           gdn_conv_w, gdn_A_log, gdn_dt_bias, norm_mlstm, norm_ret, norm_gdn, w_out, w_up, w_down):
    bp, tp, d = x_prompt.shape
    bs, ts, _ = x_sample.shape
    depth = w_in.shape[0]
    rows_p = bp * tp
    rows_s = bs * ts
    rows = rows_p + rows_s

    x = jnp.concatenate([x_prompt.reshape(rows_p, d), x_sample.reshape(rows_s, d)], axis=0)
    w_in_t = jnp.swapaxes(w_in, 1, 2)
    w_in_p, w_gate_p = _repack_w_in(w_in_t, 0)
    gp = _gate_params(mlstm_gate_bias, gdn_dt_bias, gdn_A_log)
    na = norm_mlstm.reshape(depth, H_A, HEAD_DIM)
    nb = norm_ret.reshape(depth, H_B, HEAD_DIM)
    nc = norm_gdn.reshape(depth, 1, HEAD_DIM)

    nrep = jnp.repeat(state_mlstm_n.reshape(depth, bs, D_A), ts, axis=1)
    mrep = jnp.pad(jnp.repeat(state_mlstm_m, ts, axis=1), ((0, 0), (0, 0), (0, 128 - H_A)))
    s_states = (state_mlstm_C, nrep, mrep, state_ret_S, state_gdn_S, state_gdn_conv)

    p_tables = _prompt_tables(tp, PROMPT_CHUNK)
    s_tables = _sample_tables(ts, SAMPLE_BLOCK)

    tm = rows // 8
    h = _rmsnorm_rows(x, norm_mix_pre[0], tm // 2)
    p_states = []
    s_out = None
    for l in range(depth):
        proj, w_out_b, w_up_b = _matmul_in(h, w_in_p, w_out, w_up, l, tm, 2048)
        gates = _matmul_gates(h, w_gate_p, tm)
        outs = _prompt_mixers(proj, gates, p_tables, gp[l], gdn_conv_w[l], na[l], nb[l], nc[l], bp, tp,
                              PROMPT_CHUNK, PROMPT_SEQS)
        p_states.append(outs[1:])
        res = _sample_mixers(l, proj, gates, s_tables, gp[l], gdn_conv_w[l], na[l], nb[l], nc[l],
                             s_states, s_out, bs, ts, rows_p, SAMPLE_BLOCK)
        y_s, s_out = res[0], res[1:]
        x, h = _matmul_out(outs[0], y_s, w_out_b, x, norm_mix_post[l], norm_mlp_pre[l], rows_s)
        up = _matmul_up(h, w_up_b, w_down, w_in_t, l, tm, 2048)
        u, w_down_b = up[0], up[1]
        if l + 1 < depth:
            w_in_p, w_gate_p = up[2], up[3]
        x, h = _matmul_down(u, w_down_b, x, norm_mlp_post[l], norm_mix_pre[(l + 1) % depth], tm, 1024)

    stk = lambda j: jnp.stack([s[j] for s in p_states], axis=0)
    sc, sn, sm, sr, sg, sconv = s_out
    return (
        x[:rows_p].reshape(bp, tp, d), x[rows_p:].reshape(bs, ts, d),
        stk(0), stk(1), stk(2)[:, :, 0, :H_A], stk(3), stk(4), stk(5),
        sc, sn[:, ts - 1::ts, :].reshape(depth, bs, H_A, HEAD_DIM), sm[:, ts - 1::ts, :H_A], sr, sg,
        sconv,
    )
```

```python
import functools
import math
import types

import numpy as np
import jax
import jax.numpy as jnp
from jax import lax
from jax.experimental import pallas as pl
from jax.experimental.pallas import tpu as pltpu

f32 = jnp.float32
bf16 = jnp.bfloat16

D_MODEL = 2048
HEAD_DIM = 128
H_A, H_B, H_C = 4, 4, 8
D_A, D_B, D_C = H_A * HEAD_DIM, H_B * HEAD_DIM, H_C * HEAD_DIM
D_FF = 4 * D_MODEL
CONV_W = 4
PAST_LEN = 16384
ROPE_BASE = 10000.0
GATE_SOFTCAP = 15.0
NORM_EPS = 1e-6
QK_SCALE = HEAD_DIM ** -0.5

COL_A = 0
COL_B = 4 * D_A
COL_C = COL_B + 4 * D_B
N_PROJ = COL_C + 4 * D_C
LANE_I, LANE_F, LANE_DECAY, LANE_BETA = 0, H_A, 2 * H_A, 2 * H_A + H_C

PROMPT_CHUNK = 128
PROMPT_SEQS = 2
INV_BLOCK = 8
SAMPLE_BLOCK = 8
NEG_BIG = -1e30
VMEM_LIMIT = 60 * 1024 * 1024


def _cparams(sem):
    return pltpu.CompilerParams(dimension_semantics=sem, vmem_limit_bytes=VMEM_LIMIT)


def _rms(x, g):
    return x * lax.rsqrt(jnp.mean(x * x, axis=-1, keepdims=True) + NORM_EPS) * g


def _gather_rows_kernel(n_first, a_ref, b_ref, g_ref, xo_ref, ho_ref):
    x = jnp.where(pl.program_id(0) < n_first, a_ref[...], b_ref[...])
    xo_ref[...] = x
    ho_ref[...] = _rms(x, g_ref[...]).astype(ho_ref.dtype)


def _gather_rows(x_p, x_s, g, tm):
    d = x_p.shape[1]
    n_p, n_s = x_p.shape[0] // tm, x_s.shape[0] // tm
    m = (n_p + n_s) * tm
    row = lambda i: (i, 0)
    return pl.pallas_call(
        functools.partial(_gather_rows_kernel, n_p),
        grid=(n_p + n_s,),
        in_specs=[pl.BlockSpec((tm, d), lambda i: (jnp.minimum(i, n_p - 1), 0)),
                  pl.BlockSpec((tm, d), lambda i: (jnp.maximum(i - n_p, 0), 0)),
                  pl.BlockSpec((1, d), lambda i: (0, 0))],
        out_specs=[pl.BlockSpec((tm, d), row), pl.BlockSpec((tm, d), row)],
        out_shape=[jax.ShapeDtypeStruct((m, d), f32), jax.ShapeDtypeStruct((m, d), bf16)],
        compiler_params=_cparams(("parallel",)),
        name="gather_rows",
    )(x_p, x_s, g.reshape(1, d))


def _dot_tb(a, b):
    return lax.dot_general(a, b, (((1,), (1,)), ((), ())), preferred_element_type=f32)


def _mm_gates_kernel(a_ref, w_ref, o_ref):
    o_ref[...] = _dot_tb(a_ref[...], w_ref[...])


def _matmul_gates(h, w_t, tm):
    m, k = h.shape
    n = w_t.shape[0]
    return pl.pallas_call(
        _mm_gates_kernel,
        grid=(m // tm,),
        in_specs=[pl.BlockSpec((tm, k), lambda i: (i, 0)), pl.BlockSpec((n, k), lambda i: (0, 0))],
        out_specs=pl.BlockSpec((tm, n), lambda i: (i, 0)),
        out_shape=jax.ShapeDtypeStruct((m, n), f32),
        compiler_params=_cparams(("parallel",)),
        name="proj_gates",
    )(h, w_t)


def _mm_in_kernel(a_ref, w_ref, wo_ref, wu_ref, o_ref, wob_ref, wub_ref):
    o_ref[...] = _dot_tb(a_ref[...], w_ref[...])
    wob_ref[...] = wo_ref[...].astype(bf16)
    wub_ref[...] = wu_ref[...].astype(bf16)


def _matmul_in(h, w_t, w_out, w_up, layer, tm, tn):
    m, k = h.shape
    n = w_t.shape[0]
    nj, ni = n // tn, m // tm
    d = w_out.shape[1]
    f = w_up.shape[2]
    slab = d // (nj * ni)
    step = lambda j, i: j * ni + i
    side_in = lambda width: pl.BlockSpec((None, slab, width), lambda j, i: (layer, step(j, i), 0))
    side_out = lambda width: pl.BlockSpec((slab, width), lambda j, i: (step(j, i), 0))
    return pl.pallas_call(
        _mm_in_kernel,
        grid=(nj, ni),
        in_specs=[pl.BlockSpec((tm, k), lambda j, i: (i, 0)), pl.BlockSpec((tn, k), lambda j, i: (j, 0)),
                  side_in(d), side_in(f)],
        out_specs=[pl.BlockSpec((tm, tn), lambda j, i: (i, j)), side_out(d), side_out(f)],
        out_shape=[jax.ShapeDtypeStruct((m, n), f32), jax.ShapeDtypeStruct((d, d), bf16),
                   jax.ShapeDtypeStruct((d, f), bf16)],
        compiler_params=_cparams(("parallel", "arbitrary")),
        name="proj_in",
    )(h, w_t, w_out, w_up)


REPACK_LANES = 128


def _repack_slab(win_ref, main_ref, gate_ref):
    n_a = 4 * D_A
    n_ag = n_a + 2 * H_A
    n_bc = 4 * D_B + 4 * D_C
    main_ref[0:n_a, :] = win_ref[0:n_a, :].astype(bf16)
    main_ref[n_a:n_a + n_bc, :] = win_ref[n_ag:n_ag + n_bc, :].astype(bf16)
    gate_ref[...] = jnp.zeros_like(gate_ref)
    gate_ref[LANE_I:LANE_I + 2 * H_A, :] = win_ref[n_a:n_ag, :].astype(bf16)
    gate_ref[LANE_DECAY:LANE_DECAY + 2 * H_C, :] = win_ref[n_ag + n_bc:n_ag + n_bc + 2 * H_C, :].astype(bf16)


def _repack_part(win_ref, main_ref, gate_ref, part, parts):
    n_a = 4 * D_A
    n_ag = n_a + 2 * H_A
    n_bc = 4 * D_B + 4 * D_C
    runs_per_call = (N_PROJ // n_a) // parts
    for c in range(runs_per_call):
        dst = pl.multiple_of((part * runs_per_call + c) * n_a, 16)
        src = pl.multiple_of(dst + jnp.where(dst >= n_a, 2 * H_A, 0), 8)
        main_ref[pl.ds(dst, n_a), :] = win_ref[pl.ds(src, n_a), :].astype(bf16)
    gate_ref[...] = jnp.zeros_like(gate_ref)
    gate_ref[LANE_I:LANE_I + 2 * H_A, :] = win_ref[n_a:n_ag, :].astype(bf16)
    gate_ref[LANE_DECAY:LANE_DECAY + 2 * H_C, :] = win_ref[n_ag + n_bc:n_ag + n_bc + 2 * H_C, :].astype(bf16)


def _mm_up_kernel(repack_every, a_ref, w_ref, wd_ref, *rest):
    if repack_every:
        win_ref, o_ref, wdb_ref, main_ref, gate_ref = rest
        step = pl.program_id(0) * pl.num_programs(1) + pl.program_id(1)
        _repack_part(win_ref, main_ref, gate_ref, step % repack_every, repack_every)
    else:
        o_ref, wdb_ref = rest
    z = jnp.maximum(jnp.dot(a_ref[...], w_ref[...], preferred_element_type=f32), 0.0)
    o_ref[...] = (z * z).astype(o_ref.dtype)
    wdb_ref[...] = wd_ref[...].astype(bf16)


def _matmul_up(h, w, w_down, w_in_t, layer, tm, tn):
    m, k = h.shape
    n = w.shape[1]
    nj, ni = n // tn, m // tm
    steps = nj * ni
    step = lambda j, i: j * ni + i
    with_repack = layer + 1 < w_in_t.shape[0]
    dslab = w_down.shape[1] // steps
    d = w_down.shape[2]
    in_specs = [pl.BlockSpec((tm, k), lambda j, i: (i, 0)), pl.BlockSpec((k, tn), lambda j, i: (0, j)),
                pl.BlockSpec((None, dslab, d), lambda j, i: (layer, step(j, i), 0))]
    out_specs = [pl.BlockSpec((tm, tn), lambda j, i: (i, j)), pl.BlockSpec((dslab, d), lambda j, i: (step(j, i), 0))]
    out_shape = [jax.ShapeDtypeStruct((m, n), bf16), jax.ShapeDtypeStruct(w_down.shape[1:], bf16)]
    args = [h, w, w_down]
    every = 0
    if with_repack:
        n_in, dm = w_in_t.shape[1:]
        every = steps // (dm // REPACK_LANES)
        slab = lambda j, i: step(j, i) // every
        in_specs.append(pl.BlockSpec((None, n_in, REPACK_LANES), lambda j, i: (layer + 1, 0, slab(j, i))))
        out_specs += [pl.BlockSpec((N_PROJ, REPACK_LANES), lambda j, i: (0, slab(j, i))),
                      pl.BlockSpec((128, REPACK_LANES), lambda j, i: (0, slab(j, i)))]
        out_shape += [jax.ShapeDtypeStruct((N_PROJ, dm), bf16), jax.ShapeDtypeStruct((128, dm), bf16)]
        args.append(w_in_t)
    return pl.pallas_call(
        functools.partial(_mm_up_kernel, every),
        grid=(nj, ni),
        in_specs=in_specs,
        out_specs=out_specs,
        out_shape=out_shape,
        compiler_params=_cparams(("parallel", "arbitrary")),
        name="mlp_up",
    )(*args)


def _residual_epilogue(z, x_ref, gpost_ref, gnext_ref, xo_ref, ho_ref, rows=slice(None)):
    x_new = x_ref[rows, :] + _rms(z, gpost_ref[...])
    xo_ref[rows, :] = x_new
    ho_ref[rows, :] = _rms(x_new, gnext_ref[...]).astype(ho_ref.dtype)


OUT_ROW_PARTS = 4


def _mm_out_kernel(n_first, a_ref, b_ref, w_ref, x_ref, gpost_ref, gnext_ref, xo_ref, ho_ref):
    a = jnp.where(pl.program_id(0) < n_first, a_ref[...], b_ref[...])
    part = a.shape[0] // OUT_ROW_PARTS
    for r in range(OUT_ROW_PARTS):
        rows = slice(r * part, (r + 1) * part)
        z = jnp.dot(a[rows], w_ref[...], preferred_element_type=f32)
        _residual_epilogue(z, x_ref, gpost_ref, gnext_ref, xo_ref, ho_ref, rows)


def _matmul_out(y_p, y_s, w, x, g_post, g_next, tm):
    k = y_p.shape[1]
    n_p, n_s = y_p.shape[0] // tm, y_s.shape[0] // tm
    m = (n_p + n_s) * tm
    d = w.shape[1]
    row = lambda i: (i, 0)
    fixed = lambda i: (0, 0)
    return pl.pallas_call(
        functools.partial(_mm_out_kernel, n_p),
        grid=(n_p + n_s,),
        in_specs=[pl.BlockSpec((tm, k), lambda i: (jnp.minimum(i, n_p - 1), 0)),
                  pl.BlockSpec((tm, k), lambda i: (jnp.maximum(i - n_p, 0), 0)),
                  pl.BlockSpec((k, d), fixed), pl.BlockSpec((tm, d), row),
                  pl.BlockSpec((1, d), fixed), pl.BlockSpec((1, d), fixed)],
        out_specs=[pl.BlockSpec((tm, d), row), pl.BlockSpec((tm, d), row)],
        out_shape=[jax.ShapeDtypeStruct((m, d), f32), jax.ShapeDtypeStruct((m, d), bf16)],
        compiler_params=_cparams(("parallel",)),
        name="proj_out",
    )(y_p, y_s, w, x, g_post.reshape(1, d), g_next.reshape(1, d))


def _mm_down_kernel(a_ref, w_ref, x_ref, gpost_ref, gnext_ref, xo_ref, ho_ref):
    kk = pl.program_id(1)

    @pl.when(kk == 0)
    def _():
        xo_ref[...] = jnp.zeros_like(xo_ref)

    xo_ref[...] += jnp.dot(a_ref[...], w_ref[...], preferred_element_type=f32)

    @pl.when(kk == pl.num_programs(1) - 1)
    def _():
        _residual_epilogue(xo_ref[...], x_ref, gpost_ref, gnext_ref, xo_ref, ho_ref)


def _matmul_down(u, w, x, g_post, g_next, tm, tk):
    m, k = u.shape
    d = w.shape[1]
    row = lambda i, kk: (i, 0)
    fixed = lambda i, kk: (0, 0)
    return pl.pallas_call(
        _mm_down_kernel,
        grid=(m // tm, k // tk),
        in_specs=[pl.BlockSpec((tm, tk), lambda i, kk: (i, kk)),
                  pl.BlockSpec((tk, d), lambda i, kk: (kk, 0)),
                  pl.BlockSpec((tm, d), row), pl.BlockSpec((1, d), fixed), pl.BlockSpec((1, d), fixed)],
        out_specs=[pl.BlockSpec((tm, d), row), pl.BlockSpec((tm, d), row)],
        out_shape=[jax.ShapeDtypeStruct((m, d), f32), jax.ShapeDtypeStruct((m, d), bf16)],
        compiler_params=_cparams(("parallel", "arbitrary")),
        name="mlp_down",
    )(u, w, x, g_post.reshape(1, d), g_next.reshape(1, d))


def _dot(a, b):
    return jnp.dot(a, b, preferred_element_type=f32)


def _dot_ta(a, b):
    return lax.dot_general(a, b, (((0,), (0,)), ((), ())), preferred_element_type=f32)


def _lane_sum(x):
    return jnp.sum(x, axis=-1, keepdims=True)


def _softplus(z):
    return jnp.maximum(z, 0.0) + jnp.log1p(jnp.exp(-jnp.abs(z)))


def _gate_transform(gt, gp):
    z = gt + gp[0:1, :]
    capped = GATE_SOFTCAP * jnp.tanh(z / GATE_SOFTCAP)
    log_f = -_softplus(-capped)
    decay = -jnp.exp(gp[1:2, :]) * _softplus(z)
    beta = jax.nn.sigmoid(gt)
    return capped, log_f, decay, beta


def _head_norm(x, gain, center):
    if center:
        x = x - jnp.mean(x, axis=-1, keepdims=True)
    return x * lax.rsqrt(jnp.mean(x * x, axis=-1, keepdims=True) + NORM_EPS) * gain


def _l2norm(x):
    return x * lax.rsqrt(_lane_sum(x * x) + NORM_EPS)


def _silu(x):
    return x * jax.nn.sigmoid(x)


def _rotary(x, cos, sin_signed):
    return x * cos + pltpu.roll(x, HEAD_DIM // 2, 1) * sin_signed


def _retention_log_gamma():
    return np.log1p(-np.exp2(-5.0 - np.arange(H_B, dtype=np.float64)))


def _rope_tables(pos):
    half = HEAD_DIM // 2
    inv = ROPE_BASE ** (-np.arange(half, dtype=np.float64) / half)
    ang = np.asarray(pos, dtype=np.float64)[:, None] * inv[None, :]
    cos = np.concatenate([np.cos(ang), np.cos(ang)], axis=-1)
    sin = np.concatenate([-np.sin(ang), np.sin(ang)], axis=-1)
    return cos.astype(np.float32), sin.astype(np.float32)


def _scan_rows(x, row, length, op, fill):
    s = 1
    while s < length:
        x = op(x, jnp.where(row >= s, pltpu.roll(x, s, 0), fill))
        s *= 2
    return x


def _rows_to_lanes(x, length):
    if length < 128:
        x = jnp.concatenate([x, jnp.zeros((128 - length, 128), x.dtype)], axis=0)
    return x.T[:, :length]


def _prompt_mixer_kernel(L, NS, *refs):
    proj_refs, gt_refs = refs[:NS], refs[NS:2 * NS]
    (cos_ref, sin_ref, rd_ref, rq_ref, rk_ref, gp_ref, cw_ref, na_ref, nb_ref, nc_ref,
     y_ref, c_ref, n_ref, m_ref, r_ref, g_ref, conv_ref, ext_ref) = refs[2 * NS:]
    c = pl.program_id(1)

    @pl.when(c == 0)
    def _():
        c_ref[...] = jnp.zeros_like(c_ref)
        n_ref[...] = jnp.zeros_like(n_ref)
        m_ref[...] = jnp.zeros_like(m_ref)
        r_ref[...] = jnp.zeros_like(r_ref)
        g_ref[...] = jnp.zeros_like(g_ref)
        ext_ref[:, 0:8, :] = jnp.zeros((NS, 8, 3 * D_C), f32)

    row = lax.broadcasted_iota(jnp.int32, (L, 128), 0)
    lane = lax.broadcasted_iota(jnp.int32, (L, 128), 1)
    ri = lax.broadcasted_iota(jnp.int32, (L, L), 0)
    ci = lax.broadcasted_iota(jnp.int32, (L, L), 1)
    incl = ri >= ci
    strict = ri > ci
    head_lane = lane < H_A
    seqs = range(NS)

    def gate_block(s):
        g = types.SimpleNamespace()
        capped, log_f, decay, g.beta = _gate_transform(gt_refs[s][...], gp_ref[...])
        ig = jnp.where(head_lane, capped, 0.0)
        lf = jnp.where(head_lane, pltpu.roll(log_f, 128 - LANE_F, 1), 0.0)
        mp = m_ref[s]
        F = _scan_rows(lf, row, L, jnp.add, 0.0)
        m = F + jnp.maximum(mp, _scan_rows(ig - F, row, L, jnp.maximum, NEG_BIG))
        m_new = m[L - 1:L, :]
        f_last = F[L - 1:L, :]
        g.a_rows = F - m
        g.inter = jnp.exp(F + mp - m)
        g.inv_floor = jnp.exp(-m)
        g.wl = jnp.exp(ig + f_last - F - m_new)
        g.dec = jnp.exp(f_last + mp - m_new)
        m_ref[s] = m_new
        g.G = _scan_rows(decay, row, L, jnp.add, 0.0)
        g.b_lanes = g.g_lanes = _rows_to_lanes(jnp.where(head_lane, ig - F, g.G), L)
        g.gam = jnp.exp(g.G)
        g_last = g.G[L - 1:L, :]
        g.k_decay = jnp.exp(g_last - g.G)
        g.s_decay = jnp.exp(g_last)
        return g

    gs = [gate_block(s) for s in seqs]

    def mlstm_head(s, h):
        proj_ref, g = proj_refs[s], gs[s]
        q = proj_ref[:, COL_A + h * HEAD_DIM:COL_A + (h + 1) * HEAD_DIM]
        k = proj_ref[:, COL_A + D_A + h * HEAD_DIM:COL_A + D_A + (h + 1) * HEAD_DIM] * QK_SCALE
        v = proj_ref[:, COL_A + 2 * D_A + h * HEAD_DIM:COL_A + 2 * D_A + (h + 1) * HEAD_DIM]
        og = proj_ref[:, COL_A + 3 * D_A + h * HEAD_DIM:COL_A + 3 * D_A + (h + 1) * HEAD_DIM]
        logw = g.a_rows[:, h:h + 1] + g.b_lanes[h:h + 1, :]
        sc = _dot_tb(q, k) * jnp.exp(jnp.where(incl, logw, NEG_BIG))
        c_old = c_ref[s, h]
        n_old = n_ref[s, h:h + 1, :]
        inter_h = g.inter[:, h:h + 1]
        num = inter_h * _dot(q, c_old) + _dot(sc, v)
        den = inter_h * _lane_sum(q * n_old) + _lane_sum(sc)
        hh = num / jnp.maximum(jnp.abs(den), g.inv_floor[:, h:h + 1])
        kw = k * g.wl[:, h:h + 1]
        dec_h = g.dec[:, h:h + 1]
        c_ref[s, h] = dec_h * c_old + _dot_ta(kw, v)
        n_ref[s, h:h + 1, :] = dec_h * n_old + jnp.sum(kw, axis=0, keepdims=True)
        ya = jax.nn.sigmoid(og) * _head_norm(hh, na_ref[h:h + 1, :], False)
        y_ref[s, :, h * HEAD_DIM:(h + 1) * HEAD_DIM] = ya.astype(y_ref.dtype)

    lg = _retention_log_gamma()

    def retention_head(s, h):
        proj_ref = proj_refs[s]
        cos = cos_ref[...]
        sin = sin_ref[...]
        q = proj_ref[:, COL_B + h * HEAD_DIM:COL_B + (h + 1) * HEAD_DIM]
        k = proj_ref[:, COL_B + D_B + h * HEAD_DIM:COL_B + D_B + (h + 1) * HEAD_DIM]
        v = proj_ref[:, COL_B + 2 * D_B + h * HEAD_DIM:COL_B + 2 * D_B + (h + 1) * HEAD_DIM]
        gb = proj_ref[:, COL_B + 3 * D_B + h * HEAD_DIM:COL_B + 3 * D_B + (h + 1) * HEAD_DIM]
        qr = _rotary(q, cos, sin)
        kr = _rotary(k, cos, sin) * QK_SCALE
        s_old = r_ref[s, h]
        inner = _dot_tb(qr, kr) * rd_ref[h]
        o = _dot(inner, v) + rq_ref[h] * _dot(qr, s_old)
        r_ref[s, h] = float(np.exp(L * lg[h])) * s_old + _dot_ta(kr * rk_ref[h], v)
        yb = _silu(gb) * _head_norm(o, nb_ref[h:h + 1, :], True)
        y_ref[s, :, D_A + h * HEAD_DIM:D_A + (h + 1) * HEAD_DIM] = yb.astype(y_ref.dtype)

    fillers = [functools.partial(mlstm_head, s, h) for h in range(H_A) for s in seqs]
    fillers += [functools.partial(retention_head, s, h) for h in range(H_B) for s in seqs]

    def emit_fillers(count):
        for _ in range(min(count, len(fillers))):
            fillers.pop(0)()

    n_stages = max(int(math.log2(min(INV_BLOCK, L))) - 1, 0) + int(math.log2(L // min(INV_BLOCK, L)))
    per_stage = -(-len(fillers) // max(n_stages, 1))

    for s in seqs:
        ext_ref[s, 8:8 + L, :] = proj_refs[s][:, COL_C:COL_C + 3 * D_C]

    def conv_block(s, col):
        acc = cw_ref[CONV_W - 1:CONV_W, col:col + HEAD_DIM] * ext_ref[s, 8:8 + L, col:col + HEAD_DIM]
        for w in range(CONV_W - 1):
            off = 8 - (CONV_W - 1) + w
            acc = acc + cw_ref[w:w + 1, col:col + HEAD_DIM] * ext_ref[s, off:off + L, col:col + HEAD_DIM]
        return _silu(acc)

    pairs = [(s, h) for h in range(H_C) for s in seqs]
    heads = range(len(pairs))
    dcol = [LANE_DECAY + h for _, h in pairs]
    gam = [gs[s].gam for s, _ in pairs]
    beta_c = [gs[s].beta[:, LANE_BETA + h:LANE_BETA + h + 1] for s, h in pairs]
    q = [_l2norm(conv_block(s, h * HEAD_DIM)) * QK_SCALE for s, h in pairs]
    k = [_l2norm(conv_block(s, D_C + h * HEAD_DIM)) for s, h in pairs]
    v = [conv_block(s, 2 * D_C + h * HEAD_DIM) for s, h in pairs]
    dec_in = [jnp.exp(jnp.where(incl, gs[s].G[:, LANE_DECAY + h:LANE_DECAY + h + 1]
                                - gs[s].g_lanes[LANE_DECAY + h:LANE_DECAY + h + 1, :], NEG_BIG)) for s, h in pairs]
    a_mat = [jnp.where(strict, dec_in[h], 0.0) * beta_c[h] * _dot_tb(k[h], k[h]) for h in heads]
    qk = [_dot_tb(q[h], k[h]) * dec_in[h] for h in heads]
    blk = min(INV_BLOCK, L)
    same = (ri // blk) == (ci // blk)
    pw = [jnp.where(same, -a, 0.0) for a in a_mat]
    e_mat = list(pw)
    span = 1
    while 2 * span < blk:
        pw = [_dot(p, p) for p in pw]
        e_mat = [e_mat[h] + pw[h] + _dot(e_mat[h], pw[h]) for h in heads]
        emit_fillers(per_stage)
        span *= 2
    while blk < L:
        wider = (ri // (2 * blk)) == (ci // (2 * blk))
        a_off = [jnp.where(wider & jnp.logical_not(same), a, 0.0) for a in a_mat]
        low = [a_off[h] + _dot(e_mat[h], a_off[h]) for h in heads]
        e_mat = [e_mat[h] - low[h] - _dot(low[h], e_mat[h]) for h in heads]
        emit_fillers(per_stage)
        same = wider
        blk *= 2
    rhs = [jnp.concatenate([(beta_c[h] * gam[h][:, dcol[h]:dcol[h] + 1]) * k[h], beta_c[h] * v[h]], axis=1)
           for h in heads]
    sol = [rhs[h] + _dot(e_mat[h], rhs[h]) for h in heads]
    emit_fillers(len(fillers))
    s_old = [g_ref[s, hd] for s, hd in pairs]
    both = [_dot(jnp.concatenate([sol[h][:, :HEAD_DIM], q[h]], axis=0), s_old[h]) for h in heads]
    u = [sol[h][:, HEAD_DIM:] - both[h][:L] for h in heads]
    o = [gam[h][:, dcol[h]:dcol[h] + 1] * both[h][L:] + _dot(qk[h], u[h]) for h in heads]
    for h, (s, hd) in enumerate(pairs):
        g_ref[s, hd] = (gs[s].s_decay[:, dcol[h]:dcol[h] + 1] * s_old[h]
                        + _dot_ta(k[h] * gs[s].k_decay[:, dcol[h]:dcol[h] + 1], u[h]))
    for h, (s, hd) in enumerate(pairs):
        gz = proj_refs[s][:, COL_C + 3 * D_C + hd * HEAD_DIM:COL_C + 3 * D_C + (hd + 1) * HEAD_DIM]
        yc = _silu(gz) * _head_norm(o[h], nc_ref[...], False)
        y_ref[s, :, D_A + D_B + hd * HEAD_DIM:D_A + D_B + (hd + 1) * HEAD_DIM] = yc.astype(y_ref.dtype)

    for s in seqs:
        ext_ref[s, 0:8, :] = ext_ref[s, L:L + 8, :]

    @pl.when(c == pl.num_programs(1) - 1)
    def _():
        for s in seqs:
            conv_ref[s] = ext_ref[s, 8 + L - (CONV_W - 1):8 + L, :]


def _prompt_tables(seq, L):
    lg = _retention_log_gamma()
    i = np.arange(L, dtype=np.float64)
    diff = i[:, None] - i[None, :]
    rd = np.where(diff >= 0, np.exp(np.maximum(diff, 0.0) * lg[:, None, None]), 0.0)
    rq = np.broadcast_to(np.exp((i + 1.0) * lg[:, None])[..., None], (H_B, L, HEAD_DIM))
    rk = np.broadcast_to(np.exp((L - 1.0 - i) * lg[:, None])[..., None], (H_B, L, HEAD_DIM))
    cos, sin = _rope_tables(np.arange(seq))
    return (jnp.asarray(cos), jnp.asarray(sin), jnp.asarray(rd, f32), jnp.asarray(rq, f32), jnp.asarray(rk, f32))


def _prompt_mixers(proj, gates, tables, gp, cw, na, nb, nc, batch, seq, L, NS):
    nchunk = seq // L
    cos, sin, rd, rq, rk = tables
    full = lambda shape: pl.BlockSpec(shape, lambda b, c: (0,) * len(shape))
    state4 = lambda heads: pl.BlockSpec((NS, heads, HEAD_DIM, HEAD_DIM), lambda b, c: (b, 0, 0, 0))
    rows_of = lambda s, width: pl.BlockSpec((L, width), lambda b, c: ((b * NS + s) * nchunk + c, 0))
    outs = pl.pallas_call(
        functools.partial(_prompt_mixer_kernel, L, NS),
        grid=(batch // NS, nchunk),
        in_specs=[rows_of(s, N_PROJ) for s in range(NS)] + [rows_of(s, 128) for s in range(NS)] + [
            pl.BlockSpec((L, HEAD_DIM), lambda b, c: (c, 0)),
            pl.BlockSpec((L, HEAD_DIM), lambda b, c: (c, 0)),
            full((H_B, L, L)), full((H_B, L, HEAD_DIM)), full((H_B, L, HEAD_DIM)),
            full((8, 128)), full((CONV_W, 3 * D_C)),
            full((H_A, HEAD_DIM)), full((H_B, HEAD_DIM)), full((1, HEAD_DIM)),
        ],
        out_specs=[
            pl.BlockSpec((None, NS, None, L, D_MODEL), lambda b, c: (b, 0, c, 0, 0)),
            state4(H_A),
            pl.BlockSpec((NS, H_A, HEAD_DIM), lambda b, c: (b, 0, 0)),
            pl.BlockSpec((NS, 1, 128), lambda b, c: (b, 0, 0)),
            state4(H_B),
            state4(H_C),
            pl.BlockSpec((NS, CONV_W - 1, 3 * D_C), lambda b, c: (b, 0, 0)),
        ],
        out_shape=[
            jax.ShapeDtypeStruct((batch // NS, NS, nchunk, L, D_MODEL), bf16),
            jax.ShapeDtypeStruct((batch, H_A, HEAD_DIM, HEAD_DIM), f32),
            jax.ShapeDtypeStruct((batch, H_A, HEAD_DIM), f32),
            jax.ShapeDtypeStruct((batch, 1, 128), f32),
            jax.ShapeDtypeStruct((batch, H_B, HEAD_DIM, HEAD_DIM), f32),
            jax.ShapeDtypeStruct((batch, H_C, HEAD_DIM, HEAD_DIM), f32),
            jax.ShapeDtypeStruct((batch, CONV_W - 1, 3 * D_C), f32),
        ],
        scratch_shapes=[pltpu.VMEM((NS, L + 8, 3 * D_C), f32)],
        compiler_params=_cparams(("parallel", "arbitrary")),
        name="prompt_mixers",
    )(*([proj] * NS + [gates] * NS), cos, sin, rd, rq, rk, gp, cw, na, nb, nc)
    return [outs[0].reshape(batch * seq, D_MODEL)] + list(outs[1:])


def _sample_mixer_kernel(T, BB, layer_first, *refs):
    (proj_ref, gt_ref, cos_ref, sin_ref, rt_ref, gp_ref, cw_ref, na_ref, nb_ref, nc_ref,
     c0_ref, nrep_ref, mrep_ref, r0_ref, g0_ref, conv0_ref) = refs[:16]
    rest = refs[16 + (0 if layer_first else 6):]
    (y_ref, c_ref, nout_ref, mout_ref, r_ref, g_ref, conv_ref, cv_ref, sa_ref, ext_ref) = rest
    R = T * BB
    row = lax.broadcasted_iota(jnp.int32, (R, 128), 0)
    lane = lax.broadcasted_iota(jnp.int32, (R, 128), 1)
    t = lax.rem(row, T)
    tcol = t[:, 0:1]

    def shift(x, s):
        return x if s == 0 else pltpu.roll(x, s, 0)

    def seg_scan(x, op, fill):
        s = 1
        while s < T:
            x = op(x, jnp.where(t >= s, shift(x, s), fill))
            s *= 2
        return x

    def last_rep(x):
        x_last = jnp.where(t == T - 1, x, 0.0)
        out = x_last
        for s in range(1, T):
            out = out + pltpu.roll(x_last, R - s, 0)
        return out

    capped, log_f, decay, beta = _gate_transform(gt_ref[...], gp_ref[...])

    head_lane = lane < H_A
    ig = jnp.where(head_lane, capped, 0.0)
    lf = jnp.where(head_lane, pltpu.roll(log_f, 128 - LANE_F, 1), 0.0)
    mp = mrep_ref[...]
    F = seg_scan(lf, jnp.add, 0.0)
    m = F + jnp.maximum(mp, seg_scan(ig - F, jnp.maximum, NEG_BIG))
    m_new = last_rep(m)
    f_last = last_rep(F)
    inter = jnp.exp(F + mp - m)
    inv_floor = jnp.exp(-m)
    wl = jnp.exp(ig + f_last - F - m_new)
    dec = jnp.exp(f_last + mp - m_new)
    mout_ref[...] = m_new
    pw = [jnp.where(t >= s, jnp.exp(F - shift(F, s) + shift(ig, s) - m), 0.0) for s in range(T)]

    def seq(x, b):
        return x[b * T:(b + 1) * T]

    slot_a, slot_b, slot_w, slot_q = 0, H_A, H_A + H_B, H_A + H_B + H_C

    qa = [proj_ref[:, COL_A + h * HEAD_DIM:COL_A + (h + 1) * HEAD_DIM] for h in range(H_A)]
    for h in range(H_A):
        for b in range(BB):
            sa_ref[slot_a + h, b * T:(b + 1) * T, :] = _dot(seq(qa[h], b), c0_ref[b, h])
    cos = cos_ref[...]
    sin = sin_ref[...]
    lg = _retention_log_gamma()
    qr = [_rotary(proj_ref[:, COL_B + h * HEAD_DIM:COL_B + (h + 1) * HEAD_DIM], cos, sin) for h in range(H_B)]
    kr = [_rotary(proj_ref[:, COL_B + D_B + h * HEAD_DIM:COL_B + D_B + (h + 1) * HEAD_DIM], cos, sin) * QK_SCALE
          for h in range(H_B)]
    for h in range(H_B):
        for b in range(BB):
            sa_ref[slot_b + h, b * T:(b + 1) * T, :] = _dot(seq(qr[h], b), r0_ref[b, h])

    ka = [proj_ref[:, COL_A + D_A + h * HEAD_DIM:COL_A + D_A + (h + 1) * HEAD_DIM] * QK_SCALE for h in range(H_A)]
    va = [proj_ref[:, COL_A + 2 * D_A + h * HEAD_DIM:COL_A + 2 * D_A + (h + 1) * HEAD_DIM] for h in range(H_A)]
    n_old = [nrep_ref[:, h * HEAD_DIM:(h + 1) * HEAD_DIM] for h in range(H_A)]
    kw = [ka[h] * wl[:, h:h + 1] for h in range(H_A)]
    for h in range(H_A):
        for b in range(BB):
            dec_bh = dec[b * T:b * T + 1, h:h + 1]
            c_ref[b, h] = dec_bh * c0_ref[b, h] + _dot_ta(seq(kw[h], b), seq(va[h], b))
        nout_ref[:, h * HEAD_DIM:(h + 1) * HEAD_DIM] = dec[:, h:h + 1] * n_old[h] + seg_scan(kw[h], jnp.add, 0.0)
    vb = [proj_ref[:, COL_B + 2 * D_B + h * HEAD_DIM:COL_B + 2 * D_B + (h + 1) * HEAD_DIM] for h in range(H_B)]
    for h in range(H_B):
        kd = kr[h] * rt_ref[:, H_B + h:H_B + h + 1]
        for b in range(BB):
            r_ref[b, h] = float(np.exp(T * lg[h])) * r0_ref[b, h] + _dot_ta(seq(kd, b), seq(vb[h], b))

    for b in range(BB):
        ext_ref[b, 0:CONV_W - 1, :] = conv0_ref[b]
        ext_ref[b, CONV_W - 1:CONV_W - 1 + T, :] = proj_ref[b * T:(b + 1) * T, COL_C:COL_C + 3 * D_C]
        acc = cw_ref[0:1, :] * ext_ref[b, 0:T, :]
        for w in range(1, CONV_W):
            acc = acc + cw_ref[w:w + 1, :] * ext_ref[b, w:w + T, :]
        cv_ref[b * T:(b + 1) * T, :] = _silu(acc)
        conv_ref[b] = ext_ref[b, T:T + CONV_W - 1, :]

    G = seg_scan(decay, jnp.add, 0.0)
    gam = jnp.exp(G)
    g_last = last_rep(G)
    k_decay = jnp.exp(g_last - G)
    s_decay = jnp.exp(g_last)
    dshift = [None] + [jnp.where(t >= s, jnp.exp(G - shift(G, s)), 0.0) for s in range(1, T)]

    heads = range(H_C)
    dcol = [LANE_DECAY + h for h in heads]
    beta_c = [beta[:, LANE_BETA + h:LANE_BETA + h + 1] for h in heads]
    qg = [_l2norm(cv_ref[:, h * HEAD_DIM:(h + 1) * HEAD_DIM]) * QK_SCALE for h in heads]
    kg = [_l2norm(cv_ref[:, D_C + h * HEAD_DIM:D_C + (h + 1) * HEAD_DIM]) for h in heads]
    vg = [cv_ref[:, 2 * D_C + h * HEAD_DIM:2 * D_C + (h + 1) * HEAD_DIM] for h in heads]
    a_sub = [[None] + [beta_c[h] * dshift[s][:, dcol[h]:dcol[h] + 1] * _lane_sum(kg[h] * shift(kg[h], s))
                       for s in range(1, T)] for h in heads]
    rhs_w = [(beta_c[h] * gam[:, dcol[h]:dcol[h] + 1]) * kg[h] for h in heads]
    rhs_u = [beta_c[h] * vg[h] for h in heads]
    w_sol, u_sol = list(rhs_w), list(rhs_u)
    for i in range(1, T):
        for h in heads:
            upd_w = a_sub[h][1] * shift(w_sol[h], 1)
            upd_u = a_sub[h][1] * shift(u_sol[h], 1)
            for s in range(2, i + 1):
                upd_w = upd_w + a_sub[h][s] * shift(w_sol[h], s)
                upd_u = upd_u + a_sub[h][s] * shift(u_sol[h], s)
            w_sol[h] = jnp.where(t == i, rhs_w[h] - upd_w, w_sol[h])
            u_sol[h] = jnp.where(t == i, rhs_u[h] - upd_u, u_sol[h])
    for h in heads:
        for b in range(BB):
            both = _dot(jnp.concatenate([seq(w_sol[h], b), seq(qg[h], b)], axis=0), g0_ref[b, h])
            sa_ref[slot_w + h, b * T:(b + 1) * T, :] = both[:T]
            sa_ref[slot_q + h, b * T:(b + 1) * T, :] = both[T:]

    ug = [u_sol[h] - sa_ref[slot_w + h] for h in heads]
    for h in heads:
        kd = kg[h] * k_decay[:, dcol[h]:dcol[h] + 1]
        for b in range(BB):
            g_ref[b, h] = (s_decay[b * T:b * T + 1, dcol[h]:dcol[h] + 1] * g0_ref[b, h]
                           + _dot_ta(seq(kd, b), seq(ug[h], b)))

    sc_a = [[_lane_sum(qa[h] * shift(ka[h], s)) * pw[s][:, h:h + 1] for s in range(T)] for h in range(H_A)]
    qn = [_lane_sum(qa[h] * n_old[h]) for h in range(H_A)]
    sc_b = [[jnp.where(tcol >= s, _lane_sum(qr[h] * shift(kr[h], s)) * float(np.exp(s * lg[h])), 0.0)
             for s in range(T)] for h in range(H_B)]
    sc_c = [[_lane_sum(qg[h] * shift(kg[h], s)) * (1.0 if s == 0 else dshift[s][:, dcol[h]:dcol[h] + 1])
             for s in range(T)] for h in heads]
    hid_a, hid_b, hid_c = [], [], []
    for h in range(H_A):
        inter_h = inter[:, h:h + 1]
        num = inter_h * sa_ref[slot_a + h]
        den = inter_h * qn[h]
        for s in range(T):
            num = num + sc_a[h][s] * shift(va[h], s)
            den = den + sc_a[h][s]
        hid_a.append(num / jnp.maximum(jnp.abs(den), inv_floor[:, h:h + 1]))
    for h in range(H_B):
        o = rt_ref[:, h:h + 1] * sa_ref[slot_b + h]
        for s in range(T):
            o = o + sc_b[h][s] * shift(vb[h], s)
        hid_b.append(o)
    for h in heads:
        o = gam[:, dcol[h]:dcol[h] + 1] * sa_ref[slot_q + h]
        for s in range(T):
            o = o + sc_c[h][s] * shift(ug[h], s)
        hid_c.append(o)
    mean_b = [jnp.mean(x, axis=-1, keepdims=True) for x in hid_b]
    hid_b = [x - mu for x, mu in zip(hid_b, mean_b)]
    hidden = hid_a + hid_b + hid_c
    inv_rms = [lax.rsqrt(jnp.mean(x * x, axis=-1, keepdims=True) + NORM_EPS) for x in hidden]
    for h in range(H_A):
        og = proj_ref[:, COL_A + 3 * D_A + h * HEAD_DIM:COL_A + 3 * D_A + (h + 1) * HEAD_DIM]
        ya = jax.nn.sigmoid(og) * (hidden[h] * inv_rms[h] * na_ref[h:h + 1, :])
        y_ref[:, h * HEAD_DIM:(h + 1) * HEAD_DIM] = ya.astype(y_ref.dtype)
    for h in range(H_B):
        gb = proj_ref[:, COL_B + 3 * D_B + h * HEAD_DIM:COL_B + 3 * D_B + (h + 1) * HEAD_DIM]
        yb = _silu(gb) * (hidden[H_A + h] * inv_rms[H_A + h] * nb_ref[h:h + 1, :])
        y_ref[:, D_A + h * HEAD_DIM:D_A + (h + 1) * HEAD_DIM] = yb.astype(y_ref.dtype)
    for h in heads:
        gz = proj_ref[:, COL_C + 3 * D_C + h * HEAD_DIM:COL_C + 3 * D_C + (h + 1) * HEAD_DIM]
        yc = _silu(gz) * (hidden[H_A + H_B + h] * inv_rms[H_A + H_B + h] * nc_ref[...])
        y_ref[:, D_A + D_B + h * HEAD_DIM:D_A + D_B + (h + 1) * HEAD_DIM] = yc.astype(y_ref.dtype)


def _sample_tables(T, BB):
    lg = _retention_log_gamma()
    tt = np.arange(T, dtype=np.float64)
    rt = np.zeros((T, 128), np.float64)
    rt[:, 0:H_B] = np.exp((tt[:, None] + 1.0) * lg[None, :])
    rt[:, H_B:2 * H_B] = np.exp((T - 1.0 - tt[:, None]) * lg[None, :])
    cos, sin = _rope_tables(PAST_LEN + np.arange(T))
    rep = lambda a: jnp.asarray(np.tile(a, (BB, 1)), f32)
    return rep(cos), rep(sin), rep(rt)


def _sample_mixers(layer, proj, gates, tables, gp, cw, na, nb, nc, states, prev_out, batch, T, row0, BB):
    c0, nrep, mrep, r0, g0, conv0 = states
    R = T * BB
    nblk = batch // BB
    blk0 = row0 // R
    cos, sin, rt = tables
    full = lambda shape: pl.BlockSpec(shape, lambda i: (0,) * len(shape))
    st4 = lambda heads: pl.BlockSpec((None, BB, heads, HEAD_DIM, HEAD_DIM), lambda i: (layer, i, 0, 0, 0))
    rows = lambda width: pl.BlockSpec((None, R, width), lambda i: (layer, i, 0))
    convspec = pl.BlockSpec((None, BB, CONV_W - 1, 3 * D_C), lambda i: (layer, i, 0, 0))
    anyspec = pl.BlockSpec(memory_space=pl.ANY)
    in_specs = [
        pl.BlockSpec((R, N_PROJ), lambda i: (blk0 + i, 0)),
        pl.BlockSpec((R, 128), lambda i: (blk0 + i, 0)),
        full((R, HEAD_DIM)), full((R, HEAD_DIM)), full((R, 128)),
        full((8, 128)), full((CONV_W, 3 * D_C)),
        full((H_A, HEAD_DIM)), full((H_B, HEAD_DIM)), full((1, HEAD_DIM)),
        st4(H_A), rows(D_A), rows(128), st4(H_B), st4(H_C), convspec,
    ]
    args = [proj, gates, cos, sin, rt, gp, cw, na, nb, nc, c0, nrep, mrep, r0, g0, conv0]
    out_shape = [
        jax.ShapeDtypeStruct((batch * T, D_MODEL), bf16),
        jax.ShapeDtypeStruct(c0.shape, f32),
        jax.ShapeDtypeStruct(nrep.shape, f32),
        jax.ShapeDtypeStruct(mrep.shape, f32),
        jax.ShapeDtypeStruct(r0.shape, f32),
        jax.ShapeDtypeStruct(g0.shape, f32),
        jax.ShapeDtypeStruct(conv0.shape, f32),
    ]
    out_specs = [pl.BlockSpec((R, D_MODEL), lambda i: (i, 0)),
                 st4(H_A), rows(D_A), rows(128), st4(H_B), st4(H_C), convspec]
    first = prev_out is None
    aliases = {}
    if not first:
        in_specs += [anyspec] * 6
        args += list(prev_out)
        aliases = {16 + j: 1 + j for j in range(6)}
    return pl.pallas_call(
        functools.partial(_sample_mixer_kernel, T, BB, first),
        grid=(nblk,),
        in_specs=in_specs,
        out_specs=out_specs,
        out_shape=out_shape,
        input_output_aliases=aliases,
        scratch_shapes=[pltpu.VMEM((R, 3 * D_C), f32), pltpu.VMEM((H_A + H_B + 2 * H_C, R, HEAD_DIM), f32),
                        pltpu.VMEM((BB, 8, 3 * D_C), f32)],
        compiler_params=_cparams(("parallel",)),
        name="sample_mixers",
    )(*args)


def _repack_kernel(win_ref, main_ref, gate_ref):
    _repack_slab(win_ref, main_ref, gate_ref)


def _repack_w_in(w_in_t, layer):
    _, n_in, dm = w_in_t.shape
    return pl.pallas_call(
        _repack_kernel,
        grid=(dm // REPACK_LANES,),
        in_specs=[pl.BlockSpec((None, n_in, REPACK_LANES), lambda i: (layer, 0, i))],
        out_specs=[pl.BlockSpec((N_PROJ, REPACK_LANES), lambda i: (0, i)),
                   pl.BlockSpec((128, REPACK_LANES), lambda i: (0, i))],
        out_shape=[jax.ShapeDtypeStruct((N_PROJ, dm), bf16), jax.ShapeDtypeStruct((128, dm), bf16)],
        compiler_params=_cparams(("parallel",)),
        name="repack_w_in",
    )(w_in_t)


def _gate_params(gate_bias, dt_bias, a_log):
    depth = gate_bias.shape[0]
    gp = jnp.zeros((depth, 8, 128), f32)
    gp = gp.at[:, 0, LANE_I:LANE_I + 2 * H_A].set(gate_bias)
    gp = gp.at[:, 0, LANE_DECAY:LANE_DECAY + H_C].set(dt_bias)
    gp = gp.at[:, 1, LANE_DECAY:LANE_DECAY + H_C].set(a_log)
    return gp


def kernel(x_prompt, x_sample, state_mlstm_C, state_mlstm_n, state_mlstm_m, state_ret_S, state_gdn_S,
           state_gdn_conv, norm_mix_pre, norm_mix_post, norm_mlp_pre, norm_mlp_post, w_in, mlstm_gate_bias,
           gdn_conv_w, gdn_A_log, gdn_dt_bias, norm_mlstm, norm_ret, norm_gdn, w_out, w_up, w_down):
    bp, tp, d = x_prompt.shape
    bs, ts, _ = x_sample.shape
    depth = w_in.shape[0]
    rows_p = bp * tp
    rows_s = bs * ts
    rows = rows_p + rows_s

    w_in_t = jnp.swapaxes(w_in, 1, 2)
    w_in_p, w_gate_p = _repack_w_in(w_in_t, 0)
    gp = _gate_params(mlstm_gate_bias, gdn_dt_bias, gdn_A_log)
    na = norm_mlstm.reshape(depth, H_A, HEAD_DIM)
    nb = norm_ret.reshape(depth, H_B, HEAD_DIM)
    nc = norm_gdn.reshape(depth, 1, HEAD_DIM)

    nrep = jnp.repeat(state_mlstm_n.reshape(depth, bs, D_A), ts, axis=1)
    mrep = jnp.pad(jnp.repeat(state_mlstm_m, ts, axis=1), ((0, 0), (0, 0), (0, 128 - H_A)))
    s_states = (state_mlstm_C, nrep, mrep, state_ret_S, state_gdn_S, state_gdn_conv)

    p_tables = _prompt_tables(tp, PROMPT_CHUNK)
    s_tables = _sample_tables(ts, SAMPLE_BLOCK)

    tm = rows // 8
    x, h = _gather_rows(x_prompt.reshape(rows_p, d), x_sample.reshape(rows_s, d), norm_mix_pre[0], rows_s)
    p_states = []
    s_out = None
    for l in range(depth):
        proj, w_out_b, w_up_b = _matmul_in(h, w_in_p, w_out, w_up, l, tm, 2048)
        gates = _matmul_gates(h, w_gate_p, tm)
        outs = _prompt_mixers(proj, gates, p_tables, gp[l], gdn_conv_w[l], na[l], nb[l], nc[l], bp, tp,
                              PROMPT_CHUNK, PROMPT_SEQS)
        p_states.append(outs[1:])
        res = _sample_mixers(l, proj, gates, s_tables, gp[l], gdn_conv_w[l], na[l], nb[l], nc[l],
                             s_states, s_out, bs, ts, rows_p, SAMPLE_BLOCK)
        y_s, s_out = res[0], res[1:]
        x, h = _matmul_out(outs[0], y_s, w_out_b, x, norm_mix_post[l], norm_mlp_pre[l], rows_s)
        up = _matmul_up(h, w_up_b, w_down, w_in_t, l, tm, 2048)
        u, w_down_b = up[0], up[1]
        if l + 1 < depth:
            w_in_p, w_gate_p = up[2], up[3]
        x, h = _matmul_down(u, w_down_b, x, norm_mlp_post[l], norm_mix_pre[(l + 1) % depth], tm, 1024)

    stk = lambda j: jnp.stack([s[j] for s in p_states], axis=0)
    sc, sn, sm, sr, sg, sconv = s_out
    return (
        x[:rows_p].reshape(bp, tp, d), x[rows_p:].reshape(bs, ts, d),
        stk(0), stk(1), stk(2)[:, :, 0, :H_A], stk(3), stk(4), stk(5),
        sc, sn[:, ts - 1::ts, :].reshape(depth, bs, H_A, HEAD_DIM), sm[:, ts - 1::ts, :H_A], sr, sg,
        sconv,
    )
```

```python
import functools
import math
import types

import numpy as np
import jax
import jax.numpy as jnp
from jax import lax
from jax.experimental import pallas as pl
from jax.experimental.pallas import tpu as pltpu

f32 = jnp.float32
bf16 = jnp.bfloat16

D_MODEL = 2048
HEAD_DIM = 128
H_A, H_B, H_C = 4, 4, 8
D_A, D_B, D_C = H_A * HEAD_DIM, H_B * HEAD_DIM, H_C * HEAD_DIM
D_FF = 4 * D_MODEL
CONV_W = 4
PAST_LEN = 16384
ROPE_BASE = 10000.0
GATE_SOFTCAP = 15.0
NORM_EPS = 1e-6
QK_SCALE = HEAD_DIM ** -0.5

COL_A = 0
COL_B = 4 * D_A
COL_C = COL_B + 4 * D_B
N_PROJ = COL_C + 4 * D_C
LANE_I, LANE_F, LANE_DECAY, LANE_BETA = 0, H_A, 2 * H_A, 2 * H_A + H_C

PROMPT_CHUNK = 128
PROMPT_SEQS = 2
INV_BLOCK = 8
SAMPLE_BLOCK = 8
NEG_BIG = -1e30
VMEM_LIMIT = 60 * 1024 * 1024


def _cparams(sem):
    return pltpu.CompilerParams(dimension_semantics=sem, vmem_limit_bytes=VMEM_LIMIT)


def _rms(x, g):
    return x * lax.rsqrt(jnp.mean(x * x, axis=-1, keepdims=True) + NORM_EPS) * g


def _gather_rows_kernel(n_first, a_ref, b_ref, g_ref, xo_ref, ho_ref):
    x = jnp.where(pl.program_id(0) < n_first, a_ref[...], b_ref[...])
    xo_ref[...] = x
    ho_ref[...] = _rms(x, g_ref[...]).astype(ho_ref.dtype)


def _gather_rows(x_p, x_s, g, tm):
    d = x_p.shape[1]
    n_p, n_s = x_p.shape[0] // tm, x_s.shape[0] // tm
    m = (n_p + n_s) * tm
    row = lambda i: (i, 0)
    return pl.pallas_call(
        functools.partial(_gather_rows_kernel, n_p),
        grid=(n_p + n_s,),
        in_specs=[pl.BlockSpec((tm, d), lambda i: (jnp.minimum(i, n_p - 1), 0)),
                  pl.BlockSpec((tm, d), lambda i: (jnp.maximum(i - n_p, 0), 0)),
                  pl.BlockSpec((1, d), lambda i: (0, 0))],
        out_specs=[pl.BlockSpec((tm, d), row), pl.BlockSpec((tm, d), row)],
        out_shape=[jax.ShapeDtypeStruct((m, d), f32), jax.ShapeDtypeStruct((m, d), bf16)],
        compiler_params=_cparams(("parallel",)),
        name="gather_rows",
    )(x_p, x_s, g.reshape(1, d))


def _dot_tb(a, b):
    return lax.dot_general(a, b, (((1,), (1,)), ((), ())), preferred_element_type=f32)


def _mm_gates_kernel(a_ref, w_ref, o_ref):
    o_ref[...] = _dot_tb(a_ref[...], w_ref[...])


def _matmul_gates(h, w_t, tm):
    m, k = h.shape
    n = w_t.shape[0]
    return pl.pallas_call(
        _mm_gates_kernel,
        grid=(m // tm,),
        in_specs=[pl.BlockSpec((tm, k), lambda i: (i, 0)), pl.BlockSpec((n, k), lambda i: (0, 0))],
        out_specs=pl.BlockSpec((tm, n), lambda i: (i, 0)),
        out_shape=jax.ShapeDtypeStruct((m, n), f32),
        compiler_params=_cparams(("parallel",)),
        name="proj_gates",
    )(h, w_t)


def _mm_in_kernel(a_ref, w_ref, wo_ref, wu_ref, o_ref, wob_ref, wub_ref):
    o_ref[...] = _dot_tb(a_ref[...], w_ref[...])
    wob_ref[...] = wo_ref[...].astype(bf16)
    wub_ref[...] = wu_ref[...].astype(bf16)


def _matmul_in(h, w_t, w_out, w_up, layer, tm, tn):
    m, k = h.shape
    n = w_t.shape[0]
    nj, ni = n // tn, m // tm
    d = w_out.shape[1]
    f = w_up.shape[2]
    slab = d // (nj * ni)
    step = lambda j, i: j * ni + i
    side_in = lambda width: pl.BlockSpec((None, slab, width), lambda j, i: (layer, step(j, i), 0))
    side_out = lambda width: pl.BlockSpec((slab, width), lambda j, i: (step(j, i), 0))
    return pl.pallas_call(
        _mm_in_kernel,
        grid=(nj, ni),
        in_specs=[pl.BlockSpec((tm, k), lambda j, i: (i, 0)), pl.BlockSpec((tn, k), lambda j, i: (j, 0)),
                  side_in(d), side_in(f)],
        out_specs=[pl.BlockSpec((tm, tn), lambda j, i: (i, j)), side_out(d), side_out(f)],
        out_shape=[jax.ShapeDtypeStruct((m, n), f32), jax.ShapeDtypeStruct((d, d), bf16),
                   jax.ShapeDtypeStruct((d, f), bf16)],
        compiler_params=_cparams(("parallel", "arbitrary")),
        name="proj_in",
    )(h, w_t, w_out, w_up)


REPACK_LANES = 128


def _repack_slab(win_ref, main_ref, gate_ref):
    n_a = 4 * D_A
    n_ag = n_a + 2 * H_A
    n_bc = 4 * D_B + 4 * D_C
    main_ref[0:n_a, :] = win_ref[0:n_a, :].astype(bf16)
    main_ref[n_a:n_a + n_bc, :] = win_ref[n_ag:n_ag + n_bc, :].astype(bf16)
    gate_ref[...] = jnp.zeros_like(gate_ref)
    gate_ref[LANE_I:LANE_I + 2 * H_A, :] = win_ref[n_a:n_ag, :].astype(bf16)
    gate_ref[LANE_DECAY:LANE_DECAY + 2 * H_C, :] = win_ref[n_ag + n_bc:n_ag + n_bc + 2 * H_C, :].astype(bf16)


def _repack_part(win_ref, main_ref, gate_ref, part, parts):
    n_a = 4 * D_A
    n_ag = n_a + 2 * H_A
    n_bc = 4 * D_B + 4 * D_C
    runs_per_call = (N_PROJ // n_a) // parts
    for c in range(runs_per_call):
        dst = pl.multiple_of((part * runs_per_call + c) * n_a, 16)
        src = pl.multiple_of(dst + jnp.where(dst >= n_a, 2 * H_A, 0), 8)
        main_ref[pl.ds(dst, n_a), :] = win_ref[pl.ds(src, n_a), :].astype(bf16)
    gate_ref[...] = jnp.zeros_like(gate_ref)
    gate_ref[LANE_I:LANE_I + 2 * H_A, :] = win_ref[n_a:n_ag, :].astype(bf16)
    gate_ref[LANE_DECAY:LANE_DECAY + 2 * H_C, :] = win_ref[n_ag + n_bc:n_ag + n_bc + 2 * H_C, :].astype(bf16)


def _mm_up_kernel(repack_every, a_ref, w_ref, wd_ref, *rest):
    if repack_every:
        win_ref, o_ref, wdb_ref, main_ref, gate_ref = rest
        step = pl.program_id(0) * pl.num_programs(1) + pl.program_id(1)
        _repack_part(win_ref, main_ref, gate_ref, step % repack_every, repack_every)
    else:
        o_ref, wdb_ref = rest
    z = jnp.maximum(jnp.dot(a_ref[...], w_ref[...], preferred_element_type=f32), 0.0)
    o_ref[...] = (z * z).astype(o_ref.dtype)
    wdb_ref[...] = wd_ref[...].astype(bf16)


def _matmul_up(h, w, w_down, w_in_t, layer, tm, tn):
    m, k = h.shape
    n = w.shape[1]
    nj, ni = n // tn, m // tm
    steps = nj * ni
    step = lambda j, i: j * ni + i
    with_repack = layer + 1 < w_in_t.shape[0]
    dslab = w_down.shape[1] // steps
    d = w_down.shape[2]
    in_specs = [pl.BlockSpec((tm, k), lambda j, i: (i, 0)), pl.BlockSpec((k, tn), lambda j, i: (0, j)),
                pl.BlockSpec((None, dslab, d), lambda j, i: (layer, step(j, i), 0))]
    out_specs = [pl.BlockSpec((tm, tn), lambda j, i: (i, j)), pl.BlockSpec((dslab, d), lambda j, i: (step(j, i), 0))]
    out_shape = [jax.ShapeDtypeStruct((m, n), bf16), jax.ShapeDtypeStruct(w_down.shape[1:], bf16)]
    args = [h, w, w_down]
    every = 0
    if with_repack:
        n_in, dm = w_in_t.shape[1:]
        every = steps // (dm // REPACK_LANES)
        slab = lambda j, i: step(j, i) // every
        in_specs.append(pl.BlockSpec((None, n_in, REPACK_LANES), lambda j, i: (layer + 1, 0, slab(j, i))))
        out_specs += [pl.BlockSpec((N_PROJ, REPACK_LANES), lambda j, i: (0, slab(j, i))),
                      pl.BlockSpec((128, REPACK_LANES), lambda j, i: (0, slab(j, i)))]
        out_shape += [jax.ShapeDtypeStruct((N_PROJ, dm), bf16), jax.ShapeDtypeStruct((128, dm), bf16)]
        args.append(w_in_t)
    return pl.pallas_call(
        functools.partial(_mm_up_kernel, every),
        grid=(nj, ni),
        in_specs=in_specs,
        out_specs=out_specs,
        out_shape=out_shape,
        compiler_params=_cparams(("parallel", "arbitrary")),
        name="mlp_up",
    )(*args)


def _residual_epilogue(z, x_ref, gpost_ref, gnext_ref, xo_ref, ho_ref, rows=slice(None)):
    x_new = x_ref[rows, :] + _rms(z, gpost_ref[...])
    xo_ref[rows, :] = x_new
    ho_ref[rows, :] = _rms(x_new, gnext_ref[...]).astype(ho_ref.dtype)


OUT_ROW_PARTS = 4


def _mm_out_kernel(n_first, a_ref, b_ref, w_ref, x_ref, gpost_ref, gnext_ref, xo_ref, ho_ref):
    a = jnp.where(pl.program_id(0) < n_first, a_ref[...], b_ref[...])
    part = a.shape[0] // OUT_ROW_PARTS
    for r in range(OUT_ROW_PARTS):
        rows = slice(r * part, (r + 1) * part)
        z = jnp.dot(a[rows], w_ref[...], preferred_element_type=f32)
        _residual_epilogue(z, x_ref, gpost_ref, gnext_ref, xo_ref, ho_ref, rows)


def _matmul_out(y_p, y_s, w, x, g_post, g_next, tm):
    k = y_p.shape[1]
    n_p, n_s = y_p.shape[0] // tm, y_s.shape[0] // tm
    m = (n_p + n_s) * tm
    d = w.shape[1]
    row = lambda i: (i, 0)
    fixed = lambda i: (0, 0)
    return pl.pallas_call(
        functools.partial(_mm_out_kernel, n_p),
        grid=(n_p + n_s,),
        in_specs=[pl.BlockSpec((tm, k), lambda i: (jnp.minimum(i, n_p - 1), 0)),
                  pl.BlockSpec((tm, k), lambda i: (jnp.maximum(i - n_p, 0), 0)),
                  pl.BlockSpec((k, d), fixed), pl.BlockSpec((tm, d), row),
                  pl.BlockSpec((1, d), fixed), pl.BlockSpec((1, d), fixed)],
        out_specs=[pl.BlockSpec((tm, d), row), pl.BlockSpec((tm, d), row)],
        out_shape=[jax.ShapeDtypeStruct((m, d), f32), jax.ShapeDtypeStruct((m, d), bf16)],
        compiler_params=_cparams(("parallel",)),
        name="proj_out",
    )(y_p, y_s, w, x, g_post.reshape(1, d), g_next.reshape(1, d))


def _mm_down_kernel(a_ref, w_ref, x_ref, gpost_ref, gnext_ref, xo_ref, ho_ref):
    kk = pl.program_id(1)

    @pl.when(kk == 0)
    def _():
        xo_ref[...] = jnp.zeros_like(xo_ref)

    xo_ref[...] += jnp.dot(a_ref[...], w_ref[...], preferred_element_type=f32)

    @pl.when(kk == pl.num_programs(1) - 1)
    def _():
        _residual_epilogue(xo_ref[...], x_ref, gpost_ref, gnext_ref, xo_ref, ho_ref)


def _matmul_down(u, w, x, g_post, g_next, tm, tk):
    m, k = u.shape
    d = w.shape[1]
    row = lambda i, kk: (i, 0)
    fixed = lambda i, kk: (0, 0)
    return pl.pallas_call(
        _mm_down_kernel,
        grid=(m // tm, k // tk),
        in_specs=[pl.BlockSpec((tm, tk), lambda i, kk: (i, kk)),
                  pl.BlockSpec((tk, d), lambda i, kk: (kk, 0)),
                  pl.BlockSpec((tm, d), row), pl.BlockSpec((1, d), fixed), pl.BlockSpec((1, d), fixed)],
        out_specs=[pl.BlockSpec((tm, d), row), pl.BlockSpec((tm, d), row)],
        out_shape=[jax.ShapeDtypeStruct((m, d), f32), jax.ShapeDtypeStruct((m, d), bf16)],
        compiler_params=_cparams(("parallel", "arbitrary")),
        name="mlp_down",
    )(u, w, x, g_post.reshape(1, d), g_next.reshape(1, d))


def _dot(a, b):
    return jnp.dot(a, b, preferred_element_type=f32)


def _dot_ta(a, b):
    return lax.dot_general(a, b, (((0,), (0,)), ((), ())), preferred_element_type=f32)


def _lane_sum(x):
    return jnp.sum(x, axis=-1, keepdims=True)


def _softplus(z):
    return jnp.maximum(z, 0.0) + jnp.log1p(jnp.exp(-jnp.abs(z)))


def _gate_transform(gt, gp):
    z = gt + gp[0:1, :]
    capped = GATE_SOFTCAP * jnp.tanh(z / GATE_SOFTCAP)
    log_f = -_softplus(-capped)
    decay = -jnp.exp(gp[1:2, :]) * _softplus(z)
    beta = jax.nn.sigmoid(gt)
    return capped, log_f, decay, beta


def _head_norm(x, gain, center):
    if center:
        x = x - jnp.mean(x, axis=-1, keepdims=True)
    return x * lax.rsqrt(jnp.mean(x * x, axis=-1, keepdims=True) + NORM_EPS) * gain


def _l2norm(x):
    return x * lax.rsqrt(_lane_sum(x * x) + NORM_EPS)


def _silu(x):
    return x * jax.nn.sigmoid(x)


def _rotary(x, cos, sin_signed):
    return x * cos + pltpu.roll(x, HEAD_DIM // 2, 1) * sin_signed


def _retention_log_gamma():
    return np.log1p(-np.exp2(-5.0 - np.arange(H_B, dtype=np.float64)))


def _rope_tables(pos):
    half = HEAD_DIM // 2
    inv = ROPE_BASE ** (-np.arange(half, dtype=np.float64) / half)
    ang = np.asarray(pos, dtype=np.float64)[:, None] * inv[None, :]
    cos = np.concatenate([np.cos(ang), np.cos(ang)], axis=-1)
    sin = np.concatenate([-np.sin(ang), np.sin(ang)], axis=-1)
    return cos.astype(np.float32), sin.astype(np.float32)


def _scan_rows(x, row, length, op, fill):
    s = 1
    while s < length:
        x = op(x, jnp.where(row >= s, pltpu.roll(x, s, 0), fill))
        s *= 2
    return x


def _rows_to_lanes(x, length):
    if length < 128:
        x = jnp.concatenate([x, jnp.zeros((128 - length, 128), x.dtype)], axis=0)
    return x.T[:, :length]


def _prompt_mixer_kernel(L, NS, *refs):
    proj_refs, gt_refs = refs[:NS], refs[NS:2 * NS]
    (cos_ref, sin_ref, rd_ref, rq_ref, rk_ref, gp_ref, cw_ref, na_ref, nb_ref, nc_ref,
     y_ref, c_ref, n_ref, m_ref, r_ref, g_ref, conv_ref, ext_ref) = refs[2 * NS:]
    c = pl.program_id(1)

    @pl.when(c == 0)
    def _():
        c_ref[...] = jnp.zeros_like(c_ref)
        n_ref[...] = jnp.zeros_like(n_ref)
        m_ref[...] = jnp.zeros_like(m_ref)
        r_ref[...] = jnp.zeros_like(r_ref)
        g_ref[...] = jnp.zeros_like(g_ref)
        ext_ref[:, 0:8, :] = jnp.zeros((NS, 8, 3 * D_C), f32)

    row = lax.broadcasted_iota(jnp.int32, (L, 128), 0)
    lane = lax.broadcasted_iota(jnp.int32, (L, 128), 1)
    ri = lax.broadcasted_iota(jnp.int32, (L, L), 0)
    ci = lax.broadcasted_iota(jnp.int32, (L, L), 1)
    incl = ri >= ci
    strict = ri > ci
    head_lane = lane < H_A
    seqs = range(NS)

    def gate_block(s):
        g = types.SimpleNamespace()
        capped, log_f, decay, g.beta = _gate_transform(gt_refs[s][...], gp_ref[...])
        ig = jnp.where(head_lane, capped, 0.0)
        lf = jnp.where(head_lane, pltpu.roll(log_f, 128 - LANE_F, 1), 0.0)
        mp = m_ref[s]
        F = _scan_rows(lf, row, L, jnp.add, 0.0)
        m = F + jnp.maximum(mp, _scan_rows(ig - F, row, L, jnp.maximum, NEG_BIG))
        m_new = m[L - 1:L, :]
        f_last = F[L - 1:L, :]
        g.a_rows = F - m
        g.inter = jnp.exp(F + mp - m)
        g.inv_floor = jnp.exp(-m)
        g.wl = jnp.exp(ig + f_last - F - m_new)
        g.dec = jnp.exp(f_last + mp - m_new)
        m_ref[s] = m_new
        g.G = _scan_rows(decay, row, L, jnp.add, 0.0)
        g.b_lanes = g.g_lanes = _rows_to_lanes(jnp.where(head_lane, ig - F, g.G), L)
        g.gam = jnp.exp(g.G)
        g_last = g.G[L - 1:L, :]
        g.k_decay = jnp.exp(g_last - g.G)
        g.s_decay = jnp.exp(g_last)
        return g

    gs = [gate_block(s) for s in seqs]

    def mlstm_head(s, h):
        proj_ref, g = proj_refs[s], gs[s]
        q = proj_ref[:, COL_A + h * HEAD_DIM:COL_A + (h + 1) * HEAD_DIM]
        k = proj_ref[:, COL_A + D_A + h * HEAD_DIM:COL_A + D_A + (h + 1) * HEAD_DIM] * QK_SCALE
        v = proj_ref[:, COL_A + 2 * D_A + h * HEAD_DIM:COL_A + 2 * D_A + (h + 1) * HEAD_DIM]
        og = proj_ref[:, COL_A + 3 * D_A + h * HEAD_DIM:COL_A + 3 * D_A + (h + 1) * HEAD_DIM]
        logw = g.a_rows[:, h:h + 1] + g.b_lanes[h:h + 1, :]
        sc = _dot_tb(q, k) * jnp.exp(jnp.where(incl, logw, NEG_BIG))
        c_old = c_ref[s, h]
        n_old = n_ref[s, h:h + 1, :]
        inter_h = g.inter[:, h:h + 1]
        num = inter_h * _dot(q, c_old) + _dot(sc, v)
        den = inter_h * _lane_sum(q * n_old) + _lane_sum(sc)
        hh = num / jnp.maximum(jnp.abs(den), g.inv_floor[:, h:h + 1])
        kw = k * g.wl[:, h:h + 1]
        dec_h = g.dec[:, h:h + 1]
        c_ref[s, h] = dec_h * c_old + _dot_ta(kw, v)
        n_ref[s, h:h + 1, :] = dec_h * n_old + jnp.sum(kw, axis=0, keepdims=True)
        ya = jax.nn.sigmoid(og) * _head_norm(hh, na_ref[h:h + 1, :], False)
        y_ref[s, :, h * HEAD_DIM:(h + 1) * HEAD_DIM] = ya.astype(y_ref.dtype)

    lg = _retention_log_gamma()

    def retention_head(s, h):
        proj_ref = proj_refs[s]
        cos = cos_ref[...]
        sin = sin_ref[...]
        q = proj_ref[:, COL_B + h * HEAD_DIM:COL_B + (h + 1) * HEAD_DIM]
        k = proj_ref[:, COL_B + D_B + h * HEAD_DIM:COL_B + D_B + (h + 1) * HEAD_DIM]
        v = proj_ref[:, COL_B + 2 * D_B + h * HEAD_DIM:COL_B + 2 * D_B + (h + 1) * HEAD_DIM]
        gb = proj_ref[:, COL_B + 3 * D_B + h * HEAD_DIM:COL_B + 3 * D_B + (h + 1) * HEAD_DIM]
        qr = _rotary(q, cos, sin)
        kr = _rotary(k, cos, sin) * QK_SCALE
        s_old = r_ref[s, h]
        inner = _dot_tb(qr, kr) * rd_ref[h]
        o = _dot(inner, v) + rq_ref[h] * _dot(qr, s_old)
        r_ref[s, h] = float(np.exp(L * lg[h])) * s_old + _dot_ta(kr * rk_ref[h], v)
        yb = _silu(gb) * _head_norm(o, nb_ref[h:h + 1, :], True)
        y_ref[s, :, D_A + h * HEAD_DIM:D_A + (h + 1) * HEAD_DIM] = yb.astype(y_ref.dtype)

    fillers = [functools.partial(mlstm_head, s, h) for h in range(H_A) for s in seqs]
    fillers += [functools.partial(retention_head, s, h) for h in range(H_B) for s in seqs]

    def emit_fillers(count):
        for _ in range(min(count, len(fillers))):
            fillers.pop(0)()

    n_stages = max(int(math.log2(min(INV_BLOCK, L))) - 1, 0) + int(math.log2(L // min(INV_BLOCK, L)))
    per_stage = (3 * len(fillers) // 4) // max(n_stages, 1)

    for s in seqs:
        ext_ref[s, 8:8 + L, :] = proj_refs[s][:, COL_C:COL_C + 3 * D_C]

    def conv_block(s, col):
        acc = cw_ref[CONV_W - 1:CONV_W, col:col + HEAD_DIM] * ext_ref[s, 8:8 + L, col:col + HEAD_DIM]
        for w in range(CONV_W - 1):
            off = 8 - (CONV_W - 1) + w
            acc = acc + cw_ref[w:w + 1, col:col + HEAD_DIM] * ext_ref[s, off:off + L, col:col + HEAD_DIM]
        return _silu(acc)

    pairs = [(s, h) for h in range(H_C) for s in seqs]
    heads = range(len(pairs))
    dcol = [LANE_DECAY + h for _, h in pairs]
    gam = [gs[s].gam for s, _ in pairs]
    beta_c = [gs[s].beta[:, LANE_BETA + h:LANE_BETA + h + 1] for s, h in pairs]
    q = [_l2norm(conv_block(s, h * HEAD_DIM)) * QK_SCALE for s, h in pairs]
    k = [_l2norm(conv_block(s, D_C + h * HEAD_DIM)) for s, h in pairs]
    v = [conv_block(s, 2 * D_C + h * HEAD_DIM) for s, h in pairs]
    dec_in = [jnp.exp(jnp.where(incl, gs[s].G[:, LANE_DECAY + h:LANE_DECAY + h + 1]
                                - gs[s].g_lanes[LANE_DECAY + h:LANE_DECAY + h + 1, :], NEG_BIG)) for s, h in pairs]
    a_mat = [jnp.where(strict, dec_in[h], 0.0) * beta_c[h] * _dot_tb(k[h], k[h]) for h in heads]
    qk = [_dot_tb(q[h], k[h]) * dec_in[h] for h in heads]
    blk = min(INV_BLOCK, L)
    same = (ri // blk) == (ci // blk)
    pw = [jnp.where(same, -a, 0.0) for a in a_mat]
    e_mat = list(pw)
    span = 1
    while 2 * span < blk:
        pw = [_dot(p, p) for p in pw]
        e_mat = [e_mat[h] + pw[h] + _dot(e_mat[h], pw[h]) for h in heads]
        emit_fillers(per_stage)
        span *= 2
    while blk < L:
        wider = (ri // (2 * blk)) == (ci // (2 * blk))
        a_off = [jnp.where(wider & jnp.logical_not(same), a, 0.0) for a in a_mat]
        low = [a_off[h] + _dot(e_mat[h], a_off[h]) for h in heads]
        e_mat = [e_mat[h] - low[h] - _dot(low[h], e_mat[h]) for h in heads]
        emit_fillers(per_stage)
        same = wider
        blk *= 2
    rhs = [jnp.concatenate([(beta_c[h] * gam[h][:, dcol[h]:dcol[h] + 1]) * k[h], beta_c[h] * v[h]], axis=1)
           for h in heads]
    sol = [rhs[h] + _dot(e_mat[h], rhs[h]) for h in heads]
    emit_fillers(len(fillers))
    s_old = [g_ref[s, hd] for s, hd in pairs]
    both = [_dot(jnp.concatenate([sol[h][:, :HEAD_DIM], q[h]], axis=0), s_old[h]) for h in heads]
    u = [sol[h][:, HEAD_DIM:] - both[h][:L] for h in heads]
    o = [gam[h][:, dcol[h]:dcol[h] + 1] * both[h][L:] + _dot(qk[h], u[h]) for h in heads]
    for h, (s, hd) in enumerate(pairs):
        g_ref[s, hd] = (gs[s].s_decay[:, dcol[h]:dcol[h] + 1] * s_old[h]
                        + _dot_ta(k[h] * gs[s].k_decay[:, dcol[h]:dcol[h] + 1], u[h]))
    for h, (s, hd) in enumerate(pairs):
        gz = proj_refs[s][:, COL_C + 3 * D_C + hd * HEAD_DIM:COL_C + 3 * D_C + (hd + 1) * HEAD_DIM]
        yc = _silu(gz) * _head_norm(o[h], nc_ref[...], False)
        y_ref[s, :, D_A + D_B + hd * HEAD_DIM:D_A + D_B + (hd + 1) * HEAD_DIM] = yc.astype(y_ref.dtype)

    for s in seqs:
        ext_ref[s, 0:8, :] = ext_ref[s, L:L + 8, :]

    @pl.when(c == pl.num_programs(1) - 1)
    def _():
        for s in seqs:
            conv_ref[s] = ext_ref[s, 8 + L - (CONV_W - 1):8 + L, :]


def _prompt_tables(seq, L):
    lg = _retention_log_gamma()
    i = np.arange(L, dtype=np.float64)
    diff = i[:, None] - i[None, :]
    rd = np.where(diff >= 0, np.exp(np.maximum(diff, 0.0) * lg[:, None, None]), 0.0)
    rq = np.broadcast_to(np.exp((i + 1.0) * lg[:, None])[..., None], (H_B, L, HEAD_DIM))
    rk = np.broadcast_to(np.exp((L - 1.0 - i) * lg[:, None])[..., None], (H_B, L, HEAD_DIM))
    cos, sin = _rope_tables(np.arange(seq))
    return (jnp.asarray(cos), jnp.asarray(sin), jnp.asarray(rd, f32), jnp.asarray(rq, f32), jnp.asarray(rk, f32))


def _prompt_mixers(proj, gates, tables, gp, cw, na, nb, nc, batch, seq, L, NS):
    nchunk = seq // L
    cos, sin, rd, rq, rk = tables
    full = lambda shape: pl.BlockSpec(shape, lambda b, c: (0,) * len(shape))
    state4 = lambda heads: pl.BlockSpec((NS, heads, HEAD_DIM, HEAD_DIM), lambda b, c: (b, 0, 0, 0))
    rows_of = lambda s, width: pl.BlockSpec((L, width), lambda b, c: ((b * NS + s) * nchunk + c, 0))
    outs = pl.pallas_call(
        functools.partial(_prompt_mixer_kernel, L, NS),
        grid=(batch // NS, nchunk),
        in_specs=[rows_of(s, N_PROJ) for s in range(NS)] + [rows_of(s, 128) for s in range(NS)] + [
            pl.BlockSpec((L, HEAD_DIM), lambda b, c: (c, 0)),
            pl.BlockSpec((L, HEAD_DIM), lambda b, c: (c, 0)),
            full((H_B, L, L)), full((H_B, L, HEAD_DIM)), full((H_B, L, HEAD_DIM)),
            full((8, 128)), full((CONV_W, 3 * D_C)),
            full((H_A, HEAD_DIM)), full((H_B, HEAD_DIM)), full((1, HEAD_DIM)),
        ],
        out_specs=[
            pl.BlockSpec((None, NS, None, L, D_MODEL), lambda b, c: (b, 0, c, 0, 0)),
            state4(H_A),
            pl.BlockSpec((NS, H_A, HEAD_DIM), lambda b, c: (b, 0, 0)),
            pl.BlockSpec((NS, 1, 128), lambda b, c: (b, 0, 0)),
            state4(H_B),
            state4(H_C),
            pl.BlockSpec((NS, CONV_W - 1, 3 * D_C), lambda b, c: (b, 0, 0)),
        ],
        out_shape=[
            jax.ShapeDtypeStruct((batch // NS, NS, nchunk, L, D_MODEL), bf16),
            jax.ShapeDtypeStruct((batch, H_A, HEAD_DIM, HEAD_DIM), f32),
            jax.ShapeDtypeStruct((batch, H_A, HEAD_DIM), f32),
            jax.ShapeDtypeStruct((batch, 1, 128), f32),
            jax.ShapeDtypeStruct((batch, H_B, HEAD_DIM, HEAD_DIM), f32),
            jax.ShapeDtypeStruct((batch, H_C, HEAD_DIM, HEAD_DIM), f32),
            jax.ShapeDtypeStruct((batch, CONV_W - 1, 3 * D_C), f32),
        ],
        scratch_shapes=[pltpu.VMEM((NS, L + 8, 3 * D_C), f32)],
        compiler_params=_cparams(("parallel", "arbitrary")),
        name="prompt_mixers",
    )(*([proj] * NS + [gates] * NS), cos, sin, rd, rq, rk, gp, cw, na, nb, nc)
    return [outs[0].reshape(batch * seq, D_MODEL)] + list(outs[1:])


def _sample_mixer_kernel(T, BB, layer_first, *refs):
    (proj_ref, gt_ref, cos_ref, sin_ref, rt_ref, gp_ref, cw_ref, na_ref, nb_ref, nc_ref,
     c0_ref, nrep_ref, mrep_ref, r0_ref, g0_ref, conv0_ref) = refs[:16]
    rest = refs[16 + (0 if layer_first else 6):]
    (y_ref, c_ref, nout_ref, mout_ref, r_ref, g_ref, conv_ref, cv_ref, sa_ref, ext_ref) = rest
    R = T * BB
    row = lax.broadcasted_iota(jnp.int32, (R, 128), 0)
    lane = lax.broadcasted_iota(jnp.int32, (R, 128), 1)
    t = lax.rem(row, T)
    tcol = t[:, 0:1]

    def shift(x, s):
        return x if s == 0 else pltpu.roll(x, s, 0)

    def seg_scan(x, op, fill):
        s = 1
        while s < T:
            x = op(x, jnp.where(t >= s, shift(x, s), fill))
            s *= 2
        return x

    def last_rep(x):
        x_last = jnp.where(t == T - 1, x, 0.0)
        out = x_last
        for s in range(1, T):
            out = out + pltpu.roll(x_last, R - s, 0)
        return out

    capped, log_f, decay, beta = _gate_transform(gt_ref[...], gp_ref[...])

    head_lane = lane < H_A
    ig = jnp.where(head_lane, capped, 0.0)
    lf = jnp.where(head_lane, pltpu.roll(log_f, 128 - LANE_F, 1), 0.0)
    mp = mrep_ref[...]
    F = seg_scan(lf, jnp.add, 0.0)
    m = F + jnp.maximum(mp, seg_scan(ig - F, jnp.maximum, NEG_BIG))
    m_new = last_rep(m)
    f_last = last_rep(F)
    inter = jnp.exp(F + mp - m)
    inv_floor = jnp.exp(-m)
    wl = jnp.exp(ig + f_last - F - m_new)
    dec = jnp.exp(f_last + mp - m_new)
    mout_ref[...] = m_new
    pw = [jnp.where(t >= s, jnp.exp(F - shift(F, s) + shift(ig, s) - m), 0.0) for s in range(T)]

    def seq(x, b):
        return x[b * T:(b + 1) * T]

    slot_a, slot_b, slot_w, slot_q = 0, H_A, H_A + H_B, H_A + H_B + H_C

    qa = [proj_ref[:, COL_A + h * HEAD_DIM:COL_A + (h + 1) * HEAD_DIM] for h in range(H_A)]
    for h in range(H_A):
        for b in range(BB):
            sa_ref[slot_a + h, b * T:(b + 1) * T, :] = _dot(seq(qa[h], b), c0_ref[b, h])
    cos = cos_ref[...]
    sin = sin_ref[...]
    lg = _retention_log_gamma()
    qr = [_rotary(proj_ref[:, COL_B + h * HEAD_DIM:COL_B + (h + 1) * HEAD_DIM], cos, sin) for h in range(H_B)]
    kr = [_rotary(proj_ref[:, COL_B + D_B + h * HEAD_DIM:COL_B + D_B + (h + 1) * HEAD_DIM], cos, sin) * QK_SCALE
          for h in range(H_B)]
    for h in range(H_B):
        for b in range(BB):
            sa_ref[slot_b + h, b * T:(b + 1) * T, :] = _dot(seq(qr[h], b), r0_ref[b, h])

    ka = [proj_ref[:, COL_A + D_A + h * HEAD_DIM:COL_A + D_A + (h + 1) * HEAD_DIM] * QK_SCALE for h in range(H_A)]
    va = [proj_ref[:, COL_A + 2 * D_A + h * HEAD_DIM:COL_A + 2 * D_A + (h + 1) * HEAD_DIM] for h in range(H_A)]
    n_old = [nrep_ref[:, h * HEAD_DIM:(h + 1) * HEAD_DIM] for h in range(H_A)]
    kw = [ka[h] * wl[:, h:h + 1] for h in range(H_A)]
    for h in range(H_A):
        for b in range(BB):
            dec_bh = dec[b * T:b * T + 1, h:h + 1]
            c_ref[b, h] = dec_bh * c0_ref[b, h] + _dot_ta(seq(kw[h], b), seq(va[h], b))
        nout_ref[:, h * HEAD_DIM:(h + 1) * HEAD_DIM] = dec[:, h:h + 1] * n_old[h] + seg_scan(kw[h], jnp.add, 0.0)
    vb = [proj_ref[:, COL_B + 2 * D_B + h * HEAD_DIM:COL_B + 2 * D_B + (h + 1) * HEAD_DIM] for h in range(H_B)]
    for h in range(H_B):
        kd = kr[h] * rt_ref[:, H_B + h:H_B + h + 1]
        for b in range(BB):
            r_ref[b, h] = float(np.exp(T * lg[h])) * r0_ref[b, h] + _dot_ta(seq(kd, b), seq(vb[h], b))

    for b in range(BB):
        ext_ref[b, 0:CONV_W - 1, :] = conv0_ref[b]
        ext_ref[b, CONV_W - 1:CONV_W - 1 + T, :] = proj_ref[b * T:(b + 1) * T, COL_C:COL_C + 3 * D_C]
        acc = cw_ref[0:1, :] * ext_ref[b, 0:T, :]
        for w in range(1, CONV_W):
            acc = acc + cw_ref[w:w + 1, :] * ext_ref[b, w:w + T, :]
        cv_ref[b * T:(b + 1) * T, :] = _silu(acc)
        conv_ref[b] = ext_ref[b, T:T + CONV_W - 1, :]

    G = seg_scan(decay, jnp.add, 0.0)
    gam = jnp.exp(G)
    g_last = last_rep(G)
    k_decay = jnp.exp(g_last - G)
    s_decay = jnp.exp(g_last)
    dshift = [None] + [jnp.where(t >= s, jnp.exp(G - shift(G, s)), 0.0) for s in range(1, T)]

    heads = range(H_C)
    dcol = [LANE_DECAY + h for h in heads]
    beta_c = [beta[:, LANE_BETA + h:LANE_BETA + h + 1] for h in heads]
    qg = [_l2norm(cv_ref[:, h * HEAD_DIM:(h + 1) * HEAD_DIM]) * QK_SCALE for h in heads]
    kg = [_l2norm(cv_ref[:, D_C + h * HEAD_DIM:D_C + (h + 1) * HEAD_DIM]) for h in heads]
    vg = [cv_ref[:, 2 * D_C + h * HEAD_DIM:2 * D_C + (h + 1) * HEAD_DIM] for h in heads]
    a_sub = [[None] + [beta_c[h] * dshift[s][:, dcol[h]:dcol[h] + 1] * _lane_sum(kg[h] * shift(kg[h], s))
                       for s in range(1, T)] for h in heads]
    rhs_w = [(beta_c[h] * gam[:, dcol[h]:dcol[h] + 1]) * kg[h] for h in heads]
    rhs_u = [beta_c[h] * vg[h] for h in heads]
    w_sol, u_sol = list(rhs_w), list(rhs_u)
    for i in range(1, T):
        for h in heads:
            upd_w = a_sub[h][1] * shift(w_sol[h], 1)
            upd_u = a_sub[h][1] * shift(u_sol[h], 1)
            for s in range(2, i + 1):
                upd_w = upd_w + a_sub[h][s] * shift(w_sol[h], s)
                upd_u = upd_u + a_sub[h][s] * shift(u_sol[h], s)
            w_sol[h] = jnp.where(t == i, rhs_w[h] - upd_w, w_sol[h])
            u_sol[h] = jnp.where(t == i, rhs_u[h] - upd_u, u_sol[h])
    for h in heads:
        for b in range(BB):
            both = _dot(jnp.concatenate([seq(w_sol[h], b), seq(qg[h], b)], axis=0), g0_ref[b, h])
            sa_ref[slot_w + h, b * T:(b + 1) * T, :] = both[:T]
            sa_ref[slot_q + h, b * T:(b + 1) * T, :] = both[T:]

    ug = [u_sol[h] - sa_ref[slot_w + h] for h in heads]
    for h in heads:
        kd = kg[h] * k_decay[:, dcol[h]:dcol[h] + 1]
        for b in range(BB):
            g_ref[b, h] = (s_decay[b * T:b * T + 1, dcol[h]:dcol[h] + 1] * g0_ref[b, h]
                           + _dot_ta(seq(kd, b), seq(ug[h], b)))

    sc_a = [[_lane_sum(qa[h] * shift(ka[h], s)) * pw[s][:, h:h + 1] for s in range(T)] for h in range(H_A)]
    qn = [_lane_sum(qa[h] * n_old[h]) for h in range(H_A)]
    sc_b = [[jnp.where(tcol >= s, _lane_sum(qr[h] * shift(kr[h], s)) * float(np.exp(s * lg[h])), 0.0)
             for s in range(T)] for h in range(H_B)]
    sc_c = [[_lane_sum(qg[h] * shift(kg[h], s)) * (1.0 if s == 0 else dshift[s][:, dcol[h]:dcol[h] + 1])
             for s in range(T)] for h in heads]
    hid_a, hid_b, hid_c = [], [], []
    for h in range(H_A):
        inter_h = inter[:, h:h + 1]
        num = inter_h * sa_ref[slot_a + h]
        den = inter_h * qn[h]
        for s in range(T):
            num = num + sc_a[h][s] * shift(va[h], s)
            den = den + sc_a[h][s]
        hid_a.append(num / jnp.maximum(jnp.abs(den), inv_floor[:, h:h + 1]))
    for h in range(H_B):
        o = rt_ref[:, h:h + 1] * sa_ref[slot_b + h]
        for s in range(T):
            o = o + sc_b[h][s] * shift(vb[h], s)
        hid_b.append(o)
    for h in heads:
        o = gam[:, dcol[h]:dcol[h] + 1] * sa_ref[slot_q + h]
        for s in range(T):
            o = o + sc_c[h][s] * shift(ug[h], s)
        hid_c.append(o)
    mean_b = [jnp.mean(x, axis=-1, keepdims=True) for x in hid_b]
    hid_b = [x - mu for x, mu in zip(hid_b, mean_b)]
    hidden = hid_a + hid_b + hid_c
    inv_rms = [lax.rsqrt(jnp.mean(x * x, axis=-1, keepdims=True) + NORM_EPS) for x in hidden]
    for h in range(H_A):
        og = proj_ref[:, COL_A + 3 * D_A + h * HEAD_DIM:COL_A + 3 * D_A + (h + 1) * HEAD_DIM]
        ya = jax.nn.sigmoid(og) * (hidden[h] * inv_rms[h] * na_ref[h:h + 1, :])
        y_ref[:, h * HEAD_DIM:(h + 1) * HEAD_DIM] = ya.astype(y_ref.dtype)
    for h in range(H_B):
        gb = proj_ref[:, COL_B + 3 * D_B + h * HEAD_DIM:COL_B + 3 * D_B + (h + 1) * HEAD_DIM]
        yb = _silu(gb) * (hidden[H_A + h] * inv_rms[H_A + h] * nb_ref[h:h + 1, :])
        y_ref[:, D_A + h * HEAD_DIM:D_A + (h + 1) * HEAD_DIM] = yb.astype(y_ref.dtype)
    for h in heads:
        gz = proj_ref[:, COL_C + 3 * D_C + h * HEAD_DIM:COL_C + 3 * D_C + (h + 1) * HEAD_DIM]
        yc = _silu(gz) * (hidden[H_A + H_B + h] * inv_rms[H_A + H_B + h] * nc_ref[...])
        y_ref[:, D_A + D_B + h * HEAD_DIM:D_A + D_B + (h + 1) * HEAD_DIM] = yc.astype(y_ref.dtype)


def _sample_tables(T, BB):
    lg = _retention_log_gamma()
    tt = np.arange(T, dtype=np.float64)
    rt = np.zeros((T, 128), np.float64)
    rt[:, 0:H_B] = np.exp((tt[:, None] + 1.0) * lg[None, :])
    rt[:, H_B:2 * H_B] = np.exp((T - 1.0 - tt[:, None]) * lg[None, :])
    cos, sin = _rope_tables(PAST_LEN + np.arange(T))
    rep = lambda a: jnp.asarray(np.tile(a, (BB, 1)), f32)
    return rep(cos), rep(sin), rep(rt)


def _sample_mixers(layer, proj, gates, tables, gp, cw, na, nb, nc, states, prev_out, batch, T, row0, BB):
    c0, nrep, mrep, r0, g0, conv0 = states
    R = T * BB
    nblk = batch // BB
    blk0 = row0 // R
    cos, sin, rt = tables
    full = lambda shape: pl.BlockSpec(shape, lambda i: (0,) * len(shape))
    st4 = lambda heads: pl.BlockSpec((None, BB, heads, HEAD_DIM, HEAD_DIM), lambda i: (layer, i, 0, 0, 0))
    rows = lambda width: pl.BlockSpec((None, R, width), lambda i: (layer, i, 0))
    convspec = pl.BlockSpec((None, BB, CONV_W - 1, 3 * D_C), lambda i: (layer, i, 0, 0))
    anyspec = pl.BlockSpec(memory_space=pl.ANY)
    in_specs = [
        pl.BlockSpec((R, N_PROJ), lambda i: (blk0 + i, 0)),
        pl.BlockSpec((R, 128), lambda i: (blk0 + i, 0)),
        full((R, HEAD_DIM)), full((R, HEAD_DIM)), full((R, 128)),
        full((8, 128)), full((CONV_W, 3 * D_C)),
        full((H_A, HEAD_DIM)), full((H_B, HEAD_DIM)), full((1, HEAD_DIM)),
        st4(H_A), rows(D_A), rows(128), st4(H_B), st4(H_C), convspec,
    ]
    args = [proj, gates, cos, sin, rt, gp, cw, na, nb, nc, c0, nrep, mrep, r0, g0, conv0]
    out_shape = [
        jax.ShapeDtypeStruct((batch * T, D_MODEL), bf16),
        jax.ShapeDtypeStruct(c0.shape, f32),
        jax.ShapeDtypeStruct(nrep.shape, f32),
        jax.ShapeDtypeStruct(mrep.shape, f32),
        jax.ShapeDtypeStruct(r0.shape, f32),
        jax.ShapeDtypeStruct(g0.shape, f32),
        jax.ShapeDtypeStruct(conv0.shape, f32),
    ]
    out_specs = [pl.BlockSpec((R, D_MODEL), lambda i: (i, 0)),
                 st4(H_A), rows(D_A), rows(128), st4(H_B), st4(H_C), convspec]
    first = prev_out is None
    aliases = {}
    if not first:
        in_specs += [anyspec] * 6
        args += list(prev_out)
        aliases = {16 + j: 1 + j for j in range(6)}
    return pl.pallas_call(
        functools.partial(_sample_mixer_kernel, T, BB, first),
        grid=(nblk,),
        in_specs=in_specs,
        out_specs=out_specs,
        out_shape=out_shape,
        input_output_aliases=aliases,
        scratch_shapes=[pltpu.VMEM((R, 3 * D_C), f32), pltpu.VMEM((H_A + H_B + 2 * H_C, R, HEAD_DIM), f32),
                        pltpu.VMEM((BB, 8, 3 * D_C), f32)],
        compiler_params=_cparams(("parallel",)),
        name="sample_mixers",
    )(*args)


def _repack_kernel(win_ref, main_ref, gate_ref):
    _repack_slab(win_ref, main_ref, gate_ref)


def _repack_w_in(w_in_t, layer):
    _, n_in, dm = w_in_t.shape
    return pl.pallas_call(
        _repack_kernel,
        grid=(dm // REPACK_LANES,),
        in_specs=[pl.BlockSpec((None, n_in, REPACK_LANES), lambda i: (layer, 0, i))],
        out_specs=[pl.BlockSpec((N_PROJ, REPACK_LANES), lambda i: (0, i)),
                   pl.BlockSpec((128, REPACK_LANES), lambda i: (0, i))],
        out_shape=[jax.ShapeDtypeStruct((N_PROJ, dm), bf16), jax.ShapeDtypeStruct((128, dm), bf16)],
        compiler_params=_cparams(("parallel",)),
        name="repack_w_in",
    )(w_in_t)


def _gate_params(gate_bias, dt_bias, a_log):
    depth = gate_bias.shape[0]
    gp = jnp.zeros((depth, 8, 128), f32)
    gp = gp.at[:, 0, LANE_I:LANE_I + 2 * H_A].set(gate_bias)
    gp = gp.at[:, 0, LANE_DECAY:LANE_DECAY + H_C].set(dt_bias)
    gp = gp.at[:, 1, LANE_DECAY:LANE_DECAY + H_C].set(a_log)
    return gp


def kernel(x_prompt, x_sample, state_mlstm_C, state_mlstm_n, state_mlstm_m, state_ret_S, state_gdn_S,
           state_gdn_conv, norm_mix_pre, norm_mix_post, norm_mlp_pre, norm_mlp_post, w_in, mlstm_gate_bias,
           gdn_conv_w, gdn_A_log, gdn_dt_bias, norm_mlstm, norm_ret, norm_gdn, w_out, w_up, w_down):
    bp, tp, d = x_prompt.shape
    bs, ts, _ = x_sample.shape
    depth = w_in.shape[0]
    rows_p = bp * tp
    rows_s = bs * ts
    rows = rows_p + rows_s

    w_in_t = jnp.swapaxes(w_in, 1, 2)
    w_in_p, w_gate_p = _repack_w_in(w_in_t, 0)
    gp = _gate_params(mlstm_gate_bias, gdn_dt_bias, gdn_A_log)
    na = norm_mlstm.reshape(depth, H_A, HEAD_DIM)
    nb = norm_ret.reshape(depth, H_B, HEAD_DIM)
    nc = norm_gdn.reshape(depth, 1, HEAD_DIM)

    nrep = jnp.repeat(state_mlstm_n.reshape(depth, bs, D_A), ts, axis=1)
    mrep = jnp.pad(jnp.repeat(state_mlstm_m, ts, axis=1), ((0, 0), (0, 0), (0, 128 - H_A)))
    s_states = (state_mlstm_C, nrep, mrep, state_ret_S, state_gdn_S, state_gdn_conv)

    p_tables = _prompt_tables(tp, PROMPT_CHUNK)
    s_tables = _sample_tables(ts, SAMPLE_BLOCK)

    tm = rows // 8
    x, h = _gather_rows(x_prompt.reshape(rows_p, d), x_sample.reshape(rows_s, d), norm_mix_pre[0], rows_s)
    p_states = []
    s_out = None
    for l in range(depth):
        proj, w_out_b, w_up_b = _matmul_in(h, w_in_p, w_out, w_up, l, tm, 2048)
        gates = _matmul_gates(h, w_gate_p, tm)
        outs = _prompt_mixers(proj, gates, p_tables, gp[l], gdn_conv_w[l], na[l], nb[l], nc[l], bp, tp,
                              PROMPT_CHUNK, PROMPT_SEQS)
        p_states.append(outs[1:])
        res = _sample_mixers(l, proj, gates, s_tables, gp[l], gdn_conv_w[l], na[l], nb[l], nc[l],
                             s_states, s_out, bs, ts, rows_p, SAMPLE_BLOCK)
        y_s, s_out = res[0], res[1:]
        x, h = _matmul_out(outs[0], y_s, w_out_b, x, norm_mix_post[l], norm_mlp_pre[l], rows_s)
        up = _matmul_up(h, w_up_b, w_down, w_in_t, l, tm, 2048)
        u, w_down_b = up[0], up[1]
        if l + 1 < depth:
            w_in_p, w_gate_p = up[2], up[3]
        x, h = _matmul_down(u, w_down_b, x, norm_mlp_post[l], norm_mix_pre[(l + 1) % depth], tm, 1024)

    stk = lambda j: jnp.stack([s[j] for s in p_states], axis=0)
    sc, sn, sm, sr, sg, sconv = s_out
    return (
        x[:rows_p].reshape(bp, tp, d), x[rows_p:].reshape(bs, ts, d),
        stk(0), stk(1), stk(2)[:, :, 0, :H_A], stk(3), stk(4), stk(5),
        sc, sn[:, ts - 1::ts, :].reshape(depth, bs, H_A, HEAD_DIM), sm[:, ts - 1::ts, :H_A], sr, sg,
        sconv,
    )
```
